```python
import math
import jax
import jax.numpy as jnp
from jax import lax
import numpy as np

D_MODEL = 4096
BATCH = 8
SEQ = 2048
DEPTH = 2

CTX_LEN = 256
GRID_W = 64
N_EVEN = (DEPTH + 1) // 2
N_ODD = DEPTH // 2
EPS = 1e-6
N_MOD = 6

CONV_D = D_MODEL // 2
CONV_WIDTH = 31

SSM_D = D_MODEL // 2
SSM_HEAD_DIM = 64
SSM_HEADS = SSM_D // SSM_HEAD_DIM
SSM_GROUPS = 8
SSM_STATE = 128
SSM_CONV_WIDTH = 5
SSM_CHUNK = 128
XBC_DIM = SSM_D + 2 * SSM_GROUPS * SSM_STATE
IN0_SPLITS = [CONV_D, 2 * CONV_D, 2 * CONV_D + SSM_D, 2 * CONV_D + SSM_D + XBC_DIM]
IN0_DIM = 2 * CONV_D + SSM_D + XBC_DIM + 2 * SSM_HEADS
MIX0_DIM = CONV_D + SSM_D

ATT_HEADS = 32
ATT_KV_HEADS = 8
ATT_HEAD_DIM = 128
ATT_REP = ATT_HEADS // ATT_KV_HEADS
Q_COLS = ATT_HEADS * ATT_HEAD_DIM
KV_COLS = ATT_KV_HEADS * ATT_HEAD_DIM
QKV_DIM = Q_COLS + 2 * KV_COLS
Q_BLOCK = 128
ROPE_HALF = ATT_HEAD_DIM // 2
ROPE_AXIS_FREQS = ROPE_HALF // 2
ROPE_THETA = 10000.0

MOE_GROUPS = 4
MOE_EXPERTS = 8
MOE_TOPK = 2
MOE_HIDDEN = D_MODEL // 8

kernel_name = 'hybrid_conv_ssd_gqa_hmoe_dit'


def rms_norm(t, g):
    tf = t.astype(jnp.float32)
    tf = tf * lax.rsqrt(jnp.mean(tf * tf, axis=-1, keepdims=True) + EPS)
    return tf.astype(t.dtype) * g


def layer_norm(t, g, b):
    tf = t.astype(jnp.float32)
    tc = tf - jnp.mean(tf, axis=-1, keepdims=True)
    tf = tc * lax.rsqrt(jnp.mean(tc * tc, axis=-1, keepdims=True) + EPS)
    return tf.astype(t.dtype) * g + b


def modulate(t, shift, scale):
    return t * (1 + scale) + shift


def dwconv(t, w):
    k = w.shape[0]
    return lax.conv_general_dilated(
        t, w[:, None, :].astype(t.dtype), window_strides=(1,),
        padding=((k // 2, k // 2),), dimension_numbers=('NWC', 'WIO', 'NWC'),
        feature_group_count=t.shape[-1])


def segsum(a):
    n = a.shape[-1]
    strict = jnp.tril(jnp.ones((n, n), dtype=bool), k=-1)
    incl = jnp.tril(jnp.ones((n, n), dtype=bool))
    a_ij = jnp.where(strict, a[..., :, None], 0.0)
    return jnp.where(incl, jnp.cumsum(a_ij, axis=-2), -jnp.inf)


def ssd_scan(x, dt, a_neg, bm, cm, init_state):
    b, l, h, p = x.shape
    g, n = bm.shape[-2:]
    r = h // g
    nc = l // SSM_CHUNK
    xd = (x * dt[..., None]).reshape(b, nc, SSM_CHUNK, g, r, p)
    da = jnp.moveaxis((dt * a_neg).reshape(b, nc, SSM_CHUNK, g, r), 2, -1)
    bm = bm.reshape(b, nc, SSM_CHUNK, g, n)
    cm = cm.reshape(b, nc, SSM_CHUNK, g, n)
    cum = jnp.cumsum(da, axis=-1)
    lmat = jnp.exp(segsum(da))
    cb = jnp.einsum('bclgn,bcsgn->bcgls', cm, bm)
    y_diag = jnp.einsum('bcgls,bcgrls,bcsgrp->bclgrp', cb, lmat, xd)
    decay_states = jnp.exp(cum[..., -1:] - cum)
    states = jnp.einsum('bclgn,bcgrl,bclgrp->bcgrpn', bm, decay_states, xd)
    init = init_state.reshape(b, 1, g, r, p, n).astype(states.dtype)
    states = jnp.concatenate([init, states], axis=1)
    last = jnp.pad(jnp.moveaxis(cum[..., -1], 1, -1), ((0, 0), (0, 0), (0, 0), (1, 0)))
    chunk_decay = jnp.exp(segsum(last))
    new_states = jnp.einsum('bgrzk,bkgrpn->bzgrpn', chunk_decay, states)
    states_in, final = new_states[:, :-1], new_states[:, -1]
    y_off = jnp.einsum('bclgn,bcgrpn,bcgrl->bclgrp', cm, states_in, jnp.exp(cum))
    y = (y_diag + y_off).reshape(b, l, h, p).astype(x.dtype)
    return y, final.reshape(b, h, p, n)


def axial_rope_tables(n_tokens):
    rows = n_tokens // GRID_W
    row = jnp.repeat(jnp.arange(rows, dtype=jnp.float32), GRID_W)
    col = (jnp.arange(n_tokens) % GRID_W).astype(jnp.float32)
    inv = ROPE_THETA ** (-jnp.arange(ROPE_AXIS_FREQS, dtype=jnp.float32) / ROPE_AXIS_FREQS)
    ang = jnp.concatenate([row[:, None] * inv, col[:, None] * inv], axis=-1)
    return jnp.cos(ang), jnp.sin(ang)


def apply_rope(t, cos, sin):
    c = cos[None, :, None, :].astype(t.dtype)
    s = sin[None, :, None, :].astype(t.dtype)
    t1, t2 = t[..., :ROPE_HALF], t[..., ROPE_HALF:]
    return jnp.concatenate([t1 * c - t2 * s, t1 * s + t2 * c], axis=-1)


def attend(q, k, v):
    s = jnp.einsum('bqgrd,bkgd->bgrqk', q, k).astype(jnp.float32) * (ATT_HEAD_DIM ** -0.5)
    p = jax.nn.softmax(s, axis=-1).astype(v.dtype)
    return jnp.einsum('bgrqk,bkgd->bqgrd', p, v)


def block_attention(q, k, v):
    b, l = q.shape[:2]
    nb = l // Q_BLOCK
    qb = jnp.moveaxis(q.reshape(b, nb, Q_BLOCK, ATT_KV_HEADS, ATT_REP, ATT_HEAD_DIM), 1, 0)
    ob = lax.map(lambda qi: attend(qi, k, v), qb)
    return jnp.moveaxis(ob, 0, 1).reshape(b, l, Q_COLS)


def conv_ssd_mixer(hc, hl, w_in, conv_dw, conv_ln_g, conv_ln_b, ssm_conv_w, ssm_conv_b,
                   ssm_dt_bias, ssm_a_log, ssm_d, ssm_norm_g, w_out, ctx_out):
    bsz, lc, _ = hc.shape
    n_tok = lc + hl.shape[1]
    proj = jnp.concatenate([hc, hl], axis=1) @ w_in
    u, ga, z, xbc, dt_raw = jnp.split(proj, IN0_SPLITS, axis=-1)
    seg = lambda t: (t[:, :lc], t[:, lc:])
    flip = lambda t: jnp.flip(t, axis=1)

    glu = u * jax.nn.sigmoid(ga)
    glu_c, glu_l = seg(glu)

    def conformer(v):
        return jax.nn.silu(layer_norm(dwconv(v, conv_dw), conv_ln_g, conv_ln_b))

    xbc_c, xbc_l = seg(xbc)
    xbc = jax.nn.silu(jnp.concatenate([dwconv(xbc_c, ssm_conv_w), dwconv(xbc_l, ssm_conv_w)], axis=1) + ssm_conv_b)
    xs, bm, cm = jnp.split(xbc, [SSM_D, SSM_D + SSM_GROUPS * SSM_STATE], axis=-1)
    xs = xs.reshape(bsz, n_tok, SSM_HEADS, SSM_HEAD_DIM)
    bm = bm.reshape(bsz, n_tok, SSM_GROUPS, SSM_STATE)
    cm = cm.reshape(bsz, n_tok, SSM_GROUPS, SSM_STATE)
    dt = jax.nn.softplus(dt_raw.astype(jnp.float32).reshape(bsz, n_tok, 2, SSM_HEADS)
                         + ssm_dt_bias.astype(jnp.float32))
    a_neg = -jnp.exp(ssm_a_log.astype(jnp.float32))
    (xs_c, xs_l), (bm_c, bm_l), (cm_c, cm_l) = seg(xs), seg(bm), seg(cm)
    zero = jnp.zeros((bsz, SSM_HEADS, SSM_HEAD_DIM, SSM_STATE), jnp.float32)
    y_l = ssm_d[:, None] * xs_l
    y_c = ssm_d[:, None] * xs_c if ctx_out else None
    for d in range(2):
        order = (lambda t: t) if d == 0 else flip
        dt_c, dt_l = seg(dt[:, :, d])
        yc_d, state_c = ssd_scan(order(xs_c), order(dt_c), a_neg[d], order(bm_c), order(cm_c), zero)
        yl_d, _ = ssd_scan(order(xs_l), order(dt_l), a_neg[d], order(bm_l), order(cm_l), state_c)
        y_l = y_l + order(yl_d)
        if ctx_out:
            y_c = y_c + order(yc_d)
    z_c, z_l = seg(z)

    def merge(conf, y_ssm, zz):
        y_ssm = y_ssm.reshape(zz.shape[0], zz.shape[1], SSM_D).astype(zz.dtype)
        y_ssm = rms_norm(y_ssm * jax.nn.silu(zz), ssm_norm_g)
        return jnp.concatenate([conf, y_ssm], axis=-1) @ w_out

    out_l = merge(conformer(glu_l), y_l, z_l)
    out_c = merge(conformer(glu_c), y_c, z_c) if ctx_out else None
    return out_c, out_l


def attention_mixer(hc, hl, w_qkv, q_norm_g, k_norm_g, w_o, rope_cos, rope_sin, ctx_out):
    bsz, lc, _ = hc.shape
    heads = lambda t, nh: t.reshape(t.shape[0], t.shape[1], nh, ATT_HEAD_DIM)
    ql, kl, vl = jnp.split(hl @ w_qkv, [Q_COLS, Q_COLS + KV_COLS], axis=-1)
    ql = apply_rope(rms_norm(heads(ql, ATT_HEADS), q_norm_g), rope_cos, rope_sin)
    kl = apply_rope(rms_norm(heads(kl, ATT_KV_HEADS), k_norm_g), rope_cos, rope_sin)
    vl = heads(vl, ATT_KV_HEADS)
    if ctx_out:
        qc, kc, vc = jnp.split(hc @ w_qkv, [Q_COLS, Q_COLS + KV_COLS], axis=-1)
    else:
        kc, vc = jnp.split(hc @ w_qkv[:, Q_COLS:], [KV_COLS], axis=-1)
    kc = rms_norm(heads(kc, ATT_KV_HEADS), k_norm_g)
    vc = heads(vc, ATT_KV_HEADS)
    k_all = jnp.concatenate([kc, kl], axis=1)
    v_all = jnp.concatenate([vc, vl], axis=1)
    out_l = block_attention(ql, k_all, v_all) @ w_o
    out_c = None
    if ctx_out:
        qc = rms_norm(heads(qc, ATT_HEADS), q_norm_g)
        qc = qc.reshape(bsz, lc, ATT_KV_HEADS, ATT_REP, ATT_HEAD_DIM)
        out_c = attend(qc, kc, vc).reshape(bsz, lc, Q_COLS) @ w_o
    return out_c, out_l


def hier_moe(h, router_g, router_e, w1, w3, w2):
    t = h.shape[0]
    lg = (h @ router_g).astype(jnp.float32)
    pg = jax.nn.softmax(lg, axis=-1)
    gsel = jnp.argmax(lg, axis=-1)
    le = (h @ router_e).astype(jnp.float32).reshape(t, MOE_GROUPS, MOE_EXPERTS)
    le_sel = jnp.take_along_axis(le, gsel[:, None, None], axis=1)[:, 0]
    top_v, top_i = lax.top_k(le_sel, MOE_TOPK)
    w_k = jax.nn.softmax(top_v, axis=-1) * jnp.take_along_axis(pg, gsel[:, None], axis=1)
    expert_w = jnp.einsum('tk,tke->te', w_k, jax.nn.one_hot(top_i, MOE_EXPERTS, dtype=jnp.float32))
    combine = (jax.nn.one_hot(gsel, MOE_GROUPS, dtype=jnp.float32)[:, :, None]
               * expert_w[:, None, :]).astype(h.dtype)
    out = jnp.zeros_like(h)
    for g in range(MOE_GROUPS):
        a = jnp.einsum('td,edf->tef', h, w1[g])
        b = jnp.einsum('td,edf->tef', h, w3[g])
        hid = jax.nn.silu(a) * b * combine[:, g, :, None]
        out = out + jnp.einsum('tef,efd->td', hid, w2[g])
    return out


def setup_inputs(seed: int = 0) -> dict:
    key = jax.random.key(seed)
    ks = iter(jax.random.split(key, 40))
    f32 = jnp.float32

    def nrm(shape, scale):
        return jax.random.normal(next(ks), shape, f32) * scale

    def gain(shape):
        return 1.0 + nrm(shape, 0.02)

    d = D_MODEL
    inp = {}
    inp['x'] = nrm((BATCH, SEQ, d), 1.0)
    inp['c'] = nrm((BATCH, d), 1.0)
    inp['ctx'] = nrm((BATCH, CTX_LEN, d), 1.0)
    inp['c_ctx'] = nrm((d,), 1.0)
    inp['w_ada'] = nrm((DEPTH, d, N_MOD * d), 0.5 * d ** -0.5)
    inp['b_ada'] = nrm((DEPTH, N_MOD * d), 0.02)
    inp['norm_mix'] = gain((DEPTH, d))
    inp['norm_ffn'] = gain((DEPTH, d))
    inp['w_in0'] = nrm((N_EVEN, d, IN0_DIM), d ** -0.5)
    inp['conv_dw'] = nrm((N_EVEN, CONV_WIDTH, CONV_D), CONV_WIDTH ** -0.5)
    inp['conv_ln_g'] = gain((N_EVEN, CONV_D))
    inp['conv_ln_b'] = nrm((N_EVEN, CONV_D), 0.02)
    inp['ssm_conv_w'] = nrm((N_EVEN, SSM_CONV_WIDTH, XBC_DIM), SSM_CONV_WIDTH ** -0.5)
    inp['ssm_conv_b'] = nrm((N_EVEN, XBC_DIM), 0.02)
    dt0 = jnp.exp(jax.random.uniform(next(ks), (N_EVEN, 2, SSM_HEADS), f32, math.log(1e-3), math.log(1e-1)))
    inp['ssm_dt_bias'] = dt0 + jnp.log(-jnp.expm1(-dt0))
    inp['ssm_a_log'] = jnp.log(jax.random.uniform(next(ks), (N_EVEN, 2, SSM_HEADS), f32, 1.0, 16.0))
    inp['ssm_d'] = gain((N_EVEN, SSM_HEADS))
    inp['ssm_norm_g'] = gain((N_EVEN, SSM_D))
    inp['w_out0'] = nrm((N_EVEN, MIX0_DIM, d), MIX0_DIM ** -0.5)
    inp['w_qkv'] = nrm((N_ODD, d, QKV_DIM), d ** -0.5)
    inp['q_norm_g'] = gain((N_ODD, ATT_HEAD_DIM))
    inp['k_norm_g'] = gain((N_ODD, ATT_HEAD_DIM))
    inp['w_o'] = nrm((N_ODD, Q_COLS, d), Q_COLS ** -0.5)
    inp['moe_router_g'] = nrm((DEPTH, d, MOE_GROUPS), d ** -0.5)
    inp['moe_router_e'] = nrm((DEPTH, d, MOE_GROUPS * MOE_EXPERTS), d ** -0.5)
    inp['moe_w1'] = nrm((DEPTH, MOE_GROUPS, MOE_EXPERTS, d, MOE_HIDDEN), d ** -0.5)
    inp['moe_w3'] = nrm((DEPTH, MOE_GROUPS, MOE_EXPERTS, d, MOE_HIDDEN), d ** -0.5)
    inp['moe_w2'] = nrm((DEPTH, MOE_GROUPS, MOE_EXPERTS, MOE_HIDDEN, d), MOE_HIDDEN ** -0.5)
    inp['norm_final'] = gain((d,))
    return inp


def reference(x, c, ctx, c_ctx, w_ada, b_ada, norm_mix, norm_ffn, w_in0, conv_dw, conv_ln_g,
              conv_ln_b, ssm_conv_w, ssm_conv_b, ssm_dt_bias, ssm_a_log, ssm_d, ssm_norm_g,
              w_out0, w_qkv, q_norm_g, k_norm_g, w_o, moe_router_g, moe_router_e, moe_w1,
              moe_w3, moe_w2, norm_final):
    bsz, n_lat, d = x.shape
    lc = ctx.shape[1]
    rope_cos, rope_sin = axial_rope_tables(n_lat)
    silu_c = jax.nn.silu(c)
    silu_cc = jax.nn.silu(c_ctx)
    xl, xc = x, ctx
    for i in range(DEPTH):
        last = i == DEPTH - 1
        j = i // 2
        ml = [m[:, None, :] for m in jnp.split(silu_c @ w_ada[i] + b_ada[i], N_MOD, axis=-1)]
        n_cm = 2 if last else N_MOD
        mc = jnp.split(silu_cc @ w_ada[i][:, :n_cm * d] + b_ada[i][:n_cm * d], n_cm, axis=-1)

        hl = modulate(rms_norm(xl, norm_mix[i]), ml[0], ml[1])
        hc = modulate(rms_norm(xc, norm_mix[i]), mc[0], mc[1])
        if i % 2 == 0:
            oc, ol = conv_ssd_mixer(hc, hl, w_in0[j], conv_dw[j], conv_ln_g[j], conv_ln_b[j],
                                    ssm_conv_w[j], ssm_conv_b[j], ssm_dt_bias[j], ssm_a_log[j],
                                    ssm_d[j], ssm_norm_g[j], w_out0[j], not last)
        else:
            oc, ol = attention_mixer(hc, hl, w_qkv[j], q_norm_g[j], k_norm_g[j], w_o[j],
                                     rope_cos, rope_sin, not last)
        xl = xl + ml[2] * ol

        hl = modulate(rms_norm(xl, norm_ffn[i]), ml[3], ml[4])
        moe_p = (moe_router_g[i], moe_router_e[i], moe_w1[i], moe_w3[i], moe_w2[i])
        if last:
            yl = hier_moe(hl.reshape(-1, d), *moe_p).reshape(hl.shape)
        else:
            xc = xc + mc[2] * oc
            hc = modulate(rms_norm(xc, norm_ffn[i]), mc[3], mc[4])
            y = hier_moe(jnp.concatenate([hc, hl], axis=1).reshape(-1, d), *moe_p)
            y = y.reshape(bsz, lc + n_lat, d)
            xc = xc + mc[5] * y[:, :lc]
            yl = y[:, lc:]
        xl = xl + ml[5] * yl
    return rms_norm(xl, norm_final)
```

```python
import functools
import math

import jax
import jax.numpy as jnp
from jax import lax
from jax.experimental import pallas as pl
from jax.experimental.pallas import tpu as pltpu

F32 = jnp.float32
BF16 = jnp.bfloat16

EPS = 1e-6
N_MOD = 6
GRID_W = 64
ROPE_THETA = 10000.0
CONV_WIDTH = 31
SSM_CONV_WIDTH = 5
SSM_HEADS = 32
SSM_HEAD_DIM = 64
SSM_GROUPS = 8
SSM_STATE = 128
SSM_CHUNK = 128
ATT_HEADS = 32
ATT_KV_HEADS = 8
ATT_HEAD_DIM = 128
MOE_GROUPS = 4
MOE_EXPERTS = 8
N_EXPERTS = MOE_GROUPS * MOE_EXPERTS

LANES = 128
ROW_TILE = 256
HALO = 16
VMEM_LIMIT = 56 * 1024 * 1024


def _cparams(sem):
    return pltpu.CompilerParams(dimension_semantics=sem, vmem_limit_bytes=VMEM_LIMIT)


def _silu(v):
    return v * (1.0 / (1.0 + jnp.exp(-v)))


def _sigmoid(v):
    return 1.0 / (1.0 + jnp.exp(-v))


def _ada_kernel(c_ref, w_ref, b_ref, o_ref):
    s = _silu(c_ref[...]).astype(BF16)
    w = w_ref[...].astype(BF16)
    o_ref[...] = jnp.dot(s, w, preferred_element_type=F32) + b_ref[...]


def ada_mods(cvec, w_ada, b_ada, tn=512):
    depth, d, n = w_ada.shape
    rows = cvec.shape[0]
    return pl.pallas_call(
        _ada_kernel,
        grid=(depth, n // tn),
        in_specs=[
            pl.BlockSpec((rows, d), lambda l, j: (0, 0)),
            pl.BlockSpec((None, d, tn), lambda l, j: (l, 0, j)),
            pl.BlockSpec((None, 1, tn), lambda l, j: (l, 0, j)),
        ],
        out_specs=pl.BlockSpec((None, rows, tn), lambda l, j: (l, 0, j)),
        out_shape=jax.ShapeDtypeStruct((depth, rows, n), F32),
        compiler_params=_cparams(("arbitrary", "arbitrary")),
        name="ada_mods",
    )(cvec, w_ada, b_ada.reshape(depth, 1, n))


class Geom:
    def __init__(self, bsz, lc, n_lat):
        self.bsz, self.lc, self.n_lat = bsz, lc, n_lat
        self.per_b = (lc + n_lat) // ROW_TILE
        self.ctx_t = lc // ROW_TILE
        self.lat_t = n_lat // ROW_TILE
        self.n_all = bsz * self.per_b
        self.n_lat_tiles = bsz * self.lat_t

    def all_block(self, i):
        return i

    def lat_block(self, i):
        return (i // self.lat_t) * self.per_b + self.ctx_t + i % self.lat_t

    def all_mod_row(self, i):
        return jnp.where(i % self.per_b < self.ctx_t, self.bsz, i // self.per_b)

    def lat_mod_row(self, i):
        return i // self.lat_t


def _mod_spec(layer, m, row_fn, n_rows, grid_rank, tile_axis, width=None, col_axis=None):
    def imap(*g):
        r = row_fn(g[tile_axis])
        c = 0 if col_axis is None else g[col_axis]
        return ((layer * n_rows + r) * N_MOD + m, 0, c)
    return imap


def _norm_mod_kernel(x_ref, g_ref, sh_ref, sc_ref, h_ref):
    x = x_ref[...]
    ms = jnp.mean(x * x, axis=-1, keepdims=True)
    xn = x * lax.rsqrt(ms + EPS) * g_ref[...]
    h_ref[...] = (xn * (1.0 + sc_ref[...]) + sh_ref[...]).astype(h_ref.dtype)


def _route(logits):
    lane = lax.broadcasted_iota(jnp.int32, logits.shape, 1)
    neg = jnp.float32(-jnp.inf)
    big = jnp.int32(LANES)
    lg = jnp.where(lane < MOE_GROUPS, logits, neg)
    mg = jnp.max(lg, axis=-1, keepdims=True)
    gsel = jnp.min(jnp.where(lg == mg, lane, big), axis=-1, keepdims=True)
    pg = 1.0 / jnp.sum(jnp.exp(lg - mg), axis=-1, keepdims=True)
    lo = MOE_GROUPS + gsel * MOE_EXPERTS
    le = jnp.where((lane >= lo) & (lane < lo + MOE_EXPERTS), logits, neg)
    v1 = jnp.max(le, axis=-1, keepdims=True)
    i1 = jnp.min(jnp.where(le == v1, lane, big), axis=-1, keepdims=True)
    le2 = jnp.where(lane == i1, neg, le)
    v2 = jnp.max(le2, axis=-1, keepdims=True)
    i2 = jnp.min(jnp.where(le2 == v2, lane, big), axis=-1, keepdims=True)
    e2 = jnp.exp(v2 - v1)
    w1 = pg / (1.0 + e2)
    w2 = pg * e2 / (1.0 + e2)
    id1 = (i1 - MOE_GROUPS).astype(F32)
    id2 = (i2 - MOE_GROUPS).astype(F32)
    return jnp.where(lane == 0, id1, jnp.where(lane == 1, id2,
                     jnp.where(lane == 2, w1, jnp.where(lane == 3, w2, 0.0))))


def _norm_mod_route_kernel(x_ref, g_ref, sh_ref, sc_ref, wr_ref, h_ref, r_ref):
    x = x_ref[...]
    ms = jnp.mean(x * x, axis=-1, keepdims=True)
    xn = x * lax.rsqrt(ms + EPS) * g_ref[...]
    h = xn * (1.0 + sc_ref[...]) + sh_ref[...]
    h_ref[...] = h
    logits = jnp.dot(h, wr_ref[...], preferred_element_type=F32,
                     precision=lax.Precision.HIGHEST)
    r_ref[...] = _route(logits)


def norm_mod(x, gain, mods, layer, m_shift, n_rows, n_tiles, block_fn, row_fn,
             router_w=None, out_dtype=BF16):
    d = x.shape[1]
    x_spec = pl.BlockSpec((ROW_TILE, d), lambda i: (block_fn(i), 0))
    g_spec = pl.BlockSpec((1, d), lambda i: (0, 0))
    sh_spec = pl.BlockSpec((None, 1, d), _mod_spec(layer, m_shift, row_fn, n_rows, 1, 0))
    sc_spec = pl.BlockSpec((None, 1, d), _mod_spec(layer, m_shift + 1, row_fn, n_rows, 1, 0))
    o_spec = pl.BlockSpec((ROW_TILE, d), lambda i: (i, 0))
    rows = n_tiles * ROW_TILE
    if router_w is None:
        return pl.pallas_call(
            _norm_mod_kernel,
            grid=(n_tiles,),
            in_specs=[x_spec, g_spec, sh_spec, sc_spec],
            out_specs=o_spec,
            out_shape=jax.ShapeDtypeStruct((rows, d), out_dtype),
            compiler_params=_cparams(("parallel",)),
            name="norm_mod",
        )(x, gain.reshape(1, d), mods, mods)
    return pl.pallas_call(
        _norm_mod_route_kernel,
        grid=(n_tiles,),
        in_specs=[x_spec, g_spec, sh_spec, sc_spec,
                  pl.BlockSpec((d, LANES), lambda i: (0, 0))],
        out_specs=[o_spec, pl.BlockSpec((ROW_TILE, LANES), lambda i: (i, 0))],
        out_shape=[jax.ShapeDtypeStruct((rows, d), F32),
                   jax.ShapeDtypeStruct((rows, LANES), F32)],
        compiler_params=_cparams(("parallel",)),
        name="norm_mod_route",
    )(x, gain.reshape(1, d), mods, mods, router_w)


def _mm_kernel(x_ref, w_ref, o_ref):
    o_ref[...] = jnp.dot(x_ref[...], w_ref[...], preferred_element_type=F32).astype(o_ref.dtype)


def matmul(x, w, out_dtype, tm=512, tn=1024):
    m, k = x.shape
    n = w.shape[1]
    tn = min(tn, n)
    return pl.pallas_call(
        _mm_kernel,
        grid=(n // tn, m // tm),
        in_specs=[pl.BlockSpec((tm, k), lambda j, i: (i, 0)),
                  pl.BlockSpec((k, tn), lambda j, i: (0, j))],
        out_specs=pl.BlockSpec((tm, tn), lambda j, i: (i, j)),
        out_shape=jax.ShapeDtypeStruct((m, n), out_dtype),
        compiler_params=_cparams(("parallel", "parallel")),
        name="matmul",
    )(x, w)


def _mm_glu_kernel(x_ref, wa_ref, wb_ref, o_ref):
    x = x_ref[...]
    a = jnp.dot(x, wa_ref[...], preferred_element_type=F32)
    b = jnp.dot(x, wb_ref[...], preferred_element_type=F32)
    o_ref[...] = (a * _sigmoid(b)).astype(o_ref.dtype)


def matmul_glu(x, wa, wb, out_dtype, tm=512, tn=512):
    m, k = x.shape
    n = wa.shape[1]
    return pl.pallas_call(
        _mm_glu_kernel,
        grid=(n // tn, m // tm),
        in_specs=[pl.BlockSpec((tm, k), lambda j, i: (i, 0)),
                  pl.BlockSpec((k, tn), lambda j, i: (0, j)),
                  pl.BlockSpec((k, tn), lambda j, i: (0, j))],
        out_specs=pl.BlockSpec((tm, tn), lambda j, i: (i, j)),
        out_shape=jax.ShapeDtypeStruct((m, n), out_dtype),
        compiler_params=_cparams(("parallel", "parallel")),
        name="matmul_glu",
    )(x, wa, wb)


def _mm_res_kernel(n_lhs, *refs):
    xs = refs[:n_lhs]
    ws = refs[n_lhs:2 * n_lhs]
    res_ref, gate_ref, o_ref = refs[2 * n_lhs:]
    acc = jnp.dot(xs[0][...], ws[0][...], preferred_element_type=F32)
    for x_ref, w_ref in zip(xs[1:], ws[1:]):
        acc = acc + jnp.dot(x_ref[...], w_ref[...], preferred_element_type=F32)
    o_ref[...] = res_ref[...] + gate_ref[...] * acc


def matmul_residual(lhs, ws, res, mods, layer, m_gate, n_rows, n_tiles, res_block_fn, row_fn,
                    tn=1024):
    n_lhs = len(lhs)
    n = ws[0].shape[1]
    in_specs = [pl.BlockSpec((ROW_TILE, a.shape[1]), lambda j, i: (i, 0)) for a in lhs]
    in_specs += [pl.BlockSpec((w.shape[0], tn), lambda j, i: (0, j)) for w in ws]
    in_specs += [
        pl.BlockSpec((ROW_TILE, tn), lambda j, i: (res_block_fn(i), j)),
        pl.BlockSpec((None, 1, tn), _mod_spec(layer, m_gate, row_fn, n_rows, 2, 1, col_axis=0)),
    ]
    return pl.pallas_call(
        functools.partial(_mm_res_kernel, n_lhs),
        grid=(n // tn, n_tiles),
        in_specs=in_specs,
        out_specs=pl.BlockSpec((ROW_TILE, tn), lambda j, i: (i, j)),
        out_shape=jax.ShapeDtypeStruct((n_tiles * ROW_TILE, n), F32),
        compiler_params=_cparams(("parallel", "parallel")),
        name="matmul_residual",
    )(*lhs, *ws, res, mods)


def _conv_kernel(width, mode, per_b, ctx_t, x_ref, prev_ref, next_ref, w_ref, p0_ref, p1_ref,
                 o_ref, xp_ref, acc_ref):
    t = pl.program_id(0) % per_b
    first = (t == 0) | (t == ctx_t)
    last = (t == ctx_t - 1) | (t == per_b - 1)
    c = x_ref.shape[1]
    xp_ref[0:HALO, :] = jnp.where(first, 0.0, prev_ref[...])
    xp_ref[HALO:HALO + ROW_TILE, :] = x_ref[...]
    xp_ref[HALO + ROW_TILE:, :] = jnp.where(last, 0.0, next_ref[...])
    off = HALO - width // 2

    def chunk(j, carry):
        cs = pl.ds(pl.multiple_of(j * LANES, LANES), LANES)
        acc = jnp.zeros((ROW_TILE, LANES), F32)
        for k in range(width):
            acc = acc + xp_ref[off + k:off + k + ROW_TILE, cs] * w_ref[k:k + 1, cs]
        acc_ref[:, cs] = acc
        return carry

    lax.fori_loop(0, c // LANES, chunk, 0)
    y = acc_ref[...]
    if mode == "layernorm_silu":
        mu = jnp.mean(y, axis=-1, keepdims=True)
        yc = y - mu
        var = jnp.mean(yc * yc, axis=-1, keepdims=True)
        y = yc * lax.rsqrt(var + EPS) * p0_ref[...] + p1_ref[...]
    else:
        y = y + p0_ref[...]
    o_ref[...] = _silu(y).astype(o_ref.dtype)


def seq_conv(x, w, p0, p1, mode, geom, out_dtype=BF16):
    rows, c = x.shape
    width = w.shape[0]
    hb = ROW_TILE // HALO
    n_halo = rows // HALO
    kern = functools.partial(_conv_kernel, width, mode, geom.per_b, geom.ctx_t)
    return pl.pallas_call(
        kern,
        grid=(geom.n_all,),
        in_specs=[
            pl.BlockSpec((ROW_TILE, c), lambda i: (i, 0)),
            pl.BlockSpec((HALO, c), lambda i: (jnp.maximum(i * hb - 1, 0), 0)),
            pl.BlockSpec((HALO, c), lambda i: (jnp.minimum((i + 1) * hb, n_halo - 1), 0)),
            pl.BlockSpec((width, c), lambda i: (0, 0)),
            pl.BlockSpec((1, c), lambda i: (0, 0)),
            pl.BlockSpec((1, c), lambda i: (0, 0)),
        ],
        out_specs=pl.BlockSpec((ROW_TILE, c), lambda i: (i, 0)),
        out_shape=jax.ShapeDtypeStruct((rows, c), out_dtype),
        scratch_shapes=[pltpu.VMEM((ROW_TILE + 2 * HALO, c), F32),
                        pltpu.VMEM((ROW_TILE, c), F32)],
        compiler_params=_cparams(("parallel",)),
        name="seq_conv_" + mode,
    )(x, x, x, w, p0.reshape(1, c), p1.reshape(1, c))


def _ssd_kernel(n_chunks, ctx_chunks, xs_ref, b_ref, c_ref, dtraw_ref, bias_ref, aneg_ref,
                y_ref, state_ref, tr_ref):
    d = pl.program_id(1)
    T = SSM_CHUNK
    rows = lax.broadcasted_iota(jnp.int32, (T, T), 0)
    cols = lax.broadcasted_iota(jnp.int32, (T, T), 1)
    fwd = d == 0
    diff = (rows - cols) * jnp.where(fwd, 1, -1)
    tri = jnp.where(diff <= 0, 1.0, 0.0).astype(F32)
    keep = diff >= 0
    lane = lax.broadcasted_iota(jnp.int32, (T, T), 1)
    left = lane < SSM_HEAD_DIM
    state_ref[...] = jnp.zeros_like(state_ref)
    tr_ref[...] = jnp.zeros_like(tr_ref)
    bias = bias_ref[...]
    aneg = aneg_ref[...]
    n_heads = 4

    def step(s, carry):
        bwd_c = jnp.where(s < ctx_chunks, ctx_chunks - 1 - s, n_chunks - 1 + ctx_chunks - s)
        c = jnp.where(fwd, s, bwd_c)
        r0 = pl.multiple_of(c * T, T)
        raw = dtraw_ref[:, pl.ds(r0, T)]
        z = raw + bias
        dt_row = jnp.maximum(z, 0.0) + jnp.log1p(jnp.exp(-jnp.abs(z)))
        da_row = dt_row * aneg
        cum_row = jnp.dot(da_row, tri, preferred_element_type=F32,
                          precision=lax.Precision.HIGHEST)
        tot_row = jnp.sum(da_row, axis=-1, keepdims=True)
        tr_ref[0:8, :] = dt_row
        tr_ref[8:16, :] = cum_row
        colform = tr_ref[...].T
        bmat = b_ref[pl.ds(r0, T), :]
        cmat = c_ref[pl.ds(r0, T), :]
        cb = lax.dot_general(cmat, bmat, (((1,), (1,)), ((), ())),
                             preferred_element_type=F32)
        bt = bmat.astype(F32).T
        for pair in range(n_heads // 2):
            xs_pair = xs_ref[pl.ds(r0, T), pair * LANES:(pair + 1) * LANES].astype(F32)
            ms, bds, ecs, ets = [], [], [], []
            dts = []
            for q in range(2):
                r = pair * 2 + q
                cum_c = colform[:, 8 + r:9 + r]
                cum_r = cum_row[r:r + 1, :]
                seg = jnp.where(keep, cum_c - cum_r, -jnp.inf)
                ms.append((cb * jnp.exp(seg)).astype(BF16))
                dec_r = jnp.exp(tot_row[r:r + 1, :] - cum_r)
                bds.append((bt * dec_r).astype(BF16))
                ecs.append(jnp.exp(cum_c))
                ets.append(jnp.exp(tot_row[r:r + 1, :]))
                dts.append(colform[:, r:r + 1])
            xd = xs_pair * jnp.where(left, dts[0], dts[1])
            xd_l = jnp.where(left, xd, 0.0).astype(BF16)
            xd_r = jnp.where(left, 0.0, xd).astype(BF16)
            xd2 = jnp.concatenate([xd_l, xd_r], axis=0)
            m2 = jnp.concatenate(ms, axis=1)
            y_diag = jnp.dot(m2, xd2, preferred_element_type=F32)
            st = state_ref[pair]
            y_off = jnp.dot(cmat, st.astype(BF16), preferred_element_type=F32)
            y_off = y_off * jnp.where(left, ecs[0], ecs[1])
            y_ref[pl.ds(r0, T), pair * LANES:(pair + 1) * LANES] = (y_diag + y_off).astype(y_ref.dtype)
            bd2 = jnp.concatenate(bds, axis=1)
            upd = jnp.dot(bd2, xd2, preferred_element_type=F32)
            state_ref[pair] = st * jnp.where(left[0:1, :], ets[0], ets[1]) + upd
        return carry

    lax.fori_loop(0, n_chunks, step, 0)


def ssd_scan(xbc, dtraw_rows, bias_rows, aneg_rows, geom):
    rows = xbc.shape[0]
    per = geom.lc + geom.n_lat
    n_chunks = per // SSM_CHUNK
    ctx_chunks = geom.lc // SSM_CHUNK
    ssm_d = SSM_HEADS * SSM_HEAD_DIM
    gw = 4 * SSM_HEAD_DIM
    xs_blocks = ssm_d // gw
    kern = functools.partial(_ssd_kernel, n_chunks, ctx_chunks)
    return pl.pallas_call(
        kern,
        grid=(geom.bsz, 2, SSM_GROUPS),
        in_specs=[
            pl.BlockSpec((per, gw), lambda b, d, g: (b, g)),
            pl.BlockSpec((per, SSM_STATE), lambda b, d, g: (b, 2 * xs_blocks + g)),
            pl.BlockSpec((per, SSM_STATE), lambda b, d, g: (b, 2 * xs_blocks + SSM_GROUPS + g)),
            pl.BlockSpec((None, None, None, 8, per), lambda b, d, g: (b, d, g, 0, 0)),
            pl.BlockSpec((None, None, 8, LANES), lambda b, d, g: (d, g, 0, 0)),
            pl.BlockSpec((None, None, 8, LANES), lambda b, d, g: (d, g, 0, 0)),
        ],
        out_specs=pl.BlockSpec((None, per, gw), lambda b, d, g: (d, b, g)),
        out_shape=jax.ShapeDtypeStruct((2, rows, ssm_d), BF16),
        scratch_shapes=[pltpu.VMEM((2, SSM_STATE, LANES), F32),
                        pltpu.VMEM((LANES, SSM_CHUNK), F32)],
        compiler_params=_cparams(("parallel", "parallel", "parallel")),
        name="ssd_scan",
    )(xbc, xbc, xbc, dtraw_rows, bias_rows, aneg_rows)


def _ssd_merge_kernel(yf_ref, yb_ref, xs_ref, z_ref, d_ref, g_ref, o_ref):
    y = d_ref[...] * xs_ref[...].astype(F32) + yf_ref[...].astype(F32) + yb_ref[...].astype(F32)
    gated = y * _silu(z_ref[...].astype(F32))
    ms = jnp.mean(gated * gated, axis=-1, keepdims=True)
    o_ref[...] = (gated * lax.rsqrt(ms + EPS) * g_ref[...]).astype(o_ref.dtype)


def ssd_merge(y2, xbc, z, d_full, g, geom):
    rows, ssm_d = z.shape
    spec = pl.BlockSpec((ROW_TILE, ssm_d), lambda i: (i, 0))
    vec = pl.BlockSpec((1, ssm_d), lambda i: (0, 0))
    return pl.pallas_call(
        _ssd_merge_kernel,
        grid=(geom.n_all,),
        in_specs=[pl.BlockSpec((None, ROW_TILE, ssm_d), lambda i: (0, i, 0)),
                  pl.BlockSpec((None, ROW_TILE, ssm_d), lambda i: (1, i, 0)),
                  spec, spec, vec, vec],
        out_specs=spec,
        out_shape=jax.ShapeDtypeStruct((rows, ssm_d), BF16),
        compiler_params=_cparams(("parallel",)),
        name="ssd_merge",
    )(y2, y2, xbc, z, d_full.reshape(1, ssm_d), g.reshape(1, ssm_d))


def _qk_prep_kernel(n_q, n_k, scale, qkv_ref, cos_ref, sin_ref, gq_ref, gk_ref, q_ref, k_ref):
    cosf = cos_ref[...]
    sinf = sin_ref[...]
    hd = ATT_HEAD_DIM
    for h in range(n_q + n_k):
        x = qkv_ref[:, h * hd:(h + 1) * hd].astype(F32)
        g = gq_ref[...] if h < n_q else gk_ref[...]
        ms = jnp.mean(x * x, axis=-1, keepdims=True)
        xn = x * lax.rsqrt(ms + EPS) * g
        rot = pltpu.roll(xn, hd // 2, axis=1)
        o = xn * cosf + rot * sinf
        if h < n_q:
            q_ref[:, h * hd:(h + 1) * hd] = (o * scale).astype(q_ref.dtype)
        else:
            k_ref[:, (h - n_q) * hd:(h - n_q + 1) * hd] = o.astype(k_ref.dtype)


def qk_prep(qkv, cosf, sinf, gq, gk, geom):
    rows = qkv.shape[0]
    hd = ATT_HEAD_DIM
    qc, kc = ATT_HEADS * hd, ATT_KV_HEADS * hd
    kern = functools.partial(_qk_prep_kernel, ATT_HEADS, ATT_KV_HEADS, hd ** -0.5)
    per_b = geom.per_b
    return pl.pallas_call(
        kern,
        grid=(geom.n_all,),
        in_specs=[pl.BlockSpec((ROW_TILE, qc + kc), lambda i: (i, 0)),
                  pl.BlockSpec((ROW_TILE, hd), lambda i: (i % per_b, 0)),
                  pl.BlockSpec((ROW_TILE, hd), lambda i: (i % per_b, 0)),
                  pl.BlockSpec((1, hd), lambda i: (0, 0)),
                  pl.BlockSpec((1, hd), lambda i: (0, 0))],
        out_specs=[pl.BlockSpec((ROW_TILE, qc), lambda i: (i, 0)),
                   pl.BlockSpec((ROW_TILE, kc), lambda i: (i, 0))],
        out_shape=[jax.ShapeDtypeStruct((rows, qc), BF16),
                   jax.ShapeDtypeStruct((rows, kc), BF16)],
        compiler_params=_cparams(("parallel",)),
        name="qk_prep",
    )(qkv, cosf, sinf, gq.reshape(1, hd), gk.reshape(1, hd))


def _attn_kernel(rep, q_ref, k_ref, v_ref, o_ref):
    hd = ATT_HEAD_DIM
    k = k_ref[...]
    v = v_ref[...]
    for r in range(rep):
        q = q_ref[:, r * hd:(r + 1) * hd]
        s = lax.dot_general(q, k, (((1,), (1,)), ((), ())), preferred_element_type=F32)
        m = jnp.max(s, axis=-1, keepdims=True)
        p = jnp.exp(s - m)
        l = jnp.sum(p, axis=-1, keepdims=True)
        o = jnp.dot(p.astype(BF16), v, preferred_element_type=F32)
        o_ref[:, r * hd:(r + 1) * hd] = (o / l).astype(o_ref.dtype)


def attention(q, k, qkv, geom):
    hd = ATT_HEAD_DIM
    rep = ATT_HEADS // ATT_KV_HEADS
    per = geom.lc + geom.n_lat
    v_col0 = (ATT_HEADS + ATT_KV_HEADS)
    lat_t, per_b, ctx_t = geom.lat_t, geom.per_b, geom.ctx_t
    return pl.pallas_call(
        functools.partial(_attn_kernel, rep),
        grid=(geom.bsz, ATT_KV_HEADS, lat_t),
        in_specs=[pl.BlockSpec((ROW_TILE, rep * hd), lambda b, g, t: (b * per_b + ctx_t + t, g)),
                  pl.BlockSpec((per, hd), lambda b, g, t: (b, g)),
                  pl.BlockSpec((per, hd), lambda b, g, t: (b, v_col0 + g))],
        out_specs=pl.BlockSpec((ROW_TILE, rep * hd), lambda b, g, t: (b * lat_t + t, g)),
        out_shape=jax.ShapeDtypeStruct((geom.bsz * geom.n_lat, ATT_HEADS * hd), BF16),
        compiler_params=_cparams(("parallel", "parallel", "parallel")),
        name="attention",
    )(q, k, qkv)


def _gather_rows_kernel(n_src, n_out, ids_ref, *refs):
    srcs = refs[:n_src]
    outs = refs[n_src:2 * n_src]
    sem = refs[2 * n_src]
    base = pl.program_id(0) * ROW_TILE

    def issue(r, carry):
        for q in range(n_src):
            row = ids_ref[q * n_out + base + r]
            pltpu.make_async_copy(srcs[q].at[pl.ds(row, 1), :], outs[q].at[pl.ds(r, 1), :],
                                  sem.at[q]).start()
        return carry

    lax.fori_loop(0, ROW_TILE, issue, 0)
    for q in range(n_src):
        pltpu.make_async_copy(srcs[q].at[pl.ds(0, ROW_TILE), :], outs[q], sem.at[q]).wait()


def gather_rows(src, ids):
    n_src, n_out = ids.shape
    d = src.shape[1]
    outs = pl.pallas_call(
        functools.partial(_gather_rows_kernel, n_src, n_out),
        grid_spec=pltpu.PrefetchScalarGridSpec(
            num_scalar_prefetch=1,
            grid=(n_out // ROW_TILE,),
            in_specs=[pl.BlockSpec(memory_space=pl.ANY)] * n_src,
            out_specs=[pl.BlockSpec((ROW_TILE, d), lambda i, ids: (i, 0))] * n_src,
            scratch_shapes=[pltpu.SemaphoreType.DMA((n_src,))],
        ),
        out_shape=[jax.ShapeDtypeStruct((n_out, d), src.dtype)] * n_src,
        compiler_params=_cparams(("arbitrary",)),
        name="gather_rows",
    )(ids.reshape(-1), *([src] * n_src))
    return outs


def _expert_mlp_kernel(te_ref, nt_ref, x_ref, cw_ref, w1_ref, w3_ref, w2_ref, o_ref):
    t = pl.program_id(0)

    @pl.when(t < nt_ref[0])
    def _():
        x = x_ref[...].astype(BF16)
        a = jnp.dot(x, w1_ref[...], preferred_element_type=F32)
        b = jnp.dot(x, w3_ref[...], preferred_element_type=F32)
        hid = _silu(a) * b * cw_ref[...]
        o_ref[...] = jnp.dot(hid.astype(BF16), w2_ref[...], preferred_element_type=F32)

    @pl.when(t >= nt_ref[0])
    def _():
        o_ref[...] = jnp.zeros_like(o_ref)


def expert_mlp(xs, cw, w1, w3, w2, tile_expert, n_tiles_used):
    p, d = xs.shape
    hdim = w1.shape[2]
    n_tiles = p // ROW_TILE

    def wmap(t, te, nt):
        return (te[t], 0, 0)

    return pl.pallas_call(
        _expert_mlp_kernel,
        grid_spec=pltpu.PrefetchScalarGridSpec(
            num_scalar_prefetch=2,
            grid=(n_tiles,),
            in_specs=[pl.BlockSpec((ROW_TILE, d), lambda t, te, nt: (t, 0)),
                      pl.BlockSpec((ROW_TILE, 1), lambda t, te, nt: (t, 0)),
                      pl.BlockSpec((None, d, hdim), wmap),
                      pl.BlockSpec((None, d, hdim), wmap),
                      pl.BlockSpec((None, hdim, d), wmap)],
            out_specs=pl.BlockSpec((ROW_TILE, d), lambda t, te, nt: (t, 0)),
        ),
        out_shape=jax.ShapeDtypeStruct((p, d), F32),
        compiler_params=_cparams(("arbitrary",)),
        name="expert_mlp",
    )(tile_expert, n_tiles_used, xs, cw, w1, w3, w2)


def _combine_kernel(y0_ref, y1_ref, res_ref, gate_ref, o_ref):
    o_ref[...] = res_ref[...] + gate_ref[...] * (y0_ref[...] + y1_ref[...])


def _combine_norm_kernel(y0_ref, y1_ref, res_ref, gate_ref, g_ref, o_ref):
    x = res_ref[...] + gate_ref[...] * (y0_ref[...] + y1_ref[...])
    ms = jnp.mean(x * x, axis=-1, keepdims=True)
    o_ref[...] = x * lax.rsqrt(ms + EPS) * g_ref[...]


def moe_combine(y0, y1, res, mods, layer, n_rows, n_tiles, res_block_fn, row_fn, final_gain=None):
    d = res.shape[1]
    tile = pl.BlockSpec((ROW_TILE, d), lambda i: (i, 0))
    in_specs = [tile, tile,
                pl.BlockSpec((ROW_TILE, d), lambda i: (res_block_fn(i), 0)),
                pl.BlockSpec((None, 1, d), _mod_spec(layer, 5, row_fn, n_rows, 1, 0))]
    args = [y0, y1, res, mods]
    kern = _combine_kernel
    if final_gain is not None:
        in_specs.append(pl.BlockSpec((1, d), lambda i: (0, 0)))
        args.append(final_gain.reshape(1, d))
        kern = _combine_norm_kernel
    return pl.pallas_call(
        kern,
        grid=(n_tiles,),
        in_specs=in_specs,
        out_specs=tile,
        out_shape=jax.ShapeDtypeStruct((n_tiles * ROW_TILE, d), F32),
        compiler_params=_cparams(("parallel",)),
        name="moe_combine",
    )(*args)


def _dispatch_plan(route):
    t = route.shape[0]
    ids = route[:, 0:2].astype(jnp.int32)
    wts = route[:, 2:4]
    flat = ids.reshape(-1)
    n_assign = flat.shape[0]
    p_max = n_assign + N_EXPERTS * ROW_TILE
    p_max = (p_max // ROW_TILE) * ROW_TILE
    onehot = (flat[:, None] == jnp.arange(N_EXPERTS, dtype=jnp.int32)[None, :]).astype(jnp.int32)
    counts = jnp.sum(onehot, axis=0)
    padded = ((counts + ROW_TILE - 1) // ROW_TILE) * ROW_TILE
    ends = jnp.cumsum(padded)
    offs = ends - padded
    starts = jnp.cumsum(counts) - counts
    order = jnp.argsort(flat, stable=True).astype(jnp.int32)
    slot = jnp.arange(p_max, dtype=jnp.int32)
    slot_e = jnp.minimum(jnp.searchsorted(ends, slot, side="right"), N_EXPERTS - 1).astype(jnp.int32)
    rank = slot - offs[slot_e]
    valid = (rank < counts[slot_e]) & (slot < ends[-1])
    src_sorted = jnp.clip(starts[slot_e] + rank, 0, n_assign - 1)
    assign = order[src_sorted]
    row_ids = jnp.where(valid, assign // 2, 0).astype(jnp.int32)
    cw = jnp.where(valid, wts.reshape(-1)[assign], 0.0).astype(F32)
    rank_a = jnp.take_along_axis(jnp.cumsum(onehot, axis=0), flat[:, None], axis=1)[:, 0] - 1
    pos = (offs[flat] + rank_a).astype(jnp.int32).reshape(t, 2)
    tile_expert = slot_e[::ROW_TILE]
    n_tiles_used = (ends[-1] // ROW_TILE).astype(jnp.int32).reshape(1)
    last_e = tile_expert[jnp.maximum(n_tiles_used[0] - 1, 0)]
    tile_idx = jnp.arange(p_max // ROW_TILE, dtype=jnp.int32)
    tile_expert = jnp.where(tile_idx < n_tiles_used[0], tile_expert, last_e).astype(jnp.int32)
    return row_ids, cw, pos, tile_expert, n_tiles_used


def hier_moe_block(h, route, w1, w3, w2):
    row_ids, cw, pos, tile_expert, n_tiles_used = _dispatch_plan(route)
    (xs,) = gather_rows(h, row_ids[None, :])
    ys = expert_mlp(xs, cw[:, None], w1, w3, w2, tile_expert, n_tiles_used)
    y0, y1 = gather_rows(ys, pos.T)
    return y0, y1


def _rope_tables(geom):
    n_lat = geom.n_lat
    half = ATT_HEAD_DIM // 2
    nfreq = half // 2
    row = jnp.repeat(jnp.arange(n_lat // GRID_W, dtype=F32), GRID_W)
    col = (jnp.arange(n_lat) % GRID_W).astype(F32)
    inv = ROPE_THETA ** (-jnp.arange(nfreq, dtype=F32) / nfreq)
    ang = jnp.concatenate([row[:, None] * inv, col[:, None] * inv], axis=-1)
    cos, sin = jnp.cos(ang), jnp.sin(ang)
    cosf = jnp.concatenate([cos, cos], axis=-1)
    sinf = jnp.concatenate([-sin, sin], axis=-1)
    cosf = jnp.concatenate([jnp.ones((geom.lc, ATT_HEAD_DIM), F32), cosf], axis=0)
    sinf = jnp.concatenate([jnp.zeros((geom.lc, ATT_HEAD_DIM), F32), sinf], axis=0)
    return cosf, sinf


def _router_weights(router_g, router_e):
    d = router_g.shape[0]
    pad = jnp.zeros((d, LANES - MOE_GROUPS - N_EXPERTS), F32)
    return jnp.concatenate([router_g, router_e, pad], axis=1)


def mixer_even(xs, mods, layer, j, n_rows, geom, norm_g, w_in, conv_dw, conv_ln_g, conv_ln_b,
               ssm_conv_w, ssm_conv_b, ssm_dt_bias, ssm_a_log, ssm_d, ssm_norm_g, w_out,
               debug=False):
    bsz, per = geom.bsz, geom.lc + geom.n_lat
    conv_d = conv_dw.shape[1]
    ssm_dd = SSM_HEADS * SSM_HEAD_DIM
    xbc_dim = ssm_conv_w.shape[1]
    h = norm_mod(xs, norm_g, mods, layer, 0, n_rows, geom.n_all, geom.all_block, geom.all_mod_row)
    w_in = w_in.astype(BF16)
    c0, c1, c2, c3 = conv_d, 2 * conv_d, 2 * conv_d + ssm_dd, 2 * conv_d + ssm_dd + xbc_dim
    glu = matmul_glu(h, w_in[:, :c0], w_in[:, c0:c1], F32)
    z = matmul(h, w_in[:, c1:c2], BF16)
    xbc_raw = matmul(h, w_in[:, c2:c3], F32)
    w_dt = jnp.pad(w_in[:, c3:], ((0, 0), (0, LANES - 2 * SSM_HEADS)))
    dt_raw = matmul(h, w_dt, F32)[:, :2 * SSM_HEADS]
    conf = seq_conv(glu, conv_dw, conv_ln_g, conv_ln_b, "layernorm_silu", geom)
    xbc = seq_conv(xbc_raw, ssm_conv_w, ssm_conv_b, ssm_conv_b, "bias_silu", geom)
    dtr = dt_raw.reshape(bsz, per, 2, SSM_GROUPS, 4).transpose(0, 2, 3, 4, 1)
    dtr = jnp.pad(dtr, ((0, 0), (0, 0), (0, 0), (0, 4), (0, 0)))

    def head_rows(v):
        v = v.astype(F32).reshape(2, SSM_GROUPS, 4)
        v = jnp.pad(v, ((0, 0), (0, 0), (0, 4)))
        return jnp.broadcast_to(v[..., None], (2, SSM_GROUPS, 8, LANES))

    y2 = ssd_scan(xbc, dtr, head_rows(ssm_dt_bias), head_rows(-jnp.exp(ssm_a_log.astype(F32))), geom)
    yn = ssd_merge(y2, xbc, z, jnp.repeat(ssm_d, SSM_HEAD_DIM), ssm_norm_g, geom)
    w_out = w_out.astype(BF16)
    out = matmul_residual([conf, yn], [w_out[:conv_d], w_out[conv_d:]], xs, mods, layer, 2,
                          n_rows, geom.n_all, geom.all_block, geom.all_mod_row)
    if debug:
        return out, dict(h=h, glu=glu, z=z, xbc_raw=xbc_raw, dt_raw=dt_raw, conf=conf, xbc=xbc,
                         y2=y2, yn=yn)
    return out


def mixer_odd(xs, mods, layer, n_rows, geom, norm_g, w_qkv, q_norm_g, k_norm_g, w_o):
    h = norm_mod(xs, norm_g, mods, layer, 0, n_rows, geom.n_all, geom.all_block, geom.all_mod_row)
    qkv = matmul(h, w_qkv.astype(BF16), BF16)
    cosf, sinf = _rope_tables(geom)
    q, k = qk_prep(qkv, cosf, sinf, q_norm_g, k_norm_g, geom)
    att = attention(q, k, qkv, geom)
    return matmul_residual([att], [w_o.astype(BF16)], xs, mods, layer, 2, n_rows,
                           geom.n_lat_tiles, geom.lat_block, geom.lat_mod_row)


def moe_layer(xs, mods, layer, n_rows, n_tiles, row_fn, norm_g, router_g, router_e, w1, w3, w2,
              final_gain):
    d = xs.shape[1]
    hm, route = norm_mod(xs, norm_g, mods, layer, 3, n_rows, n_tiles, lambda t: t, row_fn,
                         router_w=_router_weights(router_g, router_e))
    w1 = w1.reshape(N_EXPERTS, d, -1).astype(BF16)
    w3 = w3.reshape(N_EXPERTS, d, -1).astype(BF16)
    w2 = w2.reshape(N_EXPERTS, -1, d).astype(BF16)
    y0, y1 = hier_moe_block(hm, route, w1, w3, w2)
    return moe_combine(y0, y1, xs, mods, layer, n_rows, n_tiles, lambda t: t, row_fn,
                       final_gain=final_gain)


def kernel(x, c, ctx, c_ctx, w_ada, b_ada, norm_mix, norm_ffn, w_in0, conv_dw, conv_ln_g, conv_ln_b, ssm_conv_w, ssm_conv_b, ssm_dt_bias, ssm_a_log, ssm_d, ssm_norm_g, w_out0, w_qkv, q_norm_g, k_norm_g, w_o, moe_router_g, moe_router_e, moe_w1, moe_w3, moe_w2, norm_final):
    bsz, n_lat, d = x.shape
    lc = ctx.shape[1]
    depth = w_ada.shape[0]
    geom = Geom(bsz, lc, n_lat)
    n_rows = 16

    cvec = jnp.concatenate([c, c_ctx[None, :], jnp.zeros((n_rows - bsz - 1, d), F32)], axis=0)
    mods = ada_mods(cvec, w_ada, b_ada).reshape(depth * n_rows * N_MOD, 1, d)
    xs = jnp.concatenate([ctx, x], axis=1).reshape(bsz * (lc + n_lat), d)

    for i in range(depth):
        last = i == depth - 1
        j = i // 2
        if i % 2 == 0:
            if last:
                raise NotImplementedError("a conv/SSD mixer in the last layer is not supported")
            xs = mixer_even(xs, mods, i, j, n_rows, geom, norm_mix[i], w_in0[j], conv_dw[j],
                            conv_ln_g[j], conv_ln_b[j], ssm_conv_w[j], ssm_conv_b[j],
                            ssm_dt_bias[j], ssm_a_log[j], ssm_d[j], ssm_norm_g[j], w_out0[j])
            n_tiles, row_fn = geom.n_all, geom.all_mod_row
        else:
            if not last:
                raise NotImplementedError("an attention mixer before the last layer is not supported")
            xs = mixer_odd(xs, mods, i, n_rows, geom, norm_mix[i], w_qkv[j], q_norm_g[j],
                           k_norm_g[j], w_o[j])
            n_tiles, row_fn = geom.n_lat_tiles, geom.lat_mod_row
        xs = moe_layer(xs, mods, i, n_rows, n_tiles, row_fn, norm_ffn[i], moe_router_g[i],
                       moe_router_e[i], moe_w1[i], moe_w3[i], moe_w2[i],
                       norm_final if last else None)
    return xs.reshape(bsz, n_lat, d)
```

```python
import functools
import math

import jax
import jax.numpy as jnp
from jax import lax
from jax.experimental import pallas as pl
from jax.experimental.pallas import tpu as pltpu

F32 = jnp.float32
BF16 = jnp.bfloat16

EPS = 1e-6
N_MOD = 6
GRID_W = 64
ROPE_THETA = 10000.0
CONV_WIDTH = 31
SSM_CONV_WIDTH = 5
SSM_HEADS = 32
SSM_HEAD_DIM = 64
SSM_GROUPS = 8
SSM_STATE = 128
SSM_CHUNK = 128
ATT_HEADS = 32
ATT_KV_HEADS = 8
ATT_HEAD_DIM = 128
MOE_GROUPS = 4
MOE_EXPERTS = 8
N_EXPERTS = MOE_GROUPS * MOE_EXPERTS

LANES = 128
ROW_TILE = 256
HALO = 16
VMEM_LIMIT = 56 * 1024 * 1024


def _cparams(sem):
    return pltpu.CompilerParams(dimension_semantics=sem, vmem_limit_bytes=VMEM_LIMIT)


def _silu(v):
    return v * (1.0 / (1.0 + jnp.exp(-v)))


def _sigmoid(v):
    return 1.0 / (1.0 + jnp.exp(-v))


def _ada_kernel(c_ref, w_ref, b_ref, o_ref):
    s = _silu(c_ref[...]).astype(BF16)
    w = w_ref[...].astype(BF16)
    o_ref[...] = jnp.dot(s, w, preferred_element_type=F32) + b_ref[...]


def ada_mods(cvec, w_ada, b_ada, tn=512):
    depth, d, n = w_ada.shape
    rows = cvec.shape[0]
    return pl.pallas_call(
        _ada_kernel,
        grid=(depth, n // tn),
        in_specs=[
            pl.BlockSpec((rows, d), lambda l, j: (0, 0)),
            pl.BlockSpec((None, d, tn), lambda l, j: (l, 0, j)),
            pl.BlockSpec((None, 1, tn), lambda l, j: (l, 0, j)),
        ],
        out_specs=pl.BlockSpec((None, rows, tn), lambda l, j: (l, 0, j)),
        out_shape=jax.ShapeDtypeStruct((depth, rows, n), F32),
        compiler_params=_cparams(("arbitrary", "arbitrary")),
        name="ada_mods",
    )(cvec, w_ada, b_ada.reshape(depth, 1, n))


class Geom:
    def __init__(self, bsz, lc, n_lat):
        self.bsz, self.lc, self.n_lat = bsz, lc, n_lat
        self.per_b = (lc + n_lat) // ROW_TILE
        self.ctx_t = lc // ROW_TILE
        self.lat_t = n_lat // ROW_TILE
        self.n_all = bsz * self.per_b
        self.n_lat_tiles = bsz * self.lat_t

    def all_block(self, i):
        return i

    def lat_block(self, i):
        return (i // self.lat_t) * self.per_b + self.ctx_t + i % self.lat_t

    def all_mod_row(self, i):
        return jnp.where(i % self.per_b < self.ctx_t, self.bsz, i // self.per_b)

    def lat_mod_row(self, i):
        return i // self.lat_t


def _mod_spec(layer, m, row_fn, n_rows, grid_rank, tile_axis, width=None, col_axis=None):
    def imap(*g):
        r = row_fn(g[tile_axis])
        c = 0 if col_axis is None else g[col_axis]
        return ((layer * n_rows + r) * N_MOD + m, 0, c)
    return imap


def _norm_mod_kernel(x_ref, g_ref, sh_ref, sc_ref, h_ref):
    x = x_ref[...]
    ms = jnp.mean(x * x, axis=-1, keepdims=True)
    xn = x * lax.rsqrt(ms + EPS) * g_ref[...]
    h_ref[...] = (xn * (1.0 + sc_ref[...]) + sh_ref[...]).astype(h_ref.dtype)


def _route(logits):
    lane = lax.broadcasted_iota(jnp.int32, logits.shape, 1)
    neg = jnp.float32(-jnp.inf)
    big = jnp.int32(LANES)
    lg = jnp.where(lane < MOE_GROUPS, logits, neg)
    mg = jnp.max(lg, axis=-1, keepdims=True)
    gsel = jnp.min(jnp.where(lg == mg, lane, big), axis=-1, keepdims=True)
    pg = 1.0 / jnp.sum(jnp.exp(lg - mg), axis=-1, keepdims=True)
    lo = MOE_GROUPS + gsel * MOE_EXPERTS
    le = jnp.where((lane >= lo) & (lane < lo + MOE_EXPERTS), logits, neg)
    v1 = jnp.max(le, axis=-1, keepdims=True)
    i1 = jnp.min(jnp.where(le == v1, lane, big), axis=-1, keepdims=True)
    le2 = jnp.where(lane == i1, neg, le)
    v2 = jnp.max(le2, axis=-1, keepdims=True)
    i2 = jnp.min(jnp.where(le2 == v2, lane, big), axis=-1, keepdims=True)
    e2 = jnp.exp(v2 - v1)
    w1 = pg / (1.0 + e2)
    w2 = pg * e2 / (1.0 + e2)
    id1 = (i1 - MOE_GROUPS).astype(F32)
    id2 = (i2 - MOE_GROUPS).astype(F32)
    return jnp.where(lane == 0, id1, jnp.where(lane == 1, id2,
                     jnp.where(lane == 2, w1, jnp.where(lane == 3, w2, 0.0))))


def _to_row_major(x):
    parts = [x[:, s * LANES:(s + 1) * LANES] for s in range(x.shape[1] // LANES)]
    return pltpu.einshape("stl->tsl", jnp.stack(parts, axis=0))


def _from_row_major(x3, dtype):
    xt = pltpu.einshape("tsl->stl", x3)
    return jnp.concatenate([xt[s].astype(dtype) for s in range(x3.shape[1])], axis=1)


def _norm_mod_route_kernel(x_ref, g_ref, sh_ref, sc_ref, wr_ref, h_ref, r_ref):
    x = x_ref[...]
    ms = jnp.mean(x * x, axis=-1, keepdims=True)
    xn = x * lax.rsqrt(ms + EPS) * g_ref[...]
    h = xn * (1.0 + sc_ref[...]) + sh_ref[...]
    h_ref[...] = _to_row_major(h)
    logits = jnp.dot(h, wr_ref[...], preferred_element_type=F32,
                     precision=lax.Precision.HIGHEST)
    r_ref[...] = _route(logits)


def norm_mod(x, gain, mods, layer, m_shift, n_rows, n_tiles, block_fn, row_fn,
             router_w=None, out_dtype=BF16):
    d = x.shape[1]
    x_spec = pl.BlockSpec((ROW_TILE, d), lambda i: (block_fn(i), 0))
    g_spec = pl.BlockSpec((1, d), lambda i: (0, 0))
    sh_spec = pl.BlockSpec((None, 1, d), _mod_spec(layer, m_shift, row_fn, n_rows, 1, 0))
    sc_spec = pl.BlockSpec((None, 1, d), _mod_spec(layer, m_shift + 1, row_fn, n_rows, 1, 0))
    o_spec = pl.BlockSpec((ROW_TILE, d), lambda i: (i, 0))
    rows = n_tiles * ROW_TILE
    if router_w is None:
        return pl.pallas_call(
            _norm_mod_kernel,
            grid=(n_tiles,),
            in_specs=[x_spec, g_spec, sh_spec, sc_spec],
            out_specs=o_spec,
            out_shape=jax.ShapeDtypeStruct((rows, d), out_dtype),
            compiler_params=_cparams(("parallel",)),
            name="norm_mod",
        )(x, gain.reshape(1, d), mods, mods)
    return pl.pallas_call(
        _norm_mod_route_kernel,
        grid=(n_tiles,),
        in_specs=[x_spec, g_spec, sh_spec, sc_spec,
                  pl.BlockSpec((d, LANES), lambda i: (0, 0))],
        out_specs=[pl.BlockSpec((ROW_TILE, d // LANES, LANES), lambda i: (i, 0, 0)),
                   pl.BlockSpec((ROW_TILE, LANES), lambda i: (i, 0))],
        out_shape=[jax.ShapeDtypeStruct((rows, d // LANES, LANES), F32),
                   jax.ShapeDtypeStruct((rows, LANES), F32)],
        compiler_params=_cparams(("parallel",)),
        name="norm_mod_route",
    )(x, gain.reshape(1, d), mods, mods, router_w)


def _stage_weights(w_refs, wb_refs):
    @pl.when(pl.program_id(1) == 0)
    def _():
        for w_ref, wb_ref in zip(w_refs, wb_refs):
            wb_ref[...] = w_ref[...].astype(BF16)


def _mm_kernel(x_ref, w_ref, o_ref, wb_ref):
    _stage_weights([w_ref], [wb_ref])
    o_ref[...] = jnp.dot(x_ref[...], wb_ref[...], preferred_element_type=F32).astype(o_ref.dtype)


def matmul(x, w, col0, n, out_dtype, tm=512, tn=512):
    m, k = x.shape
    tn = min(tn, n)
    cb = col0 // tn
    return pl.pallas_call(
        _mm_kernel,
        grid=(n // tn, m // tm),
        in_specs=[pl.BlockSpec((tm, k), lambda j, i: (i, 0)),
                  pl.BlockSpec((k, tn), lambda j, i: (0, cb + j))],
        out_specs=pl.BlockSpec((tm, tn), lambda j, i: (i, j)),
        out_shape=jax.ShapeDtypeStruct((m, n), out_dtype),
        scratch_shapes=[pltpu.VMEM((k, tn), BF16)],
        compiler_params=_cparams(("parallel", "arbitrary")),
        name="matmul",
    )(x, w)


def _mm_glu_kernel(x_ref, wa_ref, wg_ref, o_ref, wab_ref, wgb_ref):
    _stage_weights([wa_ref, wg_ref], [wab_ref, wgb_ref])
    x = x_ref[...]
    a = jnp.dot(x, wab_ref[...], preferred_element_type=F32)
    b = jnp.dot(x, wgb_ref[...], preferred_element_type=F32)
    o_ref[...] = (a * _sigmoid(b)).astype(o_ref.dtype)


def matmul_glu(x, w, col_a, col_g, n, out_dtype, tm=512, tn=256):
    m, k = x.shape
    ca, cg = col_a // tn, col_g // tn
    return pl.pallas_call(
        _mm_glu_kernel,
        grid=(n // tn, m // tm),
        in_specs=[pl.BlockSpec((tm, k), lambda j, i: (i, 0)),
                  pl.BlockSpec((k, tn), lambda j, i: (0, ca + j)),
                  pl.BlockSpec((k, tn), lambda j, i: (0, cg + j))],
        out_specs=pl.BlockSpec((tm, tn), lambda j, i: (i, j)),
        out_shape=jax.ShapeDtypeStruct((m, n), out_dtype),
        scratch_shapes=[pltpu.VMEM((k, tn), BF16), pltpu.VMEM((k, tn), BF16)],
        compiler_params=_cparams(("parallel", "arbitrary")),
        name="matmul_glu",
    )(x, w, w)


def _mm_res_kernel(n_lhs, *refs):
    xs = refs[:n_lhs]
    ws = refs[n_lhs:2 * n_lhs]
    res_ref, gate_ref, o_ref = refs[2 * n_lhs:2 * n_lhs + 3]
    wbs = refs[2 * n_lhs + 3:]
    _stage_weights(ws, wbs)
    acc = jnp.dot(xs[0][...], wbs[0][...], preferred_element_type=F32)
    for x_ref, wb_ref in zip(xs[1:], wbs[1:]):
        acc = acc + jnp.dot(x_ref[...], wb_ref[...], preferred_element_type=F32)
    o_ref[...] = res_ref[...] + gate_ref[...] * acc


def matmul_residual(lhs, w, res, mods, layer, m_gate, n_rows, n_tiles, res_block_fn, row_fn,
                    tn=512):
    n_lhs = len(lhs)
    n = w.shape[1]
    widths = [a.shape[1] for a in lhs]
    assert all(wd == widths[0] for wd in widths) and sum(widths) == w.shape[0]
    in_specs = [pl.BlockSpec((ROW_TILE, wd), lambda j, i: (i, 0)) for wd in widths]
    in_specs += [pl.BlockSpec((wd, tn), functools.partial(lambda q, j, i: (q, j), q))
                 for q, wd in enumerate(widths)]
    in_specs += [
        pl.BlockSpec((ROW_TILE, tn), lambda j, i: (res_block_fn(i), j)),
        pl.BlockSpec((None, 1, tn), _mod_spec(layer, m_gate, row_fn, n_rows, 2, 1, col_axis=0)),
    ]
    return pl.pallas_call(
        functools.partial(_mm_res_kernel, n_lhs),
        grid=(n // tn, n_tiles),
        in_specs=in_specs,
        out_specs=pl.BlockSpec((ROW_TILE, tn), lambda j, i: (i, j)),
        out_shape=jax.ShapeDtypeStruct((n_tiles * ROW_TILE, n), F32),
        scratch_shapes=[pltpu.VMEM((wd, tn), BF16) for wd in widths],
        compiler_params=_cparams(("parallel", "arbitrary")),
        name="matmul_residual",
    )(*lhs, *([w] * n_lhs), res, mods)


def _conv_kernel(width, mode, per_b, ctx_t, x_ref, prev_ref, next_ref, w_ref, p0_ref, p1_ref,
                 o_ref, xp_ref, acc_ref):
    t = pl.program_id(0) % per_b
    first = (t == 0) | (t == ctx_t)
    last = (t == ctx_t - 1) | (t == per_b - 1)
    c = x_ref.shape[1]
    xp_ref[0:HALO, :] = jnp.where(first, 0.0, prev_ref[...])
    xp_ref[HALO:HALO + ROW_TILE, :] = x_ref[...]
    xp_ref[HALO + ROW_TILE:, :] = jnp.where(last, 0.0, next_ref[...])
    off = HALO - width // 2

    def chunk(j, carry):
        cs = pl.ds(pl.multiple_of(j * LANES, LANES), LANES)
        acc = jnp.zeros((ROW_TILE, LANES), F32)
        for k in range(width):
            acc = acc + xp_ref[off + k:off + k + ROW_TILE, cs] * w_ref[k:k + 1, cs]
        acc_ref[:, cs] = acc
        return carry

    lax.fori_loop(0, c // LANES, chunk, 0)
    y = acc_ref[...]
    if mode == "layernorm_silu":
        mu = jnp.mean(y, axis=-1, keepdims=True)
        yc = y - mu
        var = jnp.mean(yc * yc, axis=-1, keepdims=True)
        y = yc * lax.rsqrt(var + EPS) * p0_ref[...] + p1_ref[...]
    else:
        y = y + p0_ref[...]
    o_ref[...] = _silu(y).astype(o_ref.dtype)


def seq_conv(x, w, p0, p1, mode, geom, out_dtype=BF16):
    rows, c = x.shape
    width = w.shape[0]
    hb = ROW_TILE // HALO
    n_halo = rows // HALO
    kern = functools.partial(_conv_kernel, width, mode, geom.per_b, geom.ctx_t)
    return pl.pallas_call(
        kern,
        grid=(geom.n_all,),
        in_specs=[
            pl.BlockSpec((ROW_TILE, c), lambda i: (i, 0)),
            pl.BlockSpec((HALO, c), lambda i: (jnp.maximum(i * hb - 1, 0), 0)),
            pl.BlockSpec((HALO, c), lambda i: (jnp.minimum((i + 1) * hb, n_halo - 1), 0)),
            pl.BlockSpec((width, c), lambda i: (0, 0)),
            pl.BlockSpec((1, c), lambda i: (0, 0)),
            pl.BlockSpec((1, c), lambda i: (0, 0)),
        ],
        out_specs=pl.BlockSpec((ROW_TILE, c), lambda i: (i, 0)),
        out_shape=jax.ShapeDtypeStruct((rows, c), out_dtype),
        scratch_shapes=[pltpu.VMEM((ROW_TILE + 2 * HALO, c), F32),
                        pltpu.VMEM((ROW_TILE, c), F32)],
        compiler_params=_cparams(("parallel",)),
        name="seq_conv_" + mode,
    )(x, x, x, w, p0.reshape(1, c), p1.reshape(1, c))


def _ssd_kernel(n_chunks, ctx_chunks, xs_ref, b_ref, c_ref, dtraw_ref, bias_ref, aneg_ref,
                y_ref, state_ref, tr_ref):
    d = pl.program_id(1)
    T = SSM_CHUNK
    rows = lax.broadcasted_iota(jnp.int32, (T, T), 0)
    cols = lax.broadcasted_iota(jnp.int32, (T, T), 1)
    fwd = d == 0
    diff = (rows - cols) * jnp.where(fwd, 1, -1)
    tri = jnp.where(diff <= 0, 1.0, 0.0).astype(F32)
    keep = diff >= 0
    lane = lax.broadcasted_iota(jnp.int32, (T, T), 1)
    left = lane < SSM_HEAD_DIM
    state_ref[...] = jnp.zeros_like(state_ref)
    tr_ref[...] = jnp.zeros_like(tr_ref)
    bias = bias_ref[...]
    aneg = aneg_ref[...]
    n_heads = 4

    def step(s, carry):
        bwd_c = jnp.where(s < ctx_chunks, ctx_chunks - 1 - s, n_chunks - 1 + ctx_chunks - s)
        c = jnp.where(fwd, s, bwd_c)
        r0 = pl.multiple_of(c * T, T)
        raw = dtraw_ref[:, pl.ds(r0, T)]
        z = raw + bias
        dt_row = jnp.maximum(z, 0.0) + jnp.log1p(jnp.exp(-jnp.abs(z)))
        da_row = dt_row * aneg
        cum_row = jnp.dot(da_row, tri, preferred_element_type=F32,
                          precision=lax.Precision.HIGHEST)
        tot_row = jnp.sum(da_row, axis=-1, keepdims=True)
        tr_ref[0:8, :] = cum_row
        colform = tr_ref[...].T
        bmat = b_ref[pl.ds(r0, T), :]
        cmat = c_ref[pl.ds(r0, T), :]
        cb = lax.dot_general(cmat, bmat, (((1,), (1,)), ((), ())),
                             preferred_element_type=F32)
        bt = bmat.astype(F32).T
        for pair in range(n_heads // 2):
            xs_pair = xs_ref[pl.ds(r0, T), pair * LANES:(pair + 1) * LANES]
            ms, bds, ecs, ets = [], [], [], []
            for q in range(2):
                r = pair * 2 + q
                cum_c = jnp.broadcast_to(colform[:, r:r + 1], (T, T))
                cum_r = cum_row[r:r + 1, :]
                dt_r = dt_row[r:r + 1, :]
                seg = jnp.where(keep, cum_c - cum_r, -jnp.inf)
                ms.append((cb * jnp.exp(seg) * dt_r).astype(BF16))
                dec_r = jnp.exp(tot_row[r:r + 1, :] - cum_r) * dt_r
                bds.append((bt * dec_r).astype(BF16))
                ecs.append(jnp.exp(cum_c))
                ets.append(jnp.exp(tot_row[r:r + 1, :]))
            xs32 = xs_pair.astype(F32)
            xd2 = jnp.concatenate([jnp.where(left, xs32, 0.0).astype(BF16),
                                   jnp.where(left, 0.0, xs32).astype(BF16)], axis=0)
            m2 = jnp.concatenate(ms, axis=1)
            y_diag = jnp.dot(m2, xd2, preferred_element_type=F32)
            st = state_ref[pair]
            y_off = jnp.dot(cmat, st.astype(BF16), preferred_element_type=F32)
            y_off = y_off * jnp.where(left, ecs[0], ecs[1])
            y_ref[pl.ds(r0, T), pair * LANES:(pair + 1) * LANES] = (y_diag + y_off).astype(y_ref.dtype)
            bd2 = jnp.concatenate(bds, axis=1)
            upd = jnp.dot(bd2, xd2, preferred_element_type=F32)
            state_ref[pair] = st * jnp.where(left[0:1, :], ets[0], ets[1]) + upd
        return carry

    lax.fori_loop(0, n_chunks, step, 0, unroll=2)


def ssd_scan(xbc, dtraw_rows, bias_rows, aneg_rows, geom):
    rows = xbc.shape[0]
    per = geom.lc + geom.n_lat
    n_chunks = per // SSM_CHUNK
    ctx_chunks = geom.lc // SSM_CHUNK
    ssm_d = SSM_HEADS * SSM_HEAD_DIM
    gw = 4 * SSM_HEAD_DIM
    xs_blocks = ssm_d // gw
    kern = functools.partial(_ssd_kernel, n_chunks, ctx_chunks)
    return pl.pallas_call(
        kern,
        grid=(geom.bsz, 2, SSM_GROUPS),
        in_specs=[
            pl.BlockSpec((per, gw), lambda b, d, g: (b, g)),
            pl.BlockSpec((per, SSM_STATE), lambda b, d, g: (b, 2 * xs_blocks + g)),
            pl.BlockSpec((per, SSM_STATE), lambda b, d, g: (b, 2 * xs_blocks + SSM_GROUPS + g)),
            pl.BlockSpec((None, None, None, 8, per), lambda b, d, g: (b, d, g, 0, 0)),
            pl.BlockSpec((None, None, 8, LANES), lambda b, d, g: (d, g, 0, 0)),
            pl.BlockSpec((None, None, 8, LANES), lambda b, d, g: (d, g, 0, 0)),
        ],
        out_specs=pl.BlockSpec((None, per, gw), lambda b, d, g: (d, b, g)),
        out_shape=jax.ShapeDtypeStruct((2, rows, ssm_d), BF16),
        scratch_shapes=[pltpu.VMEM((2, SSM_STATE, LANES), F32),
                        pltpu.VMEM((LANES, SSM_CHUNK), F32)],
        compiler_params=_cparams(("parallel", "parallel", "parallel")),
        name="ssd_scan",
    )(xbc, xbc, xbc, dtraw_rows, bias_rows, aneg_rows)


def _ssd_merge_kernel(yf_ref, yb_ref, xs_ref, z_ref, d_ref, g_ref, o_ref):
    y = d_ref[...] * xs_ref[...].astype(F32) + yf_ref[...].astype(F32) + yb_ref[...].astype(F32)
    gated = y * _silu(z_ref[...].astype(F32))
    ms = jnp.mean(gated * gated, axis=-1, keepdims=True)
    o_ref[...] = (gated * lax.rsqrt(ms + EPS) * g_ref[...]).astype(o_ref.dtype)


def ssd_merge(y2, xbc, z, d_full, g, geom):
    rows, ssm_d = z.shape
    spec = pl.BlockSpec((ROW_TILE, ssm_d), lambda i: (i, 0))
    vec = pl.BlockSpec((1, ssm_d), lambda i: (0, 0))
    return pl.pallas_call(
        _ssd_merge_kernel,
        grid=(geom.n_all,),
        in_specs=[pl.BlockSpec((None, ROW_TILE, ssm_d), lambda i: (0, i, 0)),
                  pl.BlockSpec((None, ROW_TILE, ssm_d), lambda i: (1, i, 0)),
                  spec, spec, vec, vec],
        out_specs=spec,
        out_shape=jax.ShapeDtypeStruct((rows, ssm_d), BF16),
        compiler_params=_cparams(("parallel",)),
        name="ssd_merge",
    )(y2, y2, xbc, z, d_full.reshape(1, ssm_d), g.reshape(1, ssm_d))


def _qk_prep_kernel(n_q, n_k, scale, qkv_ref, cos_ref, sin_ref, gq_ref, gk_ref, q_ref, k_ref):
    cosf = cos_ref[...]
    sinf = sin_ref[...]
    hd = ATT_HEAD_DIM
    for h in range(n_q + n_k):
        x = qkv_ref[:, h * hd:(h + 1) * hd].astype(F32)
        g = gq_ref[...] if h < n_q else gk_ref[...]
        ms = jnp.mean(x * x, axis=-1, keepdims=True)
        xn = x * lax.rsqrt(ms + EPS) * g
        rot = pltpu.roll(xn, hd // 2, axis=1)
        o = xn * cosf + rot * sinf
        if h < n_q:
            q_ref[:, h * hd:(h + 1) * hd] = (o * scale).astype(q_ref.dtype)
        else:
            k_ref[:, (h - n_q) * hd:(h - n_q + 1) * hd] = o.astype(k_ref.dtype)


def qk_prep(qkv, cosf, sinf, gq, gk, geom):
    rows = qkv.shape[0]
    hd = ATT_HEAD_DIM
    qc, kc = ATT_HEADS * hd, ATT_KV_HEADS * hd
    kern = functools.partial(_qk_prep_kernel, ATT_HEADS, ATT_KV_HEADS, hd ** -0.5)
    per_b = geom.per_b
    return pl.pallas_call(
        kern,
        grid=(geom.n_all,),
        in_specs=[pl.BlockSpec((ROW_TILE, qc + kc), lambda i: (i, 0)),
                  pl.BlockSpec((ROW_TILE, hd), lambda i: (i % per_b, 0)),
                  pl.BlockSpec((ROW_TILE, hd), lambda i: (i % per_b, 0)),
                  pl.BlockSpec((1, hd), lambda i: (0, 0)),
                  pl.BlockSpec((1, hd), lambda i: (0, 0))],
        out_specs=[pl.BlockSpec((ROW_TILE, qc), lambda i: (i, 0)),
                   pl.BlockSpec((ROW_TILE, kc), lambda i: (i, 0))],
        out_shape=[jax.ShapeDtypeStruct((rows, qc), BF16),
                   jax.ShapeDtypeStruct((rows, kc), BF16)],
        compiler_params=_cparams(("parallel",)),
        name="qk_prep",
    )(qkv, cosf, sinf, gq.reshape(1, hd), gk.reshape(1, hd))


def _attn_kernel(rep, q_ref, k_ref, v_ref, o_ref):
    hd = ATT_HEAD_DIM
    k = k_ref[...]
    v = v_ref[...]
    v1 = jnp.concatenate([v, jnp.ones_like(v)], axis=1)
    for r in range(rep):
        q = q_ref[:, r * hd:(r + 1) * hd]
        s = lax.dot_general(q, k, (((1,), (1,)), ((), ())), preferred_element_type=F32)
        m = jnp.max(s, axis=-1, keepdims=True)
        p = jnp.exp((s - m).astype(BF16))
        o = jnp.dot(p, v1, preferred_element_type=F32)
        o_ref[:, r * hd:(r + 1) * hd] = (o[:, :hd] / o[:, hd:hd + 1]).astype(o_ref.dtype)


def attention(q, k, qkv, geom):
    hd = ATT_HEAD_DIM
    rep = ATT_HEADS // ATT_KV_HEADS
    per = geom.lc + geom.n_lat
    v_col0 = (ATT_HEADS + ATT_KV_HEADS)
    lat_t, per_b, ctx_t = geom.lat_t, geom.per_b, geom.ctx_t
    return pl.pallas_call(
        functools.partial(_attn_kernel, rep),
        grid=(geom.bsz, ATT_KV_HEADS, lat_t),
        in_specs=[pl.BlockSpec((ROW_TILE, rep * hd), lambda b, g, t: (b * per_b + ctx_t + t, g)),
                  pl.BlockSpec((per, hd), lambda b, g, t: (b, g)),
                  pl.BlockSpec((per, hd), lambda b, g, t: (b, v_col0 + g))],
        out_specs=pl.BlockSpec((ROW_TILE, rep * hd), lambda b, g, t: (b * lat_t + t, g)),
        out_shape=jax.ShapeDtypeStruct((geom.bsz * geom.n_lat, ATT_HEADS * hd), BF16),
        compiler_params=_cparams(("parallel", "parallel", "parallel")),
        name="attention",
    )(q, k, qkv)


def _start_row_gather(ids_ref, first, stride, src_ref, dst_ref, sem):
    def issue(r, carry):
        row = ids_ref[first + stride * r]
        pltpu.make_async_copy(src_ref.at[pl.ds(row, 1)], dst_ref.at[pl.ds(r, 1)], sem).start()
        return carry

    lax.fori_loop(0, ROW_TILE, issue, 0)


def _wait_row_gather(src_ref, dst_ref, sem):
    pltpu.make_async_copy(src_ref.at[pl.ds(0, ROW_TILE)], dst_ref, sem).wait()


def _cast_kernel(x_ref, o_ref):
    o_ref[...] = x_ref[...].astype(o_ref.dtype)


def cast_experts(w, dtype=BF16):
    e, k, n = w.shape
    spec = pl.BlockSpec((None, k, n), lambda i: (i, 0, 0))
    return pl.pallas_call(
        _cast_kernel,
        grid=(e,),
        in_specs=[spec],
        out_specs=spec,
        out_shape=jax.ShapeDtypeStruct(w.shape, dtype),
        compiler_params=_cparams(("parallel",)),
        name="cast_experts",
    )(w)


def _expert_mlp_kernel(te_ref, nt_ref, ids_ref, h_ref, cw_ref, w1_ref, w3_ref, w2_ref, o_ref,
                       xbuf_ref, sem):
    t = pl.program_id(0)
    nt = nt_ref[0]
    slot = t % 2

    @pl.when(t == 0)
    def _():
        _start_row_gather(ids_ref, 0, 1, h_ref, xbuf_ref.at[0], sem.at[0])

    @pl.when(t + 1 < nt)
    def _():
        _start_row_gather(ids_ref, (t + 1) * ROW_TILE, 1, h_ref, xbuf_ref.at[1 - slot],
                          sem.at[1 - slot])

    @pl.when(t < nt)
    def _():
        _wait_row_gather(h_ref, xbuf_ref.at[slot], sem.at[slot])
        x = _from_row_major(xbuf_ref[slot], BF16)
        a = jnp.dot(x, w1_ref[...], preferred_element_type=F32)
        b = jnp.dot(x, w3_ref[...], preferred_element_type=F32)
        hid = _silu(a) * b * cw_ref[...]
        y = jnp.dot(hid.astype(BF16), w2_ref[...], preferred_element_type=F32)
        o_ref[...] = _to_row_major(y)

    @pl.when(t >= nt)
    def _():
        o_ref[...] = jnp.zeros_like(o_ref)


def expert_mlp(h3, row_ids, cw, w1, w3, w2, tile_expert, n_tiles_used):
    p = row_ids.shape[0]
    _, s, l = h3.shape
    d = s * l
    hdim = w1.shape[2]
    n_tiles = p // ROW_TILE

    def wmap(t, te, nt, ids):
        return (te[t], 0, 0)

    return pl.pallas_call(
        _expert_mlp_kernel,
        grid_spec=pltpu.PrefetchScalarGridSpec(
            num_scalar_prefetch=3,
            grid=(n_tiles,),
            in_specs=[pl.BlockSpec(memory_space=pl.ANY),
                      pl.BlockSpec((ROW_TILE, 1), lambda t, te, nt, ids: (t, 0)),
                      pl.BlockSpec((None, d, hdim), wmap),
                      pl.BlockSpec((None, d, hdim), wmap),
                      pl.BlockSpec((None, hdim, d), wmap)],
            out_specs=pl.BlockSpec((ROW_TILE, s, l), lambda t, te, nt, ids: (t, 0, 0)),
            scratch_shapes=[pltpu.VMEM((2, ROW_TILE, s, l), F32), pltpu.SemaphoreType.DMA((2,))],
        ),
        out_shape=jax.ShapeDtypeStruct((p, s, l), F32),
        compiler_params=_cparams(("arbitrary",)),
        name="expert_mlp",
    )(tile_expert, n_tiles_used, row_ids, h3, cw, w1, w3, w2)


def _moe_combine_kernel(final_norm, pos_ref, ys_ref, res_ref, gate_ref, g_ref, o_ref, buf_ref, sem):
    i = pl.program_id(0)
    n = pl.num_programs(0)
    slot = i % 2

    def start(tile, s):
        for q in range(2):
            _start_row_gather(pos_ref, tile * (2 * ROW_TILE) + q, 2, ys_ref, buf_ref.at[s, q],
                              sem.at[s, q])

    @pl.when(i == 0)
    def _():
        start(0, 0)

    @pl.when(i + 1 < n)
    def _():
        start(i + 1, 1 - slot)

    for q in range(2):
        _wait_row_gather(ys_ref, buf_ref.at[slot, q], sem.at[slot, q])
    y = _from_row_major(buf_ref[slot, 0] + buf_ref[slot, 1], F32)
    x = res_ref[...] + gate_ref[...] * y
    if final_norm:
        ms = jnp.mean(x * x, axis=-1, keepdims=True)
        x = x * lax.rsqrt(ms + EPS) * g_ref[...]
    o_ref[...] = x


def moe_combine(ys, pos, res, mods, layer, n_rows, n_tiles, row_fn, final_gain=None):
    d = res.shape[1]
    _, s, l = ys.shape
    tile = pl.BlockSpec((ROW_TILE, d), lambda i, pos: (i, 0))
    mod_map = _mod_spec(layer, 5, row_fn, n_rows, 1, 0)
    gain = jnp.ones((d,), F32) if final_gain is None else final_gain
    return pl.pallas_call(
        functools.partial(_moe_combine_kernel, final_gain is not None),
        grid_spec=pltpu.PrefetchScalarGridSpec(
            num_scalar_prefetch=1,
            grid=(n_tiles,),
            in_specs=[pl.BlockSpec(memory_space=pl.ANY),
                      tile,
                      pl.BlockSpec((None, 1, d), lambda i, pos: mod_map(i)),
                      pl.BlockSpec((1, d), lambda i, pos: (0, 0))],
            out_specs=tile,
            scratch_shapes=[pltpu.VMEM((2, 2, ROW_TILE, s, l), F32),
                            pltpu.SemaphoreType.DMA((2, 2))],
        ),
        out_shape=jax.ShapeDtypeStruct((n_tiles * ROW_TILE, d), F32),
        compiler_params=_cparams(("arbitrary",)),
        name="moe_combine",
    )(pos.reshape(-1), ys, res, mods, gain.reshape(1, d))


def _dispatch_plan(route):
    t = route.shape[0]
    ids = route[:, 0:2].astype(jnp.int32)
    wts = route[:, 2:4]
    flat = ids.reshape(-1)
    n_assign = flat.shape[0]
    p_max = n_assign + N_EXPERTS * ROW_TILE
    p_max = (p_max // ROW_TILE) * ROW_TILE
    onehot = (flat[:, None] == jnp.arange(N_EXPERTS, dtype=jnp.int32)[None, :]).astype(jnp.int32)
    counts = jnp.sum(onehot, axis=0)
    padded = ((counts + ROW_TILE - 1) // ROW_TILE) * ROW_TILE
    ends = jnp.cumsum(padded)
    offs = ends - padded
    starts = jnp.cumsum(counts) - counts
    order = jnp.argsort(flat, stable=True).astype(jnp.int32)
    slot = jnp.arange(p_max, dtype=jnp.int32)
    slot_e = jnp.sum((slot[:, None] >= ends[None, :]).astype(jnp.int32), axis=1)
    slot_e = jnp.minimum(slot_e, N_EXPERTS - 1)
    rank = slot - offs[slot_e]
    valid = (rank < counts[slot_e]) & (slot < ends[-1])
    src_sorted = jnp.clip(starts[slot_e] + rank, 0, n_assign - 1)
    assign = order[src_sorted]
    row_ids = jnp.where(valid, assign // 2, 0).astype(jnp.int32)
    cw = jnp.where(valid, wts.reshape(-1)[assign], 0.0).astype(F32)
    sorted_pos = jnp.argsort(order).astype(jnp.int32)
    pos = (sorted_pos + (offs - starts)[flat]).astype(jnp.int32).reshape(t, 2)
    tile_expert = slot_e[::ROW_TILE]
    n_tiles_used = (ends[-1] // ROW_TILE).astype(jnp.int32).reshape(1)
    last_e = tile_expert[jnp.maximum(n_tiles_used[0] - 1, 0)]
    tile_idx = jnp.arange(p_max // ROW_TILE, dtype=jnp.int32)
    tile_expert = jnp.where(tile_idx < n_tiles_used[0], tile_expert, last_e).astype(jnp.int32)
    return row_ids, cw, pos, tile_expert, n_tiles_used


def hier_moe_block(h, route, w1, w3, w2):
    row_ids, cw, pos, tile_expert, n_tiles_used = _dispatch_plan(route)
    ys = expert_mlp(h, row_ids, cw[:, None], w1, w3, w2, tile_expert, n_tiles_used)
    return ys, pos


def _rope_tables(geom):
    n_lat = geom.n_lat
    half = ATT_HEAD_DIM // 2
    nfreq = half // 2
    row = jnp.repeat(jnp.arange(n_lat // GRID_W, dtype=F32), GRID_W)
    col = (jnp.arange(n_lat) % GRID_W).astype(F32)
    inv = ROPE_THETA ** (-jnp.arange(nfreq, dtype=F32) / nfreq)
    ang = jnp.concatenate([row[:, None] * inv, col[:, None] * inv], axis=-1)
    cos, sin = jnp.cos(ang), jnp.sin(ang)
    cosf = jnp.concatenate([cos, cos], axis=-1)
    sinf = jnp.concatenate([-sin, sin], axis=-1)
    cosf = jnp.concatenate([jnp.ones((geom.lc, ATT_HEAD_DIM), F32), cosf], axis=0)
    sinf = jnp.concatenate([jnp.zeros((geom.lc, ATT_HEAD_DIM), F32), sinf], axis=0)
    return cosf, sinf


def _router_weights(router_g, router_e):
    d = router_g.shape[0]
    pad = jnp.zeros((d, LANES - MOE_GROUPS - N_EXPERTS), F32)
    return jnp.concatenate([router_g, router_e, pad], axis=1)


def mixer_even(xs, mods, layer, j, n_rows, geom, norm_g, w_in, conv_dw, conv_ln_g, conv_ln_b,
               ssm_conv_w, ssm_conv_b, ssm_dt_bias, ssm_a_log, ssm_d, ssm_norm_g, w_out,
               debug=False):
    bsz, per = geom.bsz, geom.lc + geom.n_lat
    conv_d = conv_dw.shape[1]
    ssm_dd = SSM_HEADS * SSM_HEAD_DIM
    xbc_dim = ssm_conv_w.shape[1]
    h = norm_mod(xs, norm_g, mods, layer, 0, n_rows, geom.n_all, geom.all_block, geom.all_mod_row)
    c0, c1, c2, c3 = conv_d, 2 * conv_d, 2 * conv_d + ssm_dd, 2 * conv_d + ssm_dd + xbc_dim
    glu = matmul_glu(h, w_in, 0, c0, conv_d, F32)
    z = matmul(h, w_in, c1, ssm_dd, BF16)
    xbc_raw = matmul(h, w_in, c2, xbc_dim, F32)
    w_dt = jnp.pad(w_in[:, c3:], ((0, 0), (0, LANES - 2 * SSM_HEADS)))
    dt_raw = matmul(h, w_dt, 0, LANES, F32)[:, :2 * SSM_HEADS]
    conf = seq_conv(glu, conv_dw, conv_ln_g, conv_ln_b, "layernorm_silu", geom)
    xbc = seq_conv(xbc_raw, ssm_conv_w, ssm_conv_b, ssm_conv_b, "bias_silu", geom)
    dtr = dt_raw.reshape(bsz, per, 2, SSM_GROUPS, 4).transpose(0, 2, 3, 4, 1)
    dtr = jnp.pad(dtr, ((0, 0), (0, 0), (0, 0), (0, 4), (0, 0)))

    def head_rows(v):
        v = v.astype(F32).reshape(2, SSM_GROUPS, 4)
        v = jnp.pad(v, ((0, 0), (0, 0), (0, 4)))
        return jnp.broadcast_to(v[..., None], (2, SSM_GROUPS, 8, LANES))

    y2 = ssd_scan(xbc, dtr, head_rows(ssm_dt_bias), head_rows(-jnp.exp(ssm_a_log.astype(F32))), geom)
    yn = ssd_merge(y2, xbc, z, jnp.repeat(ssm_d, SSM_HEAD_DIM), ssm_norm_g, geom)
    out = matmul_residual([conf, yn], w_out, xs, mods, layer, 2, n_rows, geom.n_all,
                          geom.all_block, geom.all_mod_row)
    if debug:
        return out, dict(h=h, glu=glu, z=z, xbc_raw=xbc_raw, dt_raw=dt_raw, conf=conf, xbc=xbc,
                         y2=y2, yn=yn)
    return out


def mixer_odd(xs, mods, layer, n_rows, geom, norm_g, w_qkv, q_norm_g, k_norm_g, w_o):
    h = norm_mod(xs, norm_g, mods, layer, 0, n_rows, geom.n_all, geom.all_block, geom.all_mod_row)
    qkv = matmul(h, w_qkv, 0, w_qkv.shape[1], BF16)
    cosf, sinf = _rope_tables(geom)
    q, k = qk_prep(qkv, cosf, sinf, q_norm_g, k_norm_g, geom)
    att = attention(q, k, qkv, geom)
    return matmul_residual([att], w_o, xs, mods, layer, 2, n_rows, geom.n_lat_tiles,
                           geom.lat_block, geom.lat_mod_row)


def moe_layer(xs, mods, layer, n_rows, n_tiles, row_fn, norm_g, router_g, router_e, w1, w3, w2,
              final_gain):
    d = xs.shape[1]
    hm, route = norm_mod(xs, norm_g, mods, layer, 3, n_rows, n_tiles, lambda t: t, row_fn,
                         router_w=_router_weights(router_g, router_e))
    w1 = cast_experts(w1.reshape(N_EXPERTS, d, -1))
    w3 = cast_experts(w3.reshape(N_EXPERTS, d, -1))
    w2 = cast_experts(w2.reshape(N_EXPERTS, -1, d))
    ys, pos = hier_moe_block(hm, route, w1, w3, w2)
    return moe_combine(ys, pos, xs, mods, layer, n_rows, n_tiles, row_fn, final_gain=final_gain)


def kernel(x, c, ctx, c_ctx, w_ada, b_ada, norm_mix, norm_ffn, w_in0, conv_dw, conv_ln_g, conv_ln_b, ssm_conv_w, ssm_conv_b, ssm_dt_bias, ssm_a_log, ssm_d, ssm_norm_g, w_out0, w_qkv, q_norm_g, k_norm_g, w_o, moe_router_g, moe_router_e, moe_w1, moe_w3, moe_w2, norm_final):
    bsz, n_lat, d = x.shape
    lc = ctx.shape[1]
    depth = w_ada.shape[0]
    geom = Geom(bsz, lc, n_lat)
    n_rows = 16

    cvec = jnp.concatenate([c, c_ctx[None, :], jnp.zeros((n_rows - bsz - 1, d), F32)], axis=0)
    mods = ada_mods(cvec, w_ada, b_ada).reshape(depth * n_rows * N_MOD, 1, d)
    xs = jnp.concatenate([ctx, x], axis=1).reshape(bsz * (lc + n_lat), d)

    for i in range(depth):
        last = i == depth - 1
        j = i // 2
        if i % 2 == 0:
            if last:
                raise NotImplementedError("a conv/SSD mixer in the last layer is not supported")
            xs = mixer_even(xs, mods, i, j, n_rows, geom, norm_mix[i], w_in0[j], conv_dw[j],
                            conv_ln_g[j], conv_ln_b[j], ssm_conv_w[j], ssm_conv_b[j],
                            ssm_dt_bias[j], ssm_a_log[j], ssm_d[j], ssm_norm_g[j], w_out0[j])
            n_tiles, row_fn = geom.n_all, geom.all_mod_row
        else:
            if not last:
                raise NotImplementedError("an attention mixer before the last layer is not supported")
            xs = mixer_odd(xs, mods, i, n_rows, geom, norm_mix[i], w_qkv[j], q_norm_g[j],
                           k_norm_g[j], w_o[j])
            n_tiles, row_fn = geom.n_lat_tiles, geom.lat_mod_row
        xs = moe_layer(xs, mods, i, n_rows, n_tiles, row_fn, norm_ffn[i], moe_router_g[i],
                       moe_router_e[i], moe_w1[i], moe_w3[i], moe_w2[i],
                       norm_final if last else None)
    return xs.reshape(bsz, n_lat, d)
```

```python
import functools
import math

import jax
import jax.numpy as jnp
from jax import lax
from jax.experimental import pallas as pl
from jax.experimental.pallas import tpu as pltpu

F32 = jnp.float32
BF16 = jnp.bfloat16

EPS = 1e-6
N_MOD = 6
GRID_W = 64
ROPE_THETA = 10000.0
CONV_WIDTH = 31
SSM_CONV_WIDTH = 5
SSM_HEADS = 32
SSM_HEAD_DIM = 64
SSM_GROUPS = 8
SSM_STATE = 128
SSM_CHUNK = 128
ATT_HEADS = 32
ATT_KV_HEADS = 8
ATT_HEAD_DIM = 128
MOE_GROUPS = 4
MOE_EXPERTS = 8
N_EXPERTS = MOE_GROUPS * MOE_EXPERTS

LANES = 128
SUBLANES = 8
ROW_TILE = 256
HALO = 16
VMEM_LIMIT = 56 * 1024 * 1024


def _cparams(sem):
    return pltpu.CompilerParams(dimension_semantics=sem, vmem_limit_bytes=VMEM_LIMIT)


def _silu(v):
    return v * (1.0 / (1.0 + jnp.exp(-v)))


def _sigmoid(v):
    return 1.0 / (1.0 + jnp.exp(-v))


def _ada_kernel(c_ref, w_ref, b_ref, o_ref):
    s = _silu(c_ref[...]).astype(BF16)
    w = w_ref[...].astype(BF16)
    o_ref[...] = jnp.dot(s, w, preferred_element_type=F32) + b_ref[...]


def ada_mods(cvec, w_ada, b_ada, tn=512):
    depth, d, n = w_ada.shape
    rows = cvec.shape[0]
    return pl.pallas_call(
        _ada_kernel,
        grid=(depth, n // tn),
        in_specs=[
            pl.BlockSpec((rows, d), lambda l, j: (0, 0)),
            pl.BlockSpec((None, d, tn), lambda l, j: (l, 0, j)),
            pl.BlockSpec((None, 1, tn), lambda l, j: (l, 0, j)),
        ],
        out_specs=pl.BlockSpec((None, rows, tn), lambda l, j: (l, 0, j)),
        out_shape=jax.ShapeDtypeStruct((depth, rows, n), F32),
        compiler_params=_cparams(("arbitrary", "arbitrary")),
        name="ada_mods",
    )(cvec, w_ada, b_ada.reshape(depth, 1, n))


class Geom:
    def __init__(self, bsz, lc, n_lat):
        self.bsz, self.lc, self.n_lat = bsz, lc, n_lat
        self.per_b = (lc + n_lat) // ROW_TILE
        self.ctx_t = lc // ROW_TILE
        self.lat_t = n_lat // ROW_TILE
        self.n_all = bsz * self.per_b
        self.n_lat_tiles = bsz * self.lat_t

    def all_block(self, i):
        return i

    def lat_block(self, i):
        return (i // self.lat_t) * self.per_b + self.ctx_t + i % self.lat_t

    def all_mod_row(self, i):
        return jnp.where(i % self.per_b < self.ctx_t, self.bsz, i // self.per_b)

    def lat_mod_row(self, i):
        return i // self.lat_t


def _mod_spec(layer, m, row_fn, n_rows, grid_rank, tile_axis, width=None, col_axis=None):
    def imap(*g):
        r = row_fn(g[tile_axis])
        c = 0 if col_axis is None else g[col_axis]
        return ((layer * n_rows + r) * N_MOD + m, 0, c)
    return imap


def _norm_mod_kernel(x_ref, g_ref, sh_ref, sc_ref, h_ref):
    x = x_ref[...]
    ms = jnp.mean(x * x, axis=-1, keepdims=True)
    xn = x * lax.rsqrt(ms + EPS) * g_ref[...]
    h_ref[...] = (xn * (1.0 + sc_ref[...]) + sh_ref[...]).astype(h_ref.dtype)


def _route(logits):
    lane = lax.broadcasted_iota(jnp.int32, logits.shape, 1)
    neg = jnp.float32(-jnp.inf)
    big = jnp.int32(LANES)
    lg = jnp.where(lane < MOE_GROUPS, logits, neg)
    mg = jnp.max(lg, axis=-1, keepdims=True)
    gsel = jnp.min(jnp.where(lg == mg, lane, big), axis=-1, keepdims=True)
    pg = 1.0 / jnp.sum(jnp.exp(lg - mg), axis=-1, keepdims=True)
    lo = MOE_GROUPS + gsel * MOE_EXPERTS
    le = jnp.where((lane >= lo) & (lane < lo + MOE_EXPERTS), logits, neg)
    v1 = jnp.max(le, axis=-1, keepdims=True)
    i1 = jnp.min(jnp.where(le == v1, lane, big), axis=-1, keepdims=True)
    le2 = jnp.where(lane == i1, neg, le)
    v2 = jnp.max(le2, axis=-1, keepdims=True)
    i2 = jnp.min(jnp.where(le2 == v2, lane, big), axis=-1, keepdims=True)
    e2 = jnp.exp(v2 - v1)
    w1 = pg / (1.0 + e2)
    w2 = pg * e2 / (1.0 + e2)
    id1 = (i1 - MOE_GROUPS).astype(F32)
    id2 = (i2 - MOE_GROUPS).astype(F32)
    return jnp.where(lane == 0, id1, jnp.where(lane == 1, id2,
                     jnp.where(lane == 2, w1, jnp.where(lane == 3, w2, 0.0))))


def _to_row_major(x):
    parts = [x[:, s * LANES:(s + 1) * LANES] for s in range(x.shape[1] // LANES)]
    return pltpu.einshape("stl->tsl", jnp.stack(parts, axis=0))


def _from_row_major(x3, dtype):
    xt = pltpu.einshape("tsl->stl", x3)
    return jnp.concatenate([xt[s].astype(dtype) for s in range(x3.shape[1])], axis=1)


def _norm_mod_route_kernel(x_ref, g_ref, sh_ref, sc_ref, wr_ref, h_ref, r_ref):
    x = x_ref[...]
    ms = jnp.mean(x * x, axis=-1, keepdims=True)
    xn = x * lax.rsqrt(ms + EPS) * g_ref[...]
    h = xn * (1.0 + sc_ref[...]) + sh_ref[...]
    h_ref[...] = h
    logits = jnp.dot(h, wr_ref[...], preferred_element_type=F32,
                     precision=lax.Precision.HIGHEST)
    r_ref[...] = _route(logits)


def norm_mod(x, gain, mods, layer, m_shift, n_rows, n_tiles, block_fn, row_fn,
             router_w=None, out_dtype=BF16):
    d = x.shape[1]
    x_spec = pl.BlockSpec((ROW_TILE, d), lambda i: (block_fn(i), 0))
    g_spec = pl.BlockSpec((1, d), lambda i: (0, 0))
    sh_spec = pl.BlockSpec((None, 1, d), _mod_spec(layer, m_shift, row_fn, n_rows, 1, 0))
    sc_spec = pl.BlockSpec((None, 1, d), _mod_spec(layer, m_shift + 1, row_fn, n_rows, 1, 0))
    o_spec = pl.BlockSpec((ROW_TILE, d), lambda i: (i, 0))
    rows = n_tiles * ROW_TILE
    if router_w is None:
        return pl.pallas_call(
            _norm_mod_kernel,
            grid=(n_tiles,),
            in_specs=[x_spec, g_spec, sh_spec, sc_spec],
            out_specs=o_spec,
            out_shape=jax.ShapeDtypeStruct((rows, d), out_dtype),
            compiler_params=_cparams(("parallel",)),
            name="norm_mod",
        )(x, gain.reshape(1, d), mods, mods)
    return pl.pallas_call(
        _norm_mod_route_kernel,
        grid=(n_tiles,),
        in_specs=[x_spec, g_spec, sh_spec, sc_spec,
                  pl.BlockSpec((d, LANES), lambda i: (0, 0))],
        out_specs=[o_spec, pl.BlockSpec((ROW_TILE, LANES), lambda i: (i, 0))],
        out_shape=[jax.ShapeDtypeStruct((rows, d), F32),
                   jax.ShapeDtypeStruct((rows, LANES), F32)],
        compiler_params=_cparams(("parallel",)),
        name="norm_mod_route",
    )(x, gain.reshape(1, d), mods, mods, router_w)


def _stage_weights(w_refs, wb_refs):
    @pl.when(pl.program_id(1) == 0)
    def _():
        for w_ref, wb_ref in zip(w_refs, wb_refs):
            wb_ref[...] = w_ref[...].astype(BF16)


def _mm_kernel(x_ref, w_ref, o_ref, wb_ref):
    _stage_weights([w_ref], [wb_ref])
    o_ref[...] = jnp.dot(x_ref[...], wb_ref[...], preferred_element_type=F32).astype(o_ref.dtype)


def _row_block(m, want=1024):
    tm = want
    while m % tm:
        tm //= 2
    return tm


def matmul(x, w, col0, n, out_dtype, tn=512):
    m, k = x.shape
    tm = _row_block(m)
    tn = min(tn, n)
    cb = col0 // tn
    return pl.pallas_call(
        _mm_kernel,
        grid=(n // tn, m // tm),
        in_specs=[pl.BlockSpec((tm, k), lambda j, i: (i, 0)),
                  pl.BlockSpec((k, tn), lambda j, i: (0, cb + j))],
        out_specs=pl.BlockSpec((tm, tn), lambda j, i: (i, j)),
        out_shape=jax.ShapeDtypeStruct((m, n), out_dtype),
        scratch_shapes=[pltpu.VMEM((k, tn), BF16)],
        compiler_params=_cparams(("parallel", "arbitrary")),
        name="matmul",
    )(x, w)


def _mm_glu_kernel(x_ref, wa_ref, wg_ref, o_ref, wab_ref, wgb_ref):
    _stage_weights([wa_ref, wg_ref], [wab_ref, wgb_ref])
    x = x_ref[...]
    a = jnp.dot(x, wab_ref[...], preferred_element_type=F32)
    b = jnp.dot(x, wgb_ref[...], preferred_element_type=F32)
    o_ref[...] = (a * _sigmoid(b)).astype(o_ref.dtype)


def matmul_glu(x, w, col_a, col_g, n, out_dtype, tn=256):
    m, k = x.shape
    tm = _row_block(m)
    ca, cg = col_a // tn, col_g // tn
    return pl.pallas_call(
        _mm_glu_kernel,
        grid=(n // tn, m // tm),
        in_specs=[pl.BlockSpec((tm, k), lambda j, i: (i, 0)),
                  pl.BlockSpec((k, tn), lambda j, i: (0, ca + j)),
                  pl.BlockSpec((k, tn), lambda j, i: (0, cg + j))],
        out_specs=pl.BlockSpec((tm, tn), lambda j, i: (i, j)),
        out_shape=jax.ShapeDtypeStruct((m, n), out_dtype),
        scratch_shapes=[pltpu.VMEM((k, tn), BF16), pltpu.VMEM((k, tn), BF16)],
        compiler_params=_cparams(("parallel", "arbitrary")),
        name="matmul_glu",
    )(x, w, w)


def _mm_res_kernel(n_lhs, *refs):
    xs = refs[:n_lhs]
    ws = refs[n_lhs:2 * n_lhs]
    res_ref, gate_ref, o_ref = refs[2 * n_lhs:2 * n_lhs + 3]
    wbs = refs[2 * n_lhs + 3:]
    _stage_weights(ws, wbs)
    acc = jnp.dot(xs[0][...], wbs[0][...], preferred_element_type=F32)
    for x_ref, wb_ref in zip(xs[1:], wbs[1:]):
        acc = acc + jnp.dot(x_ref[...], wb_ref[...], preferred_element_type=F32)
    o_ref[...] = res_ref[...] + gate_ref[...] * acc


def matmul_residual(lhs, w, res, mods, layer, m_gate, n_rows, n_tiles, res_block_fn, row_fn,
                    tn=1024):
    n_lhs = len(lhs)
    n = w.shape[1]
    widths = [a.shape[1] for a in lhs]
    assert all(wd == widths[0] for wd in widths) and sum(widths) == w.shape[0]
    in_specs = [pl.BlockSpec((ROW_TILE, wd), lambda j, i: (i, 0)) for wd in widths]
    in_specs += [pl.BlockSpec((wd, tn), functools.partial(lambda q, j, i: (q, j), q),
                              pipeline_mode=pl.Buffered(1))
                 for q, wd in enumerate(widths)]
    in_specs += [
        pl.BlockSpec((ROW_TILE, tn), lambda j, i: (res_block_fn(i), j)),
        pl.BlockSpec((None, 1, tn), _mod_spec(layer, m_gate, row_fn, n_rows, 2, 1, col_axis=0)),
    ]
    return pl.pallas_call(
        functools.partial(_mm_res_kernel, n_lhs),
        grid=(n // tn, n_tiles),
        in_specs=in_specs,
        out_specs=pl.BlockSpec((ROW_TILE, tn), lambda j, i: (i, j)),
        out_shape=jax.ShapeDtypeStruct((n_tiles * ROW_TILE, n), F32),
        scratch_shapes=[pltpu.VMEM((wd, tn), BF16) for wd in widths],
        compiler_params=_cparams(("parallel", "arbitrary")),
        name="matmul_residual",
    )(*lhs, *([w] * n_lhs), res, mods)


def _conv_kernel(width, mode, per_b, ctx_t, x_ref, prev_ref, next_ref, w_ref, p0_ref, p1_ref,
                 o_ref, xp_ref, acc_ref, sh_ref):
    t = pl.program_id(0) % per_b
    first = (t == 0) | (t == ctx_t)
    last = (t == ctx_t - 1) | (t == per_b - 1)
    c = x_ref.shape[1]
    xp_ref[0:HALO, :] = jnp.where(first, 0.0, prev_ref[...])
    xp_ref[HALO:HALO + ROW_TILE, :] = x_ref[...]
    xp_ref[HALO + ROW_TILE:, :] = jnp.where(last, 0.0, next_ref[...])
    off = HALO - width // 2
    sub = SUBLANES
    taps_of = [[k for k in range(width) if (off + k) % sub == ph] for ph in range(sub)]
    span = ROW_TILE + sub * ((off + width - 1) // sub)

    def chunk(j, carry):
        cs = pl.ds(pl.multiple_of(j * LANES, LANES), LANES)
        for ph in range(1, sub):
            if taps_of[ph]:
                sh_ref[ph, 0:span, :] = xp_ref[ph:ph + span, cs]
        acc = jnp.zeros((ROW_TILE, LANES), F32)
        for ph in range(sub):
            for k in taps_of[ph]:
                lo = off + k - ph
                if ph == 0:
                    win = xp_ref[lo:lo + ROW_TILE, cs]
                else:
                    win = sh_ref[ph, lo:lo + ROW_TILE, :]
                acc = acc + win * w_ref[k:k + 1, cs]
        acc_ref[:, cs] = acc
        return carry

    lax.fori_loop(0, c // LANES, chunk, 0)
    y = acc_ref[...]
    if mode == "layernorm_silu":
        mu = jnp.mean(y, axis=-1, keepdims=True)
        yc = y - mu
        var = jnp.mean(yc * yc, axis=-1, keepdims=True)
        y = yc * lax.rsqrt(var + EPS) * p0_ref[...] + p1_ref[...]
    else:
        y = y + p0_ref[...]
    o_ref[...] = _silu(y).astype(o_ref.dtype)


def seq_conv(x, w, p0, p1, mode, geom, out_dtype=BF16):
    rows, c = x.shape
    width = w.shape[0]
    hb = ROW_TILE // HALO
    n_halo = rows // HALO
    kern = functools.partial(_conv_kernel, width, mode, geom.per_b, geom.ctx_t)
    return pl.pallas_call(
        kern,
        grid=(geom.n_all,),
        in_specs=[
            pl.BlockSpec((ROW_TILE, c), lambda i: (i, 0)),
            pl.BlockSpec((HALO, c), lambda i: (jnp.maximum(i * hb - 1, 0), 0)),
            pl.BlockSpec((HALO, c), lambda i: (jnp.minimum((i + 1) * hb, n_halo - 1), 0)),
            pl.BlockSpec((width, c), lambda i: (0, 0)),
            pl.BlockSpec((1, c), lambda i: (0, 0)),
            pl.BlockSpec((1, c), lambda i: (0, 0)),
        ],
        out_specs=pl.BlockSpec((ROW_TILE, c), lambda i: (i, 0)),
        out_shape=jax.ShapeDtypeStruct((rows, c), out_dtype),
        scratch_shapes=[pltpu.VMEM((ROW_TILE + 2 * HALO, c), F32),
                        pltpu.VMEM((ROW_TILE, c), F32),
                        pltpu.VMEM((SUBLANES, ROW_TILE + 2 * HALO, LANES), F32)],
        compiler_params=_cparams(("parallel",)),
        name="seq_conv_" + mode,
    )(x, x, x, w, p0.reshape(1, c), p1.reshape(1, c))


def _ssd_kernel(n_chunks, ctx_chunks, xs_ref, b_ref, c_ref, dtraw_ref, bias_ref, aneg_ref,
                y_ref, state_ref, tr_ref):
    d = pl.program_id(1)
    T = SSM_CHUNK
    rows = lax.broadcasted_iota(jnp.int32, (T, T), 0)
    cols = lax.broadcasted_iota(jnp.int32, (T, T), 1)
    fwd = d == 0
    diff = (rows - cols) * jnp.where(fwd, 1, -1)
    tri = jnp.where(diff <= 0, 1.0, 0.0).astype(F32)
    keep = diff >= 0
    lane = lax.broadcasted_iota(jnp.int32, (T, T), 1)
    left = lane < SSM_HEAD_DIM
    state_ref[...] = jnp.zeros_like(state_ref)
    tr_ref[...] = jnp.zeros_like(tr_ref)
    bias = bias_ref[...]
    aneg = aneg_ref[...]
    n_heads = 4

    def step(s, carry):
        bwd_c = jnp.where(s < ctx_chunks, ctx_chunks - 1 - s, n_chunks - 1 + ctx_chunks - s)
        c = jnp.where(fwd, s, bwd_c)
        r0 = pl.multiple_of(c * T, T)
        raw = dtraw_ref[:, pl.ds(r0, T)]
        z = raw + bias
        dt_row = jnp.maximum(z, 0.0) + jnp.log1p(jnp.exp(-jnp.abs(z)))
        da_row = dt_row * aneg
        cum_row = jnp.dot(da_row, tri, preferred_element_type=F32,
                          precision=lax.Precision.HIGHEST)
        tot_row = jnp.sum(da_row, axis=-1, keepdims=True)
        tr_ref[0:8, :] = cum_row
        colform = tr_ref[...].T
        bmat = b_ref[pl.ds(r0, T), :]
        cmat = c_ref[pl.ds(r0, T), :]
        cb = lax.dot_general(cmat, bmat, (((1,), (1,)), ((), ())),
                             preferred_element_type=F32)
        bt = bmat.astype(F32).T
        for pair in range(n_heads // 2):
            xs_pair = xs_ref[pl.ds(r0, T), pair * LANES:(pair + 1) * LANES]
            ms, bds, ecs, ets = [], [], [], []
            for q in range(2):
                r = pair * 2 + q
                cum_c = jnp.broadcast_to(colform[:, r:r + 1], (T, T))
                cum_r = cum_row[r:r + 1, :]
                dt_r = dt_row[r:r + 1, :]
                seg = jnp.where(keep, cum_c - cum_r, -jnp.inf)
                ms.append((cb * jnp.exp(seg) * dt_r).astype(BF16))
                dec_r = jnp.exp(tot_row[r:r + 1, :] - cum_r) * dt_r
                bds.append((bt * dec_r).astype(BF16))
                ecs.append(jnp.exp(cum_c))
                ets.append(jnp.exp(tot_row[r:r + 1, :]))
            xs32 = xs_pair.astype(F32)
            xd2 = jnp.concatenate([jnp.where(left, xs32, 0.0).astype(BF16),
                                   jnp.where(left, 0.0, xs32).astype(BF16)], axis=0)
            m2 = jnp.concatenate(ms, axis=1)
            y_diag = jnp.dot(m2, xd2, preferred_element_type=F32)
            st = state_ref[pair]
            y_off = jnp.dot(cmat, st.astype(BF16), preferred_element_type=F32)
            y_off = y_off * jnp.where(left, ecs[0], ecs[1])
            y_ref[pl.ds(r0, T), pair * LANES:(pair + 1) * LANES] = (y_diag + y_off).astype(y_ref.dtype)
            bd2 = jnp.concatenate(bds, axis=1)
            upd = jnp.dot(bd2, xd2, preferred_element_type=F32)
            state_ref[pair] = st * jnp.where(left[0:1, :], ets[0], ets[1]) + upd
        return carry

    lax.fori_loop(0, n_chunks, step, 0, unroll=2)


def ssd_scan(xbc, dtraw_rows, bias_rows, aneg_rows, geom):
    rows = xbc.shape[0]
    per = geom.lc + geom.n_lat
    n_chunks = per // SSM_CHUNK
    ctx_chunks = geom.lc // SSM_CHUNK
    ssm_d = SSM_HEADS * SSM_HEAD_DIM
    gw = 4 * SSM_HEAD_DIM
    xs_blocks = ssm_d // gw
    kern = functools.partial(_ssd_kernel, n_chunks, ctx_chunks)
    return pl.pallas_call(
        kern,
        grid=(geom.bsz, 2, SSM_GROUPS),
        in_specs=[
            pl.BlockSpec((per, gw), lambda b, d, g: (b, g)),
            pl.BlockSpec((per, SSM_STATE), lambda b, d, g: (b, 2 * xs_blocks + g)),
            pl.BlockSpec((per, SSM_STATE), lambda b, d, g: (b, 2 * xs_blocks + SSM_GROUPS + g)),
            pl.BlockSpec((None, None, None, 8, per), lambda b, d, g: (b, d, g, 0, 0)),
            pl.BlockSpec((None, None, 8, LANES), lambda b, d, g: (d, g, 0, 0)),
            pl.BlockSpec((None, None, 8, LANES), lambda b, d, g: (d, g, 0, 0)),
        ],
        out_specs=pl.BlockSpec((None, per, gw), lambda b, d, g: (d, b, g)),
        out_shape=jax.ShapeDtypeStruct((2, rows, ssm_d), BF16),
        scratch_shapes=[pltpu.VMEM((2, SSM_STATE, LANES), F32),
                        pltpu.VMEM((LANES, SSM_CHUNK), F32)],
        compiler_params=_cparams(("parallel", "parallel", "parallel")),
        name="ssd_scan",
    )(xbc, xbc, xbc, dtraw_rows, bias_rows, aneg_rows)


def _ssd_merge_kernel(yf_ref, yb_ref, xs_ref, z_ref, d_ref, g_ref, o_ref):
    y = d_ref[...] * xs_ref[...].astype(F32) + yf_ref[...].astype(F32) + yb_ref[...].astype(F32)
    gated = y * _silu(z_ref[...].astype(F32))
    ms = jnp.mean(gated * gated, axis=-1, keepdims=True)
    o_ref[...] = (gated * lax.rsqrt(ms + EPS) * g_ref[...]).astype(o_ref.dtype)


def ssd_merge(y2, xbc, z, d_full, g, geom):
    rows, ssm_d = z.shape
    spec = pl.BlockSpec((ROW_TILE, ssm_d), lambda i: (i, 0))
    vec = pl.BlockSpec((1, ssm_d), lambda i: (0, 0))
    return pl.pallas_call(
        _ssd_merge_kernel,
        grid=(geom.n_all,),
        in_specs=[pl.BlockSpec((None, ROW_TILE, ssm_d), lambda i: (0, i, 0)),
                  pl.BlockSpec((None, ROW_TILE, ssm_d), lambda i: (1, i, 0)),
                  spec, spec, vec, vec],
        out_specs=spec,
        out_shape=jax.ShapeDtypeStruct((rows, ssm_d), BF16),
        compiler_params=_cparams(("parallel",)),
        name="ssd_merge",
    )(y2, y2, xbc, z, d_full.reshape(1, ssm_d), g.reshape(1, ssm_d))


def _qk_prep_kernel(n_q, n_k, scale, qkv_ref, cos_ref, sin_ref, gq_ref, gk_ref, q_ref, k_ref):
    cosf = cos_ref[...]
    sinf = sin_ref[...]
    hd = ATT_HEAD_DIM
    for h in range(n_q + n_k):
        x = qkv_ref[:, h * hd:(h + 1) * hd].astype(F32)
        g = gq_ref[...] if h < n_q else gk_ref[...]
        ms = jnp.mean(x * x, axis=-1, keepdims=True)
        xn = x * lax.rsqrt(ms + EPS) * g
        rot = pltpu.roll(xn, hd // 2, axis=1)
        o = xn * cosf + rot * sinf
        if h < n_q:
            q_ref[:, h * hd:(h + 1) * hd] = (o * scale).astype(q_ref.dtype)
        else:
            k_ref[:, (h - n_q) * hd:(h - n_q + 1) * hd] = o.astype(k_ref.dtype)


def qk_prep(qkv, cosf, sinf, gq, gk, geom):
    rows = qkv.shape[0]
    hd = ATT_HEAD_DIM
    qc, kc = ATT_HEADS * hd, ATT_KV_HEADS * hd
    kern = functools.partial(_qk_prep_kernel, ATT_HEADS, ATT_KV_HEADS, hd ** -0.5)
    per_b = geom.per_b
    return pl.pallas_call(
        kern,
        grid=(geom.n_all,),
        in_specs=[pl.BlockSpec((ROW_TILE, qc + kc), lambda i: (i, 0)),
                  pl.BlockSpec((ROW_TILE, hd), lambda i: (i % per_b, 0)),
                  pl.BlockSpec((ROW_TILE, hd), lambda i: (i % per_b, 0)),
                  pl.BlockSpec((1, hd), lambda i: (0, 0)),
                  pl.BlockSpec((1, hd), lambda i: (0, 0))],
        out_specs=[pl.BlockSpec((ROW_TILE, qc), lambda i: (i, 0)),
                   pl.BlockSpec((ROW_TILE, kc), lambda i: (i, 0))],
        out_shape=[jax.ShapeDtypeStruct((rows, qc), BF16),
                   jax.ShapeDtypeStruct((rows, kc), BF16)],
        compiler_params=_cparams(("parallel",)),
        name="qk_prep",
    )(qkv, cosf, sinf, gq.reshape(1, hd), gk.reshape(1, hd))


def _attn_kernel(rep, q_ref, k_ref, v_ref, o_ref):
    hd = ATT_HEAD_DIM
    k = k_ref[...]
    v = v_ref[...]
    v1 = jnp.concatenate([v, jnp.ones_like(v)], axis=1)
    for r in range(rep):
        q = q_ref[:, r * hd:(r + 1) * hd]
        s = lax.dot_general(q, k, (((1,), (1,)), ((), ())), preferred_element_type=F32)
        m = jnp.max(s, axis=-1, keepdims=True)
        p = jnp.exp((s - m).astype(BF16))
        o = jnp.dot(p, v1, preferred_element_type=F32)
        o_ref[:, r * hd:(r + 1) * hd] = (o[:, :hd] / o[:, hd:hd + 1]).astype(o_ref.dtype)


def attention(q, k, qkv, geom):
    hd = ATT_HEAD_DIM
    rep = ATT_HEADS // ATT_KV_HEADS
    per = geom.lc + geom.n_lat
    v_col0 = (ATT_HEADS + ATT_KV_HEADS)
    lat_t, per_b, ctx_t = geom.lat_t, geom.per_b, geom.ctx_t
    return pl.pallas_call(
        functools.partial(_attn_kernel, rep),
        grid=(geom.bsz, ATT_KV_HEADS, lat_t),
        in_specs=[pl.BlockSpec((ROW_TILE, rep * hd), lambda b, g, t: (b * per_b + ctx_t + t, g)),
                  pl.BlockSpec((per, hd), lambda b, g, t: (b, g)),
                  pl.BlockSpec((per, hd), lambda b, g, t: (b, v_col0 + g))],
        out_specs=pl.BlockSpec((ROW_TILE, rep * hd), lambda b, g, t: (b * lat_t + t, g)),
        out_shape=jax.ShapeDtypeStruct((geom.bsz * geom.n_lat, ATT_HEADS * hd), BF16),
        compiler_params=_cparams(("parallel", "parallel", "parallel")),
        name="attention",
    )(q, k, qkv)


def _start_row_gather(ids_ref, first, stride, src_ref, dst_ref, sem):
    def issue(r, carry):
        row = ids_ref[first + stride * r]
        pltpu.make_async_copy(src_ref.at[pl.ds(row, 1)], dst_ref.at[pl.ds(r, 1)], sem).start()
        return carry

    lax.fori_loop(0, ROW_TILE, issue, 0, unroll=8)


def _wait_row_gather(src_ref, dst_ref, sem):
    pltpu.make_async_copy(src_ref.at[pl.ds(0, ROW_TILE)], dst_ref, sem).wait()


def _cast_kernel(x_ref, o_ref):
    o_ref[...] = x_ref[...].astype(o_ref.dtype)


def cast_experts(w_all, layer, dtype=BF16):
    depth, g, e, k, n = w_all.shape
    ne = g * e
    return pl.pallas_call(
        _cast_kernel,
        grid=(ne,),
        in_specs=[pl.BlockSpec((None, k, n), lambda i: (layer * ne + i, 0, 0))],
        out_specs=pl.BlockSpec((None, k, n), lambda i: (i, 0, 0)),
        out_shape=jax.ShapeDtypeStruct((ne, k, n), dtype),
        compiler_params=_cparams(("parallel",)),
        name="cast_experts",
    )(w_all.reshape(depth * ne, k, n))


def _expert_mlp_kernel(te_ref, nt_ref, ids_ref, h_ref, cw_ref, w1_ref, w3_ref, w2_ref, o_ref,
                       xbuf_ref, sem):
    t = pl.program_id(0)
    nt = nt_ref[0]
    slot = t % 2

    @pl.when(t == 0)
    def _():
        _start_row_gather(ids_ref, 0, 1, h_ref, xbuf_ref.at[0], sem.at[0])

    @pl.when(t + 1 < nt)
    def _():
        _start_row_gather(ids_ref, (t + 1) * ROW_TILE, 1, h_ref, xbuf_ref.at[1 - slot],
                          sem.at[1 - slot])

    @pl.when(t < nt)
    def _():
        _wait_row_gather(h_ref, xbuf_ref.at[slot], sem.at[slot])
        x = xbuf_ref[slot].astype(BF16)
        a = jnp.dot(x, w1_ref[...], preferred_element_type=F32)
        b = jnp.dot(x, w3_ref[...], preferred_element_type=F32)
        hid = _silu(a) * b * cw_ref[...]
        y = jnp.dot(hid.astype(BF16), w2_ref[...], preferred_element_type=F32)
        o_ref[...] = _to_row_major(y)

    @pl.when(t >= nt)
    def _():
        o_ref[...] = jnp.zeros_like(o_ref)


def expert_mlp(h, row_ids, cw, w1, w3, w2, tile_expert, n_tiles_used):
    p = row_ids.shape[0]
    d = h.shape[1]
    s, l = d // LANES, LANES
    hdim = w1.shape[2]
    n_tiles = p // ROW_TILE

    def wmap(t, te, nt, ids):
        return (te[t], 0, 0)

    return pl.pallas_call(
        _expert_mlp_kernel,
        grid_spec=pltpu.PrefetchScalarGridSpec(
            num_scalar_prefetch=3,
            grid=(n_tiles,),
            in_specs=[pl.BlockSpec(memory_space=pl.ANY),
                      pl.BlockSpec((ROW_TILE, 1), lambda t, te, nt, ids: (t, 0)),
                      pl.BlockSpec((None, d, hdim), wmap),
                      pl.BlockSpec((None, d, hdim), wmap),
                      pl.BlockSpec((None, hdim, d), wmap)],
            out_specs=pl.BlockSpec((ROW_TILE, s, l), lambda t, te, nt, ids: (t, 0, 0)),
            scratch_shapes=[pltpu.VMEM((2, ROW_TILE, d), F32), pltpu.SemaphoreType.DMA((2,))],
        ),
        out_shape=jax.ShapeDtypeStruct((p, s, l), F32),
        compiler_params=_cparams(("arbitrary",)),
        name="expert_mlp",
    )(tile_expert, n_tiles_used, row_ids, h, cw, w1, w3, w2)


def _moe_combine_kernel(final_norm, pos_ref, ys_ref, res_ref, gate_ref, g_ref, o_ref, buf_ref, sem):
    i = pl.program_id(0)
    n = pl.num_programs(0)
    slot = i % 2

    def start(tile, s):
        for q in range(2):
            _start_row_gather(pos_ref, tile * (2 * ROW_TILE) + q, 2, ys_ref, buf_ref.at[s, q],
                              sem.at[s, q])

    @pl.when(i == 0)
    def _():
        start(0, 0)

    @pl.when(i + 1 < n)
    def _():
        start(i + 1, 1 - slot)

    for q in range(2):
        _wait_row_gather(ys_ref, buf_ref.at[slot, q], sem.at[slot, q])
    y = _from_row_major(buf_ref[slot, 0] + buf_ref[slot, 1], F32)
    x = res_ref[...] + gate_ref[...] * y
    if final_norm:
        ms = jnp.mean(x * x, axis=-1, keepdims=True)
        x = x * lax.rsqrt(ms + EPS) * g_ref[...]
    o_ref[...] = x


def moe_combine(ys, pos, res, mods, layer, n_rows, n_tiles, row_fn, final_gain=None):
    d = res.shape[1]
    _, s, l = ys.shape
    tile = pl.BlockSpec((ROW_TILE, d), lambda i, pos: (i, 0))
    mod_map = _mod_spec(layer, 5, row_fn, n_rows, 1, 0)
    gain = jnp.ones((d,), F32) if final_gain is None else final_gain
    return pl.pallas_call(
        functools.partial(_moe_combine_kernel, final_gain is not None),
        grid_spec=pltpu.PrefetchScalarGridSpec(
            num_scalar_prefetch=1,
            grid=(n_tiles,),
            in_specs=[pl.BlockSpec(memory_space=pl.ANY),
                      tile,
                      pl.BlockSpec((None, 1, d), lambda i, pos: mod_map(i)),
                      pl.BlockSpec((1, d), lambda i, pos: (0, 0))],
            out_specs=tile,
            scratch_shapes=[pltpu.VMEM((2, 2, ROW_TILE, s, l), F32),
                            pltpu.SemaphoreType.DMA((2, 2))],
        ),
        out_shape=jax.ShapeDtypeStruct((n_tiles * ROW_TILE, d), F32),
        compiler_params=_cparams(("arbitrary",)),
        name="moe_combine",
    )(pos.reshape(-1), ys, res, mods, gain.reshape(1, d))


def _dispatch_plan(route):
    t = route.shape[0]
    ids = route[:, 0:2].astype(jnp.int32)
    wts = route[:, 2:4]
    flat = ids.reshape(-1)
    n_assign = flat.shape[0]
    p_max = n_assign + N_EXPERTS * ROW_TILE
    p_max = (p_max // ROW_TILE) * ROW_TILE
    onehot = (flat[:, None] == jnp.arange(N_EXPERTS, dtype=jnp.int32)[None, :]).astype(jnp.int32)
    counts = jnp.sum(onehot, axis=0)
    padded = ((counts + ROW_TILE - 1) // ROW_TILE) * ROW_TILE
    ends = jnp.cumsum(padded)
    offs = ends - padded
    starts = jnp.cumsum(counts) - counts
    order = jnp.argsort(flat, stable=True).astype(jnp.int32)
    slot = jnp.arange(p_max, dtype=jnp.int32)
    tile_start = jnp.arange(p_max // ROW_TILE, dtype=jnp.int32) * ROW_TILE
    tile_e = jnp.sum((tile_start[:, None] >= ends[None, :]).astype(jnp.int32), axis=1)
    slot_e = jnp.repeat(jnp.minimum(tile_e, N_EXPERTS - 1), ROW_TILE)
    rank = slot - offs[slot_e]
    valid = (rank < counts[slot_e]) & (slot < ends[-1])
    src_sorted = jnp.clip(starts[slot_e] + rank, 0, n_assign - 1)
    assign = order[src_sorted]
    row_ids = jnp.where(valid, assign // 2, 0).astype(jnp.int32)
    cw = jnp.where(valid, wts.reshape(-1)[assign], 0.0).astype(F32)
    sorted_pos = jnp.argsort(order).astype(jnp.int32)
    pos = (sorted_pos + (offs - starts)[flat]).astype(jnp.int32).reshape(t, 2)
    tile_expert = slot_e[::ROW_TILE]
    n_tiles_used = (ends[-1] // ROW_TILE).astype(jnp.int32).reshape(1)
    last_e = tile_expert[jnp.maximum(n_tiles_used[0] - 1, 0)]
    tile_idx = jnp.arange(p_max // ROW_TILE, dtype=jnp.int32)
    tile_expert = jnp.where(tile_idx < n_tiles_used[0], tile_expert, last_e).astype(jnp.int32)
    return row_ids, cw, pos, tile_expert, n_tiles_used


def hier_moe_block(h, route, w1, w3, w2):
    row_ids, cw, pos, tile_expert, n_tiles_used = _dispatch_plan(route)
    ys = expert_mlp(h, row_ids, cw[:, None], w1, w3, w2, tile_expert, n_tiles_used)
    return ys, pos


def _rope_tables(geom):
    n_lat = geom.n_lat
    half = ATT_HEAD_DIM // 2
    nfreq = half // 2
    row = jnp.repeat(jnp.arange(n_lat // GRID_W, dtype=F32), GRID_W)
    col = (jnp.arange(n_lat) % GRID_W).astype(F32)
    inv = ROPE_THETA ** (-jnp.arange(nfreq, dtype=F32) / nfreq)
    ang = jnp.concatenate([row[:, None] * inv, col[:, None] * inv], axis=-1)
    cos, sin = jnp.cos(ang), jnp.sin(ang)
    cosf = jnp.concatenate([cos, cos], axis=-1)
    sinf = jnp.concatenate([-sin, sin], axis=-1)
    cosf = jnp.concatenate([jnp.ones((geom.lc, ATT_HEAD_DIM), F32), cosf], axis=0)
    sinf = jnp.concatenate([jnp.zeros((geom.lc, ATT_HEAD_DIM), F32), sinf], axis=0)
    return cosf, sinf


def _router_weights(router_g, router_e):
    d = router_g.shape[0]
    pad = jnp.zeros((d, LANES - MOE_GROUPS - N_EXPERTS), F32)
    return jnp.concatenate([router_g, router_e, pad], axis=1)


def mixer_even(xs, mods, layer, j, n_rows, geom, norm_g, w_in, conv_dw, conv_ln_g, conv_ln_b,
               ssm_conv_w, ssm_conv_b, ssm_dt_bias, ssm_a_log, ssm_d, ssm_norm_g, w_out,
               debug=False):
    bsz, per = geom.bsz, geom.lc + geom.n_lat
    conv_d = conv_dw.shape[1]
    ssm_dd = SSM_HEADS * SSM_HEAD_DIM
    xbc_dim = ssm_conv_w.shape[1]
    h = norm_mod(xs, norm_g, mods, layer, 0, n_rows, geom.n_all, geom.all_block, geom.all_mod_row)
    c0, c1, c2, c3 = conv_d, 2 * conv_d, 2 * conv_d + ssm_dd, 2 * conv_d + ssm_dd + xbc_dim
    glu = matmul_glu(h, w_in, 0, c0, conv_d, F32)
    z = matmul(h, w_in, c1, ssm_dd, BF16)
    xbc_raw = matmul(h, w_in, c2, xbc_dim, F32)
    w_dt = jnp.pad(w_in[:, c3:], ((0, 0), (0, LANES - 2 * SSM_HEADS)))
    dt_raw = matmul(h, w_dt, 0, LANES, F32)[:, :2 * SSM_HEADS]
    conf = seq_conv(glu, conv_dw, conv_ln_g, conv_ln_b, "layernorm_silu", geom)
    xbc = seq_conv(xbc_raw, ssm_conv_w, ssm_conv_b, ssm_conv_b, "bias_silu", geom)
    dtr = dt_raw.reshape(bsz, per, 2, SSM_GROUPS, 4).transpose(0, 2, 3, 4, 1)
    dtr = jnp.pad(dtr, ((0, 0), (0, 0), (0, 0), (0, 4), (0, 0)))

    def head_rows(v):
        v = v.astype(F32).reshape(2, SSM_GROUPS, 4)
        v = jnp.pad(v, ((0, 0), (0, 0), (0, 4)))
        return jnp.broadcast_to(v[..., None], (2, SSM_GROUPS, 8, LANES))

    y2 = ssd_scan(xbc, dtr, head_rows(ssm_dt_bias), head_rows(-jnp.exp(ssm_a_log.astype(F32))), geom)
    yn = ssd_merge(y2, xbc, z, jnp.repeat(ssm_d, SSM_HEAD_DIM), ssm_norm_g, geom)
    out = matmul_residual([conf, yn], w_out, xs, mods, layer, 2, n_rows, geom.n_all,
                          geom.all_block, geom.all_mod_row)
    if debug:
        return out, dict(h=h, glu=glu, z=z, xbc_raw=xbc_raw, dt_raw=dt_raw, conf=conf, xbc=xbc,
                         y2=y2, yn=yn)
    return out


def mixer_odd(xs, mods, layer, n_rows, geom, norm_g, w_qkv, q_norm_g, k_norm_g, w_o):
    h = norm_mod(xs, norm_g, mods, layer, 0, n_rows, geom.n_all, geom.all_block, geom.all_mod_row)
    qkv = matmul(h, w_qkv, 0, w_qkv.shape[1], BF16)
    cosf, sinf = _rope_tables(geom)
    q, k = qk_prep(qkv, cosf, sinf, q_norm_g, k_norm_g, geom)
    att = attention(q, k, qkv, geom)
    return matmul_residual([att], w_o, xs, mods, layer, 2, n_rows, geom.n_lat_tiles,
                           geom.lat_block, geom.lat_mod_row)


def moe_layer(xs, mods, layer, n_rows, n_tiles, row_fn, norm_g, router_g, router_e, w1, w3, w2,
              final_gain):
    hm, route = norm_mod(xs, norm_g, mods, layer, 3, n_rows, n_tiles, lambda t: t, row_fn,
                         router_w=_router_weights(router_g, router_e))
    w1 = cast_experts(w1, layer)
    w3 = cast_experts(w3, layer)
    w2 = cast_experts(w2, layer)
    ys, pos = hier_moe_block(hm, route, w1, w3, w2)
    return moe_combine(ys, pos, xs, mods, layer, n_rows, n_tiles, row_fn, final_gain=final_gain)


def kernel(x, c, ctx, c_ctx, w_ada, b_ada, norm_mix, norm_ffn, w_in0, conv_dw, conv_ln_g, conv_ln_b, ssm_conv_w, ssm_conv_b, ssm_dt_bias, ssm_a_log, ssm_d, ssm_norm_g, w_out0, w_qkv, q_norm_g, k_norm_g, w_o, moe_router_g, moe_router_e, moe_w1, moe_w3, moe_w2, norm_final):
    bsz, n_lat, d = x.shape
    lc = ctx.shape[1]
    depth = w_ada.shape[0]
    geom = Geom(bsz, lc, n_lat)
    n_rows = 16

    cvec = jnp.concatenate([c, c_ctx[None, :], jnp.zeros((n_rows - bsz - 1, d), F32)], axis=0)
    mods = ada_mods(cvec, w_ada, b_ada).reshape(depth * n_rows * N_MOD, 1, d)
    xs = jnp.concatenate([ctx, x], axis=1).reshape(bsz * (lc + n_lat), d)

    for i in range(depth):
        last = i == depth - 1
        j = i // 2
        if i % 2 == 0:
            if last:
                raise NotImplementedError("a conv/SSD mixer in the last layer is not supported")
            xs = mixer_even(xs, mods, i, j, n_rows, geom, norm_mix[i], w_in0[j], conv_dw[j],
                            conv_ln_g[j], conv_ln_b[j], ssm_conv_w[j], ssm_conv_b[j],
                            ssm_dt_bias[j], ssm_a_log[j], ssm_d[j], ssm_norm_g[j], w_out0[j])
            n_tiles, row_fn = geom.n_all, geom.all_mod_row
        else:
            if not last:
                raise NotImplementedError("an attention mixer before the last layer is not supported")
            xs = mixer_odd(xs, mods, i, n_rows, geom, norm_mix[i], w_qkv[j], q_norm_g[j],
                           k_norm_g[j], w_o[j])
            n_tiles, row_fn = geom.n_lat_tiles, geom.lat_mod_row
        xs = moe_layer(xs, mods, i, n_rows, n_tiles, row_fn, norm_ffn[i], moe_router_g[i],
                       moe_router_e[i], moe_w1, moe_w3, moe_w2, norm_final if last else None)
    return xs.reshape(bsz, n_lat, d)
```

```python
import functools
import math

import jax
import jax.numpy as jnp
from jax import lax
from jax.experimental import pallas as pl
from jax.experimental.pallas import tpu as pltpu

F32 = jnp.float32
BF16 = jnp.bfloat16

EPS = 1e-6
N_MOD = 6
GRID_W = 64
ROPE_THETA = 10000.0
CONV_WIDTH = 31
SSM_CONV_WIDTH = 5
SSM_HEADS = 32
SSM_HEAD_DIM = 64
SSM_GROUPS = 8
SSM_STATE = 128
SSM_CHUNK = 128
ATT_HEADS = 32
ATT_KV_HEADS = 8
ATT_HEAD_DIM = 128
MOE_GROUPS = 4
MOE_EXPERTS = 8
N_EXPERTS = MOE_GROUPS * MOE_EXPERTS

LANES = 128
SUBLANES = 8
ROW_TILE = 256
HALO = 16
VMEM_LIMIT = 56 * 1024 * 1024


def _cparams(sem):
    return pltpu.CompilerParams(dimension_semantics=sem, vmem_limit_bytes=VMEM_LIMIT)


def _silu(v):
    return v * (1.0 / (1.0 + jnp.exp(-v)))


def _sigmoid(v):
    return 1.0 / (1.0 + jnp.exp(-v))


def _ada_kernel(c_ref, w_ref, b_ref, o_ref):
    s = _silu(c_ref[...]).astype(BF16)
    w = w_ref[...].astype(BF16)
    o_ref[...] = jnp.dot(s, w, preferred_element_type=F32) + b_ref[...]


def ada_mods(cvec, w_ada, b_ada, tn=512):
    depth, d, n = w_ada.shape
    rows = cvec.shape[0]
    return pl.pallas_call(
        _ada_kernel,
        grid=(depth, n // tn),
        in_specs=[
            pl.BlockSpec((rows, d), lambda l, j: (0, 0)),
            pl.BlockSpec((None, d, tn), lambda l, j: (l, 0, j)),
            pl.BlockSpec((None, 1, tn), lambda l, j: (l, 0, j)),
        ],
        out_specs=pl.BlockSpec((None, rows, tn), lambda l, j: (l, 0, j)),
        out_shape=jax.ShapeDtypeStruct((depth, rows, n), F32),
        compiler_params=_cparams(("arbitrary", "arbitrary")),
        name="ada_mods",
    )(cvec, w_ada, b_ada.reshape(depth, 1, n))


class Geom:
    def __init__(self, bsz, lc, n_lat):
        self.bsz, self.lc, self.n_lat = bsz, lc, n_lat
        self.per_b = (lc + n_lat) // ROW_TILE
        self.ctx_t = lc // ROW_TILE
        self.lat_t = n_lat // ROW_TILE
        self.n_all = bsz * self.per_b
        self.n_lat_tiles = bsz * self.lat_t

    def all_block(self, i):
        return i

    def lat_block(self, i):
        return (i // self.lat_t) * self.per_b + self.ctx_t + i % self.lat_t

    def all_mod_row(self, i):
        return jnp.where(i % self.per_b < self.ctx_t, self.bsz, i // self.per_b)

    def lat_mod_row(self, i):
        return i // self.lat_t

    def ctx_src_block(self, i):
        return (i // self.per_b) * self.ctx_t + jnp.minimum(i % self.per_b, self.ctx_t - 1)

    def lat_src_block(self, i):
        return (i // self.per_b) * self.lat_t + jnp.clip(i % self.per_b - self.ctx_t, 0,
                                                         self.lat_t - 1)

    def tile_is_ctx(self, i):
        return i % self.per_b < self.ctx_t


def _mod_spec(layer, m, row_fn, n_rows, grid_rank, tile_axis, width=None, col_axis=None):
    def imap(*g):
        r = row_fn(g[tile_axis])
        c = 0 if col_axis is None else g[col_axis]
        return ((layer * n_rows + r) * N_MOD + m, 0, c)
    return imap


def _norm_mod_kernel(geom, *refs):
    if geom is None:
        x_ref, g_ref, sh_ref, sc_ref, h_ref = refs
        x = x_ref[...]
    else:
        xc_ref, xl_ref, g_ref, sh_ref, sc_ref, h_ref = refs
        x = jnp.where(geom.tile_is_ctx(pl.program_id(0)), xc_ref[...], xl_ref[...])
    ms = jnp.mean(x * x, axis=-1, keepdims=True)
    xn = x * lax.rsqrt(ms + EPS) * g_ref[...]
    h_ref[...] = (xn * (1.0 + sc_ref[...]) + sh_ref[...]).astype(h_ref.dtype)


def _route(logits):
    lane = lax.broadcasted_iota(jnp.int32, logits.shape, 1)
    neg = jnp.float32(-jnp.inf)
    big = jnp.int32(LANES)
    lg = jnp.where(lane < MOE_GROUPS, logits, neg)
    mg = jnp.max(lg, axis=-1, keepdims=True)
    gsel = jnp.min(jnp.where(lg == mg, lane, big), axis=-1, keepdims=True)
    pg = 1.0 / jnp.sum(jnp.exp(lg - mg), axis=-1, keepdims=True)
    lo = MOE_GROUPS + gsel * MOE_EXPERTS
    le = jnp.where((lane >= lo) & (lane < lo + MOE_EXPERTS), logits, neg)
    v1 = jnp.max(le, axis=-1, keepdims=True)
    i1 = jnp.min(jnp.where(le == v1, lane, big), axis=-1, keepdims=True)
    le2 = jnp.where(lane == i1, neg, le)
    v2 = jnp.max(le2, axis=-1, keepdims=True)
    i2 = jnp.min(jnp.where(le2 == v2, lane, big), axis=-1, keepdims=True)
    e2 = jnp.exp(v2 - v1)
    w1 = pg / (1.0 + e2)
    w2 = pg * e2 / (1.0 + e2)
    id1 = (i1 - MOE_GROUPS).astype(F32)
    id2 = (i2 - MOE_GROUPS).astype(F32)
    return jnp.where(lane == 0, id1, jnp.where(lane == 1, id2,
                     jnp.where(lane == 2, w1, jnp.where(lane == 3, w2, 0.0))))


def _to_row_major(x):
    parts = [x[:, s * LANES:(s + 1) * LANES] for s in range(x.shape[1] // LANES)]
    return pltpu.einshape("stl->tsl", jnp.stack(parts, axis=0))


def _from_row_major(x3, dtype):
    xt = pltpu.einshape("tsl->stl", x3)
    return jnp.concatenate([xt[s].astype(dtype) for s in range(x3.shape[1])], axis=1)


def _norm_mod_route_kernel(x_ref, g_ref, sh_ref, sc_ref, wr_ref, h_ref, r_ref):
    x = x_ref[...]
    ms = jnp.mean(x * x, axis=-1, keepdims=True)
    xn = x * lax.rsqrt(ms + EPS) * g_ref[...]
    h = xn * (1.0 + sc_ref[...]) + sh_ref[...]
    h_ref[...] = h
    logits = jnp.dot(h, wr_ref[...], preferred_element_type=F32,
                     precision=lax.Precision.HIGHEST)
    r_ref[...] = _route(logits)


def norm_mod(x, gain, mods, layer, m_shift, n_rows, n_tiles, block_fn, row_fn,
             router_w=None, out_dtype=BF16):
    split = None
    if isinstance(x, tuple):
        xc, xl, split = x
        d = xc.shape[1]
        x_specs = [pl.BlockSpec((ROW_TILE, d), lambda i: (split.ctx_src_block(i), 0)),
                   pl.BlockSpec((ROW_TILE, d), lambda i: (split.lat_src_block(i), 0))]
        x_args = [xc, xl]
        assert router_w is None
    else:
        d = x.shape[1]
        x_specs = [pl.BlockSpec((ROW_TILE, d), lambda i: (block_fn(i), 0))]
        x_args = [x]
    x_spec = x_specs[0]
    g_spec = pl.BlockSpec((1, d), lambda i: (0, 0))
    sh_spec = pl.BlockSpec((None, 1, d), _mod_spec(layer, m_shift, row_fn, n_rows, 1, 0))
    sc_spec = pl.BlockSpec((None, 1, d), _mod_spec(layer, m_shift + 1, row_fn, n_rows, 1, 0))
    o_spec = pl.BlockSpec((ROW_TILE, d), lambda i: (i, 0))
    rows = n_tiles * ROW_TILE
    if router_w is None:
        return pl.pallas_call(
            functools.partial(_norm_mod_kernel, split),
            grid=(n_tiles,),
            in_specs=x_specs + [g_spec, sh_spec, sc_spec],
            out_specs=o_spec,
            out_shape=jax.ShapeDtypeStruct((rows, d), out_dtype),
            compiler_params=_cparams(("parallel",)),
            name="norm_mod",
        )(*x_args, gain.reshape(1, d), mods, mods)
    return pl.pallas_call(
        _norm_mod_route_kernel,
        grid=(n_tiles,),
        in_specs=[x_spec, g_spec, sh_spec, sc_spec,
                  pl.BlockSpec((d, LANES), lambda i: (0, 0))],
        out_specs=[o_spec, pl.BlockSpec((ROW_TILE, LANES), lambda i: (i, 0))],
        out_shape=[jax.ShapeDtypeStruct((rows, d), F32),
                   jax.ShapeDtypeStruct((rows, LANES), F32)],
        compiler_params=_cparams(("parallel",)),
        name="norm_mod_route",
    )(x, gain.reshape(1, d), mods, mods, router_w)


def _stage_weights(w_refs, wb_refs):
    @pl.when(pl.program_id(1) == 0)
    def _():
        for w_ref, wb_ref in zip(w_refs, wb_refs):
            wb_ref[...] = w_ref[...].astype(BF16)


def _mm_kernel(x_ref, w_ref, o_ref, wb_ref):
    _stage_weights([w_ref], [wb_ref])
    o_ref[...] = jnp.dot(x_ref[...], wb_ref[...], preferred_element_type=F32).astype(o_ref.dtype)


def _row_block(m, want=1024):
    tm = want
    while m % tm:
        tm //= 2
    return tm


def matmul(x, w, col0, n, out_dtype, tn=512):
    m, k = x.shape
    tm = _row_block(m)
    tn = min(tn, n)
    cb = col0 // tn
    return pl.pallas_call(
        _mm_kernel,
        grid=(n // tn, m // tm),
        in_specs=[pl.BlockSpec((tm, k), lambda j, i: (i, 0)),
                  pl.BlockSpec((k, tn), lambda j, i: (0, cb + j))],
        out_specs=pl.BlockSpec((tm, tn), lambda j, i: (i, j)),
        out_shape=jax.ShapeDtypeStruct((m, n), out_dtype),
        scratch_shapes=[pltpu.VMEM((k, tn), BF16)],
        compiler_params=_cparams(("parallel", "arbitrary")),
        name="matmul",
    )(x, w)


def _mm_glu_kernel(x_ref, wa_ref, wg_ref, o_ref, wab_ref, wgb_ref):
    _stage_weights([wa_ref, wg_ref], [wab_ref, wgb_ref])
    x = x_ref[...]
    a = jnp.dot(x, wab_ref[...], preferred_element_type=F32)
    b = jnp.dot(x, wgb_ref[...], preferred_element_type=F32)
    o_ref[...] = (a * _sigmoid(b)).astype(o_ref.dtype)


def matmul_glu(x, w, col_a, col_g, n, out_dtype, tn=256):
    m, k = x.shape
    tm = _row_block(m)
    ca, cg = col_a // tn, col_g // tn
    return pl.pallas_call(
        _mm_glu_kernel,
        grid=(n // tn, m // tm),
        in_specs=[pl.BlockSpec((tm, k), lambda j, i: (i, 0)),
                  pl.BlockSpec((k, tn), lambda j, i: (0, ca + j)),
                  pl.BlockSpec((k, tn), lambda j, i: (0, cg + j))],
        out_specs=pl.BlockSpec((tm, tn), lambda j, i: (i, j)),
        out_shape=jax.ShapeDtypeStruct((m, n), out_dtype),
        scratch_shapes=[pltpu.VMEM((k, tn), BF16), pltpu.VMEM((k, tn), BF16)],
        compiler_params=_cparams(("parallel", "arbitrary")),
        name="matmul_glu",
    )(x, w, w)


def _mm_res_kernel(n_lhs, split, *refs):
    xs = refs[:n_lhs]
    ws = refs[n_lhs:2 * n_lhs]
    n_res = 1 if split is None else 2
    res_refs = refs[2 * n_lhs:2 * n_lhs + n_res]
    gate_ref, o_ref = refs[2 * n_lhs + n_res:2 * n_lhs + n_res + 2]
    wbs = refs[2 * n_lhs + n_res + 2:]
    _stage_weights(ws, wbs)
    acc = jnp.dot(xs[0][...], wbs[0][...], preferred_element_type=F32)
    for x_ref, wb_ref in zip(xs[1:], wbs[1:]):
        acc = acc + jnp.dot(x_ref[...], wb_ref[...], preferred_element_type=F32)
    if split is None:
        res = res_refs[0][...]
    else:
        res = jnp.where(split.tile_is_ctx(pl.program_id(1)), res_refs[0][...], res_refs[1][...])
    o_ref[...] = res + gate_ref[...] * acc


def matmul_residual(lhs, w, res, mods, layer, m_gate, n_rows, n_tiles, res_block_fn, row_fn,
                    tn=1024):
    n_lhs = len(lhs)
    n = w.shape[1]
    widths = [a.shape[1] for a in lhs]
    assert all(wd == widths[0] for wd in widths) and sum(widths) == w.shape[0]
    in_specs = [pl.BlockSpec((ROW_TILE, wd), lambda j, i: (i, 0)) for wd in widths]
    in_specs += [pl.BlockSpec((wd, tn), functools.partial(lambda q, j, i: (q, j), q),
                              pipeline_mode=pl.Buffered(1))
                 for q, wd in enumerate(widths)]
    split = None
    if isinstance(res, tuple):
        rc, rl, split = res
        in_specs += [pl.BlockSpec((ROW_TILE, tn), lambda j, i: (split.ctx_src_block(i), j)),
                     pl.BlockSpec((ROW_TILE, tn), lambda j, i: (split.lat_src_block(i), j))]
        res_args = [rc, rl]
    else:
        in_specs.append(pl.BlockSpec((ROW_TILE, tn), lambda j, i: (res_block_fn(i), j)))
        res_args = [res]
    in_specs.append(
        pl.BlockSpec((None, 1, tn), _mod_spec(layer, m_gate, row_fn, n_rows, 2, 1, col_axis=0)))
    return pl.pallas_call(
        functools.partial(_mm_res_kernel, n_lhs, split),
        grid=(n // tn, n_tiles),
        in_specs=in_specs,
        out_specs=pl.BlockSpec((ROW_TILE, tn), lambda j, i: (i, j)),
        out_shape=jax.ShapeDtypeStruct((n_tiles * ROW_TILE, n), F32),
        scratch_shapes=[pltpu.VMEM((wd, tn), BF16) for wd in widths],
        compiler_params=_cparams(("parallel", "arbitrary")),
        name="matmul_residual",
    )(*lhs, *([w] * n_lhs), *res_args, mods)


def _conv_kernel(width, mode, per_b, ctx_t, x_ref, prev_ref, next_ref, w_ref, p0_ref, p1_ref,
                 o_ref, xp_ref, acc_ref, sh_ref):
    t = pl.program_id(0) % per_b
    first = (t == 0) | (t == ctx_t)
    last = (t == ctx_t - 1) | (t == per_b - 1)
    c = x_ref.shape[1]
    xp_ref[0:HALO, :] = jnp.where(first, 0.0, prev_ref[...])
    xp_ref[HALO:HALO + ROW_TILE, :] = x_ref[...]
    xp_ref[HALO + ROW_TILE:, :] = jnp.where(last, 0.0, next_ref[...])
    off = HALO - width // 2
    sub = SUBLANES
    taps_of = [[k for k in range(width) if (off + k) % sub == ph] for ph in range(sub)]
    span = ROW_TILE + sub * ((off + width - 1) // sub)

    def chunk(j, carry):
        cs = pl.ds(pl.multiple_of(j * LANES, LANES), LANES)
        for ph in range(1, sub):
            if taps_of[ph]:
                sh_ref[ph, 0:span, :] = xp_ref[ph:ph + span, cs]
        acc = jnp.zeros((ROW_TILE, LANES), F32)
        for ph in range(sub):
            for k in taps_of[ph]:
                lo = off + k - ph
                if ph == 0:
                    win = xp_ref[lo:lo + ROW_TILE, cs]
                else:
                    win = sh_ref[ph, lo:lo + ROW_TILE, :]
                acc = acc + win * w_ref[k:k + 1, cs]
        acc_ref[:, cs] = acc
        return carry

    lax.fori_loop(0, c // LANES, chunk, 0)
    y = acc_ref[...]
    if mode == "layernorm_silu":
        mu = jnp.mean(y, axis=-1, keepdims=True)
        yc = y - mu
        var = jnp.mean(yc * yc, axis=-1, keepdims=True)
        y = yc * lax.rsqrt(var + EPS) * p0_ref[...] + p1_ref[...]
    else:
        y = y + p0_ref[...]
    o_ref[...] = _silu(y).astype(o_ref.dtype)


def seq_conv(x, w, p0, p1, mode, geom, out_dtype=BF16):
    rows, c = x.shape
    width = w.shape[0]
    hb = ROW_TILE // HALO
    n_halo = rows // HALO
    kern = functools.partial(_conv_kernel, width, mode, geom.per_b, geom.ctx_t)
    return pl.pallas_call(
        kern,
        grid=(geom.n_all,),
        in_specs=[
            pl.BlockSpec((ROW_TILE, c), lambda i: (i, 0)),
            pl.BlockSpec((HALO, c), lambda i: (jnp.maximum(i * hb - 1, 0), 0)),
            pl.BlockSpec((HALO, c), lambda i: (jnp.minimum((i + 1) * hb, n_halo - 1), 0)),
            pl.BlockSpec((width, c), lambda i: (0, 0)),
            pl.BlockSpec((1, c), lambda i: (0, 0)),
            pl.BlockSpec((1, c), lambda i: (0, 0)),
        ],
        out_specs=pl.BlockSpec((ROW_TILE, c), lambda i: (i, 0)),
        out_shape=jax.ShapeDtypeStruct((rows, c), out_dtype),
        scratch_shapes=[pltpu.VMEM((ROW_TILE + 2 * HALO, c), F32),
                        pltpu.VMEM((ROW_TILE, c), F32),
                        pltpu.VMEM((SUBLANES, ROW_TILE + 2 * HALO, LANES), F32)],
        compiler_params=_cparams(("parallel",)),
        name="seq_conv_" + mode,
    )(x, x, x, w, p0.reshape(1, c), p1.reshape(1, c))


def _ssd_kernel(n_chunks, ctx_chunks, xs_ref, b_ref, c_ref, dtraw_ref, bias_ref, aneg_ref,
                y_ref, state_ref, tr_ref):
    T = SSM_CHUNK
    rows = lax.broadcasted_iota(jnp.int32, (T, T), 0)
    cols = lax.broadcasted_iota(jnp.int32, (T, T), 1)
    lane = lax.broadcasted_iota(jnp.int32, (T, T), 1)
    left = lane < SSM_HEAD_DIM
    state_ref[...] = jnp.zeros_like(state_ref)
    tr_ref[...] = jnp.zeros_like(tr_ref)
    n_heads = 4

    def step(s, carry):
        chunk(0, s)
        chunk(1, jnp.where(s < ctx_chunks, ctx_chunks - 1 - s, n_chunks - 1 + ctx_chunks - s))
        return carry

    def chunk(d, c):
        diff = rows - cols if d == 0 else cols - rows
        tri = jnp.where(diff <= 0, 1.0, 0.0).astype(F32)
        keep = diff >= 0
        r0 = pl.multiple_of(c * T, T)
        raw = dtraw_ref[d, :, pl.ds(r0, T)]
        z = raw + bias_ref[d]
        dt_row = jnp.maximum(z, 0.0) + jnp.log1p(jnp.exp(-jnp.abs(z)))
        da_row = dt_row * aneg_ref[d]
        cum_row = jnp.dot(da_row, tri, preferred_element_type=F32,
                          precision=lax.Precision.HIGHEST)
        tot_row = jnp.sum(da_row, axis=-1, keepdims=True)
        tr_ref[d, 0:8, :] = cum_row
        colform = tr_ref[d].T
        bmat = b_ref[pl.ds(r0, T), :]
        cmat = c_ref[pl.ds(r0, T), :]
        cb = lax.dot_general(cmat, bmat, (((1,), (1,)), ((), ())),
                             preferred_element_type=F32)
        bt = bmat.astype(F32).T
        for pair in range(n_heads // 2):
            xs_pair = xs_ref[pl.ds(r0, T), pair * LANES:(pair + 1) * LANES]
            ms, bds, ecs, ets = [], [], [], []
            for q in range(2):
                r = pair * 2 + q
                cum_c = jnp.broadcast_to(colform[:, r:r + 1], (T, T))
                cum_r = cum_row[r:r + 1, :]
                dt_r = dt_row[r:r + 1, :]
                seg = jnp.where(keep, cum_c - cum_r, -jnp.inf)
                ms.append((cb * jnp.exp(seg) * dt_r).astype(BF16))
                dec_r = jnp.exp(tot_row[r:r + 1, :] - cum_r) * dt_r
                bds.append((bt * dec_r).astype(BF16))
                ecs.append(jnp.exp(cum_c))
                ets.append(jnp.exp(tot_row[r:r + 1, :]))
            xs32 = xs_pair.astype(F32)
            xd2 = jnp.concatenate([jnp.where(left, xs32, 0.0).astype(BF16),
                                   jnp.where(left, 0.0, xs32).astype(BF16)], axis=0)
            m2 = jnp.concatenate(ms, axis=1)
            y_diag = jnp.dot(m2, xd2, preferred_element_type=F32)
            st = state_ref[d, pair]
            y_off = jnp.dot(cmat, st.astype(BF16), preferred_element_type=F32)
            y_off = y_off * jnp.where(left, ecs[0], ecs[1])
            y_ref[d, pl.ds(r0, T), pair * LANES:(pair + 1) * LANES] = (y_diag + y_off).astype(y_ref.dtype)
            bd2 = jnp.concatenate(bds, axis=1)
            upd = jnp.dot(bd2, xd2, preferred_element_type=F32)
            state_ref[d, pair] = st * jnp.where(left[0:1, :], ets[0], ets[1]) + upd

    lax.fori_loop(0, n_chunks, step, 0)


def ssd_scan(xbc, dtraw_rows, bias_rows, aneg_rows, geom):
    rows = xbc.shape[0]
    per = geom.lc + geom.n_lat
    n_chunks = per // SSM_CHUNK
    ctx_chunks = geom.lc // SSM_CHUNK
    ssm_d = SSM_HEADS * SSM_HEAD_DIM
    gw = 4 * SSM_HEAD_DIM
    xs_blocks = ssm_d // gw
    kern = functools.partial(_ssd_kernel, n_chunks, ctx_chunks)
    return pl.pallas_call(
        kern,
        grid=(geom.bsz, SSM_GROUPS),
        in_specs=[
            pl.BlockSpec((per, gw), lambda b, g: (b, g)),
            pl.BlockSpec((per, SSM_STATE), lambda b, g: (b, 2 * xs_blocks + g)),
            pl.BlockSpec((per, SSM_STATE), lambda b, g: (b, 2 * xs_blocks + SSM_GROUPS + g)),
            pl.BlockSpec((None, 2, None, 8, per), lambda b, g: (b, 0, g, 0, 0)),
            pl.BlockSpec((2, None, 8, LANES), lambda b, g: (0, g, 0, 0)),
            pl.BlockSpec((2, None, 8, LANES), lambda b, g: (0, g, 0, 0)),
        ],
        out_specs=pl.BlockSpec((2, per, gw), lambda b, g: (0, b, g)),
        out_shape=jax.ShapeDtypeStruct((2, rows, ssm_d), BF16),
        scratch_shapes=[pltpu.VMEM((2, 2, SSM_STATE, LANES), F32),
                        pltpu.VMEM((2, LANES, SSM_CHUNK), F32)],
        compiler_params=_cparams(("parallel", "parallel")),
        name="ssd_scan",
    )(xbc, xbc, xbc, dtraw_rows, bias_rows, aneg_rows)


def _ssd_merge_kernel(yf_ref, yb_ref, xs_ref, z_ref, d_ref, g_ref, o_ref):
    y = d_ref[...] * xs_ref[...].astype(F32) + yf_ref[...].astype(F32) + yb_ref[...].astype(F32)
    gated = y * _silu(z_ref[...].astype(F32))
    ms = jnp.mean(gated * gated, axis=-1, keepdims=True)
    o_ref[...] = (gated * lax.rsqrt(ms + EPS) * g_ref[...]).astype(o_ref.dtype)


def ssd_merge(y2, xbc, z, d_full, g, geom):
    rows, ssm_d = z.shape
    spec = pl.BlockSpec((ROW_TILE, ssm_d), lambda i: (i, 0))
    vec = pl.BlockSpec((1, ssm_d), lambda i: (0, 0))
    return pl.pallas_call(
        _ssd_merge_kernel,
        grid=(geom.n_all,),
        in_specs=[pl.BlockSpec((None, ROW_TILE, ssm_d), lambda i: (0, i, 0)),
                  pl.BlockSpec((None, ROW_TILE, ssm_d), lambda i: (1, i, 0)),
                  spec, spec, vec, vec],
        out_specs=spec,
        out_shape=jax.ShapeDtypeStruct((rows, ssm_d), BF16),
        compiler_params=_cparams(("parallel",)),
        name="ssd_merge",
    )(y2, y2, xbc, z, d_full.reshape(1, ssm_d), g.reshape(1, ssm_d))


def _qk_prep_kernel(n_q, n_k, scale, qkv_ref, cos_ref, sin_ref, gq_ref, gk_ref, q_ref, k_ref):
    cosf = cos_ref[...]
    sinf = sin_ref[...]
    hd = ATT_HEAD_DIM
    for h in range(n_q + n_k):
        x = qkv_ref[:, h * hd:(h + 1) * hd].astype(F32)
        g = gq_ref[...] if h < n_q else gk_ref[...]
        ms = jnp.mean(x * x, axis=-1, keepdims=True)
        xn = x * lax.rsqrt(ms + EPS) * g
        rot = pltpu.roll(xn, hd // 2, axis=1)
        o = xn * cosf + rot * sinf
        if h < n_q:
            q_ref[:, h * hd:(h + 1) * hd] = (o * scale).astype(q_ref.dtype)
        else:
            k_ref[:, (h - n_q) * hd:(h - n_q + 1) * hd] = o.astype(k_ref.dtype)


def qk_prep(qkv, cosf, sinf, gq, gk, geom):
    rows = qkv.shape[0]
    hd = ATT_HEAD_DIM
    qc, kc = ATT_HEADS * hd, ATT_KV_HEADS * hd
    kern = functools.partial(_qk_prep_kernel, ATT_HEADS, ATT_KV_HEADS, hd ** -0.5)
    per_b = geom.per_b
    return pl.pallas_call(
        kern,
        grid=(geom.n_all,),
        in_specs=[pl.BlockSpec((ROW_TILE, qc + kc), lambda i: (i, 0)),
                  pl.BlockSpec((ROW_TILE, hd), lambda i: (i % per_b, 0)),
                  pl.BlockSpec((ROW_TILE, hd), lambda i: (i % per_b, 0)),
                  pl.BlockSpec((1, hd), lambda i: (0, 0)),
                  pl.BlockSpec((1, hd), lambda i: (0, 0))],
        out_specs=[pl.BlockSpec((ROW_TILE, qc), lambda i: (i, 0)),
                   pl.BlockSpec((ROW_TILE, kc), lambda i: (i, 0))],
        out_shape=[jax.ShapeDtypeStruct((rows, qc), BF16),
                   jax.ShapeDtypeStruct((rows, kc), BF16)],
        compiler_params=_cparams(("parallel",)),
        name="qk_prep",
    )(qkv, cosf, sinf, gq.reshape(1, hd), gk.reshape(1, hd))


def _attn_kernel(rep, q_ref, k_ref, v_ref, o_ref):
    hd = ATT_HEAD_DIM
    k = k_ref[...]
    v = v_ref[...]
    v1 = jnp.concatenate([v, jnp.ones_like(v)], axis=1)
    for r in range(rep):
        q = q_ref[:, r * hd:(r + 1) * hd]
        s = lax.dot_general(q, k, (((1,), (1,)), ((), ())), preferred_element_type=F32)
        m = jnp.max(s, axis=-1, keepdims=True)
        p = jnp.exp((s - m).astype(BF16))
        o = jnp.dot(p, v1, preferred_element_type=F32)
        o_ref[:, r * hd:(r + 1) * hd] = (o[:, :hd] / o[:, hd:hd + 1]).astype(o_ref.dtype)


def attention(q, k, qkv, geom):
    hd = ATT_HEAD_DIM
    rep = ATT_HEADS // ATT_KV_HEADS
    per = geom.lc + geom.n_lat
    v_col0 = (ATT_HEADS + ATT_KV_HEADS)
    lat_t, per_b, ctx_t = geom.lat_t, geom.per_b, geom.ctx_t
    return pl.pallas_call(
        functools.partial(_attn_kernel, rep),
        grid=(geom.bsz, ATT_KV_HEADS, lat_t),
        in_specs=[pl.BlockSpec((ROW_TILE, rep * hd), lambda b, g, t: (b * per_b + ctx_t + t, g)),
                  pl.BlockSpec((per, hd), lambda b, g, t: (b, g)),
                  pl.BlockSpec((per, hd), lambda b, g, t: (b, v_col0 + g))],
        out_specs=pl.BlockSpec((ROW_TILE, rep * hd), lambda b, g, t: (b * lat_t + t, g)),
        out_shape=jax.ShapeDtypeStruct((geom.bsz * geom.n_lat, ATT_HEADS * hd), BF16),
        compiler_params=_cparams(("parallel", "parallel", "parallel")),
        name="attention",
    )(q, k, qkv)


def _start_row_gather(ids_ref, first, stride, src_ref, dst_ref, sem):
    def issue(r, carry):
        row = ids_ref[first + stride * r]
        pltpu.make_async_copy(src_ref.at[pl.ds(row, 1)], dst_ref.at[pl.ds(r, 1)], sem).start()
        return carry

    lax.fori_loop(0, ROW_TILE, issue, 0, unroll=8)


def _wait_row_gather(src_ref, dst_ref, sem):
    pltpu.make_async_copy(src_ref.at[pl.ds(0, ROW_TILE)], dst_ref, sem).wait()


def _cast_kernel(x_ref, o_ref):
    o_ref[...] = x_ref[...].astype(o_ref.dtype)


def cast_experts(w_all, layer, dtype=BF16):
    depth, g, e, k, n = w_all.shape
    ne = g * e
    return pl.pallas_call(
        _cast_kernel,
        grid=(ne,),
        in_specs=[pl.BlockSpec((None, k, n), lambda i: (layer * ne + i, 0, 0))],
        out_specs=pl.BlockSpec((None, k, n), lambda i: (i, 0, 0)),
        out_shape=jax.ShapeDtypeStruct((ne, k, n), dtype),
        compiler_params=_cparams(("parallel",)),
        name="cast_experts",
    )(w_all.reshape(depth * ne, k, n))


def _expert_mlp_kernel(te_ref, nt_ref, ids_ref, h_ref, cw_ref, w1_ref, w3_ref, w2_ref, o_ref,
                       xbuf_ref, sem):
    t = pl.program_id(0)
    nt = nt_ref[0]
    slot = t % 2

    @pl.when(t == 0)
    def _():
        _start_row_gather(ids_ref, 0, 1, h_ref, xbuf_ref.at[0], sem.at[0])

    @pl.when(t + 1 < nt)
    def _():
        _start_row_gather(ids_ref, (t + 1) * ROW_TILE, 1, h_ref, xbuf_ref.at[1 - slot],
                          sem.at[1 - slot])

    @pl.when(t < nt)
    def _():
        _wait_row_gather(h_ref, xbuf_ref.at[slot], sem.at[slot])
        x = xbuf_ref[slot].astype(BF16)
        a = jnp.dot(x, w1_ref[...], preferred_element_type=F32)
        b = jnp.dot(x, w3_ref[...], preferred_element_type=F32)
        hid = _silu(a) * b * cw_ref[...]
        y = jnp.dot(hid.astype(BF16), w2_ref[...], preferred_element_type=F32)
        o_ref[...] = _to_row_major(y)

    @pl.when(t >= nt)
    def _():
        o_ref[...] = jnp.zeros_like(o_ref)


def expert_mlp(h, row_ids, cw, w1, w3, w2, tile_expert, n_tiles_used):
    p = row_ids.shape[0]
    d = h.shape[1]
    s, l = d // LANES, LANES
    hdim = w1.shape[2]
    n_tiles = p // ROW_TILE

    def wmap(t, te, nt, ids):
        return (te[t], 0, 0)

    return pl.pallas_call(
        _expert_mlp_kernel,
        grid_spec=pltpu.PrefetchScalarGridSpec(
            num_scalar_prefetch=3,
            grid=(n_tiles,),
            in_specs=[pl.BlockSpec(memory_space=pl.ANY),
                      pl.BlockSpec((ROW_TILE, 1), lambda t, te, nt, ids: (t, 0)),
                      pl.BlockSpec((None, d, hdim), wmap),
                      pl.BlockSpec((None, d, hdim), wmap),
                      pl.BlockSpec((None, hdim, d), wmap)],
            out_specs=pl.BlockSpec((ROW_TILE, s, l), lambda t, te, nt, ids: (t, 0, 0)),
            scratch_shapes=[pltpu.VMEM((2, ROW_TILE, d), F32), pltpu.SemaphoreType.DMA((2,))],
        ),
        out_shape=jax.ShapeDtypeStruct((p, s, l), F32),
        compiler_params=_cparams(("arbitrary",)),
        name="expert_mlp",
    )(tile_expert, n_tiles_used, row_ids, h, cw, w1, w3, w2)


def _moe_combine_kernel(final_norm, pos_ref, ys_ref, res_ref, gate_ref, g_ref, o_ref, buf_ref, sem):
    i = pl.program_id(0)
    n = pl.num_programs(0)
    slot = i % 2

    def start(tile, s):
        for q in range(2):
            _start_row_gather(pos_ref, tile * (2 * ROW_TILE) + q, 2, ys_ref, buf_ref.at[s, q],
                              sem.at[s, q])

    @pl.when(i == 0)
    def _():
        start(0, 0)

    @pl.when(i + 1 < n)
    def _():
        start(i + 1, 1 - slot)

    for q in range(2):
        _wait_row_gather(ys_ref, buf_ref.at[slot, q], sem.at[slot, q])
    y = _from_row_major(buf_ref[slot, 0] + buf_ref[slot, 1], F32)
    x = res_ref[...] + gate_ref[...] * y
    if final_norm:
        ms = jnp.mean(x * x, axis=-1, keepdims=True)
        x = x * lax.rsqrt(ms + EPS) * g_ref[...]
    o_ref[...] = x


def moe_combine(ys, pos, res, mods, layer, n_rows, n_tiles, row_fn, final_gain=None):
    d = res.shape[1]
    _, s, l = ys.shape
    tile = pl.BlockSpec((ROW_TILE, d), lambda i, pos: (i, 0))
    mod_map = _mod_spec(layer, 5, row_fn, n_rows, 1, 0)
    gain = jnp.ones((d,), F32) if final_gain is None else final_gain
    return pl.pallas_call(
        functools.partial(_moe_combine_kernel, final_gain is not None),
        grid_spec=pltpu.PrefetchScalarGridSpec(
            num_scalar_prefetch=1,
            grid=(n_tiles,),
            in_specs=[pl.BlockSpec(memory_space=pl.ANY),
                      tile,
                      pl.BlockSpec((None, 1, d), lambda i, pos: mod_map(i)),
                      pl.BlockSpec((1, d), lambda i, pos: (0, 0))],
            out_specs=tile,
            scratch_shapes=[pltpu.VMEM((2, 2, ROW_TILE, s, l), F32),
                            pltpu.SemaphoreType.DMA((2, 2))],
        ),
        out_shape=jax.ShapeDtypeStruct((n_tiles * ROW_TILE, d), F32),
        compiler_params=_cparams(("arbitrary",)),
        name="moe_combine",
    )(pos.reshape(-1), ys, res, mods, gain.reshape(1, d))


def _dispatch_plan(route):
    t = route.shape[0]
    ids = route[:, 0:2].astype(jnp.int32)
    wts = route[:, 2:4]
    flat = ids.reshape(-1)
    n_assign = flat.shape[0]
    p_max = n_assign + N_EXPERTS * ROW_TILE
    p_max = (p_max // ROW_TILE) * ROW_TILE
    onehot = (flat[:, None] == jnp.arange(N_EXPERTS, dtype=jnp.int32)[None, :]).astype(jnp.int32)
    counts = jnp.sum(onehot, axis=0)
    padded = ((counts + ROW_TILE - 1) // ROW_TILE) * ROW_TILE
    ends = jnp.cumsum(padded)
    offs = ends - padded
    starts = jnp.cumsum(counts) - counts
    order = jnp.argsort(flat, stable=True).astype(jnp.int32)
    n_tiles = p_max // ROW_TILE
    tile_first = jnp.arange(n_tiles, dtype=jnp.int32) * ROW_TILE
    tile_e = jnp.sum((tile_first[:, None] >= ends[None, :]).astype(jnp.int32), axis=1)
    tile_e = jnp.minimum(tile_e, N_EXPERTS - 1)
    tile_rank = tile_first - offs[tile_e]
    within = jnp.arange(ROW_TILE, dtype=jnp.int32)[None, :]
    rank = tile_rank[:, None] + within
    valid = (rank < counts[tile_e][:, None]) & (tile_first[:, None] < ends[-1])
    src_sorted = jnp.clip(starts[tile_e][:, None] + rank, 0, n_assign - 1).reshape(-1)
    valid = valid.reshape(-1)
    assign = order[src_sorted]
    row_ids = jnp.where(valid, assign // 2, 0).astype(jnp.int32)
    cw = jnp.where(valid, wts.reshape(-1)[assign], 0.0).astype(F32)
    sorted_pos = jnp.argsort(order).astype(jnp.int32)
    shift = jnp.sum(onehot * (offs - starts)[None, :], axis=1)
    pos = (sorted_pos + shift).astype(jnp.int32).reshape(t, 2)
    n_tiles_used = (ends[-1] // ROW_TILE).astype(jnp.int32).reshape(1)
    last_e = tile_e[jnp.maximum(n_tiles_used[0] - 1, 0)]
    tile_idx = jnp.arange(n_tiles, dtype=jnp.int32)
    tile_expert = jnp.where(tile_idx < n_tiles_used[0], tile_e, last_e).astype(jnp.int32)
    return row_ids, cw, pos, tile_expert, n_tiles_used


def hier_moe_block(h, route, w1, w3, w2):
    row_ids, cw, pos, tile_expert, n_tiles_used = _dispatch_plan(route)
    ys = expert_mlp(h, row_ids, cw[:, None], w1, w3, w2, tile_expert, n_tiles_used)
    return ys, pos


def _rope_tables(geom):
    n_lat = geom.n_lat
    half = ATT_HEAD_DIM // 2
    nfreq = half // 2
    row = jnp.repeat(jnp.arange(n_lat // GRID_W, dtype=F32), GRID_W)
    col = (jnp.arange(n_lat) % GRID_W).astype(F32)
    inv = ROPE_THETA ** (-jnp.arange(nfreq, dtype=F32) / nfreq)
    ang = jnp.concatenate([row[:, None] * inv, col[:, None] * inv], axis=-1)
    cos, sin = jnp.cos(ang), jnp.sin(ang)
    cosf = jnp.concatenate([cos, cos], axis=-1)
    sinf = jnp.concatenate([-sin, sin], axis=-1)
    cosf = jnp.concatenate([jnp.ones((geom.lc, ATT_HEAD_DIM), F32), cosf], axis=0)
    sinf = jnp.concatenate([jnp.zeros((geom.lc, ATT_HEAD_DIM), F32), sinf], axis=0)
    return cosf, sinf


def _router_weights(router_g, router_e):
    d = router_g.shape[0]
    pad = jnp.zeros((d, LANES - MOE_GROUPS - N_EXPERTS), F32)
    return jnp.concatenate([router_g, router_e, pad], axis=1)


def mixer_even(xs, mods, layer, j, n_rows, geom, norm_g, w_in, conv_dw, conv_ln_g, conv_ln_b,
               ssm_conv_w, ssm_conv_b, ssm_dt_bias, ssm_a_log, ssm_d, ssm_norm_g, w_out,
               debug=False):
    bsz, per = geom.bsz, geom.lc + geom.n_lat
    conv_d = conv_dw.shape[1]
    ssm_dd = SSM_HEADS * SSM_HEAD_DIM
    xbc_dim = ssm_conv_w.shape[1]
    h = norm_mod(xs, norm_g, mods, layer, 0, n_rows, geom.n_all, geom.all_block, geom.all_mod_row)
    c0, c1, c2, c3 = conv_d, 2 * conv_d, 2 * conv_d + ssm_dd, 2 * conv_d + ssm_dd + xbc_dim
    glu = matmul_glu(h, w_in, 0, c0, conv_d, F32)
    z = matmul(h, w_in, c1, ssm_dd, BF16)
    xbc_raw = matmul(h, w_in, c2, xbc_dim, F32)
    w_dt = jnp.pad(w_in[:, c3:], ((0, 0), (0, LANES - 2 * SSM_HEADS)))
    dt_raw = matmul(h, w_dt, 0, LANES, F32)[:, :2 * SSM_HEADS]
    conf = seq_conv(glu, conv_dw, conv_ln_g, conv_ln_b, "layernorm_silu", geom)
    xbc = seq_conv(xbc_raw, ssm_conv_w, ssm_conv_b, ssm_conv_b, "bias_silu", geom)
    dtr = dt_raw.reshape(bsz, per, 2, SSM_GROUPS, 4).transpose(0, 2, 3, 4, 1)
    dtr = jnp.pad(dtr, ((0, 0), (0, 0), (0, 0), (0, 4), (0, 0)))

    def head_rows(v):
        v = v.astype(F32).reshape(2, SSM_GROUPS, 4)
        v = jnp.pad(v, ((0, 0), (0, 0), (0, 4)))
        return jnp.broadcast_to(v[..., None], (2, SSM_GROUPS, 8, LANES))

    y2 = ssd_scan(xbc, dtr, head_rows(ssm_dt_bias), head_rows(-jnp.exp(ssm_a_log.astype(F32))), geom)
    yn = ssd_merge(y2, xbc, z, jnp.repeat(ssm_d, SSM_HEAD_DIM), ssm_norm_g, geom)
    out = matmul_residual([conf, yn], w_out, xs, mods, layer, 2, n_rows, geom.n_all,
                          geom.all_block, geom.all_mod_row)
    if debug:
        return out, dict(h=h, glu=glu, z=z, xbc_raw=xbc_raw, dt_raw=dt_raw, conf=conf, xbc=xbc,
                         y2=y2, yn=yn)
    return out


def mixer_odd(xs, mods, layer, n_rows, geom, norm_g, w_qkv, q_norm_g, k_norm_g, w_o):
    h = norm_mod(xs, norm_g, mods, layer, 0, n_rows, geom.n_all, geom.all_block, geom.all_mod_row)
    qkv = matmul(h, w_qkv, 0, w_qkv.shape[1], BF16)
    cosf, sinf = _rope_tables(geom)
    q, k = qk_prep(qkv, cosf, sinf, q_norm_g, k_norm_g, geom)
    att = attention(q, k, qkv, geom)
    return matmul_residual([att], w_o, xs, mods, layer, 2, n_rows, geom.n_lat_tiles,
                           geom.lat_block, geom.lat_mod_row)


def moe_layer(xs, mods, layer, n_rows, n_tiles, row_fn, norm_g, router_g, router_e, w1, w3, w2,
              final_gain):
    hm, route = norm_mod(xs, norm_g, mods, layer, 3, n_rows, n_tiles, lambda t: t, row_fn,
                         router_w=_router_weights(router_g, router_e))
    w1 = cast_experts(w1, layer)
    w3 = cast_experts(w3, layer)
    w2 = cast_experts(w2, layer)
    ys, pos = hier_moe_block(hm, route, w1, w3, w2)
    return moe_combine(ys, pos, xs, mods, layer, n_rows, n_tiles, row_fn, final_gain=final_gain)


def kernel(x, c, ctx, c_ctx, w_ada, b_ada, norm_mix, norm_ffn, w_in0, conv_dw, conv_ln_g, conv_ln_b, ssm_conv_w, ssm_conv_b, ssm_dt_bias, ssm_a_log, ssm_d, ssm_norm_g, w_out0, w_qkv, q_norm_g, k_norm_g, w_o, moe_router_g, moe_router_e, moe_w1, moe_w3, moe_w2, norm_final):
    bsz, n_lat, d = x.shape
    lc = ctx.shape[1]
    depth = w_ada.shape[0]
    geom = Geom(bsz, lc, n_lat)
    n_rows = 16

    cvec = jnp.concatenate([c, c_ctx[None, :], jnp.zeros((n_rows - bsz - 1, d), F32)], axis=0)
    mods = ada_mods(cvec, w_ada, b_ada).reshape(depth * n_rows * N_MOD, 1, d)
    xs = (ctx.reshape(bsz * lc, d), x.reshape(bsz * n_lat, d), geom)

    for i in range(depth):
        last = i == depth - 1
        j = i // 2
        if i % 2 == 0:
            if last:
                raise NotImplementedError("a conv/SSD mixer in the last layer is not supported")
            xs = mixer_even(xs, mods, i, j, n_rows, geom, norm_mix[i], w_in0[j], conv_dw[j],
                            conv_ln_g[j], conv_ln_b[j], ssm_conv_w[j], ssm_conv_b[j],
                            ssm_dt_bias[j], ssm_a_log[j], ssm_d[j], ssm_norm_g[j], w_out0[j])
            n_tiles, row_fn = geom.n_all, geom.all_mod_row
        else:
            if not last:
                raise NotImplementedError("an attention mixer before the last layer is not supported")
            xs = mixer_odd(xs, mods, i, n_rows, geom, norm_mix[i], w_qkv[j], q_norm_g[j],
                           k_norm_g[j], w_o[j])
            n_tiles, row_fn = geom.n_lat_tiles, geom.lat_mod_row
        xs = moe_layer(xs, mods, i, n_rows, n_tiles, row_fn, norm_ffn[i], moe_router_g[i],
                       moe_router_e[i], moe_w1, moe_w3, moe_w2, norm_final if last else None)
    return xs.reshape(bsz, n_lat, d)
```

```python
import functools
import math

import jax
import jax.numpy as jnp
from jax import lax
from jax.experimental import pallas as pl
from jax.experimental.pallas import tpu as pltpu

F32 = jnp.float32
BF16 = jnp.bfloat16

EPS = 1e-6
N_MOD = 6
GRID_W = 64
ROPE_THETA = 10000.0
CONV_WIDTH = 31
SSM_CONV_WIDTH = 5
SSM_HEADS = 32
SSM_HEAD_DIM = 64
SSM_GROUPS = 8
SSM_STATE = 128
SSM_CHUNK = 128
ATT_HEADS = 32
ATT_KV_HEADS = 8
ATT_HEAD_DIM = 128
MOE_GROUPS = 4
MOE_EXPERTS = 8
N_EXPERTS = MOE_GROUPS * MOE_EXPERTS

LANES = 128
SUBLANES = 8
ROW_TILE = 256
HALO = 16
VMEM_LIMIT = 56 * 1024 * 1024


def _cparams(sem):
    return pltpu.CompilerParams(dimension_semantics=sem, vmem_limit_bytes=VMEM_LIMIT)


def _silu(v):
    return v * (1.0 / (1.0 + jnp.exp(-v)))


def _sigmoid(v):
    return 1.0 / (1.0 + jnp.exp(-v))


def _ada_kernel(c_ref, w_ref, b_ref, o_ref):
    s = _silu(c_ref[...]).astype(BF16)
    w = w_ref[...].astype(BF16)
    o_ref[...] = jnp.dot(s, w, preferred_element_type=F32) + b_ref[...]


def ada_mods(cvec, w_ada, b_ada, tn=512):
    depth, d, n = w_ada.shape
    rows = cvec.shape[0]
    return pl.pallas_call(
        _ada_kernel,
        grid=(depth, n // tn),
        in_specs=[
            pl.BlockSpec((rows, d), lambda l, j: (0, 0)),
            pl.BlockSpec((None, d, tn), lambda l, j: (l, 0, j)),
            pl.BlockSpec((None, 1, tn), lambda l, j: (l, 0, j)),
        ],
        out_specs=pl.BlockSpec((None, rows, tn), lambda l, j: (l, 0, j)),
        out_shape=jax.ShapeDtypeStruct((depth, rows, n), F32),
        compiler_params=_cparams(("arbitrary", "arbitrary")),
        name="ada_mods",
    )(cvec, w_ada, b_ada.reshape(depth, 1, n))


class Geom:
    def __init__(self, bsz, lc, n_lat):
        self.bsz, self.lc, self.n_lat = bsz, lc, n_lat
        self.per_b = (lc + n_lat) // ROW_TILE
        self.ctx_t = lc // ROW_TILE
        self.lat_t = n_lat // ROW_TILE
        self.n_all = bsz * self.per_b
        self.n_lat_tiles = bsz * self.lat_t

    def all_block(self, i):
        return i

    def lat_block(self, i):
        return (i // self.lat_t) * self.per_b + self.ctx_t + i % self.lat_t

    def all_mod_row(self, i):
        return jnp.where(i % self.per_b < self.ctx_t, self.bsz, i // self.per_b)

    def lat_mod_row(self, i):
        return i // self.lat_t

    def ctx_src_block(self, i):
        return (i // self.per_b) * self.ctx_t + jnp.minimum(i % self.per_b, self.ctx_t - 1)

    def lat_src_block(self, i):
        return (i // self.per_b) * self.lat_t + jnp.clip(i % self.per_b - self.ctx_t, 0,
                                                         self.lat_t - 1)

    def tile_is_ctx(self, i):
        return i % self.per_b < self.ctx_t


def _mod_spec(layer, m, row_fn, n_rows, grid_rank, tile_axis, width=None, col_axis=None):
    def imap(*g):
        r = row_fn(g[tile_axis])
        c = 0 if col_axis is None else g[col_axis]
        return ((layer * n_rows + r) * N_MOD + m, 0, c)
    return imap


def _norm_mod_kernel(geom, *refs):
    if geom is None:
        x_ref, g_ref, sh_ref, sc_ref, h_ref = refs
        x = x_ref[...]
    else:
        xc_ref, xl_ref, g_ref, sh_ref, sc_ref, h_ref = refs
        x = jnp.where(geom.tile_is_ctx(pl.program_id(0)), xc_ref[...], xl_ref[...])
    ms = jnp.mean(x * x, axis=-1, keepdims=True)
    xn = x * lax.rsqrt(ms + EPS) * g_ref[...]
    h_ref[...] = (xn * (1.0 + sc_ref[...]) + sh_ref[...]).astype(h_ref.dtype)


def _route(logits):
    lane = lax.broadcasted_iota(jnp.int32, logits.shape, 1)
    neg = jnp.float32(-jnp.inf)
    big = jnp.int32(LANES)
    lg = jnp.where(lane < MOE_GROUPS, logits, neg)
    mg = jnp.max(lg, axis=-1, keepdims=True)
    gsel = jnp.min(jnp.where(lg == mg, lane, big), axis=-1, keepdims=True)
    pg = 1.0 / jnp.sum(jnp.exp(lg - mg), axis=-1, keepdims=True)
    lo = MOE_GROUPS + gsel * MOE_EXPERTS
    le = jnp.where((lane >= lo) & (lane < lo + MOE_EXPERTS), logits, neg)
    v1 = jnp.max(le, axis=-1, keepdims=True)
    i1 = jnp.min(jnp.where(le == v1, lane, big), axis=-1, keepdims=True)
    le2 = jnp.where(lane == i1, neg, le)
    v2 = jnp.max(le2, axis=-1, keepdims=True)
    i2 = jnp.min(jnp.where(le2 == v2, lane, big), axis=-1, keepdims=True)
    e2 = jnp.exp(v2 - v1)
    w1 = pg / (1.0 + e2)
    w2 = pg * e2 / (1.0 + e2)
    id1 = (i1 - MOE_GROUPS).astype(F32)
    id2 = (i2 - MOE_GROUPS).astype(F32)
    return jnp.where(lane == 0, id1, jnp.where(lane == 1, id2,
                     jnp.where(lane == 2, w1, jnp.where(lane == 3, w2, 0.0))))


def _to_row_major(x):
    parts = [x[:, s * LANES:(s + 1) * LANES] for s in range(x.shape[1] // LANES)]
    return pltpu.einshape("stl->tsl", jnp.stack(parts, axis=0))


def _from_row_major(x3, dtype):
    xt = pltpu.einshape("tsl->stl", x3)
    return jnp.concatenate([xt[s].astype(dtype) for s in range(x3.shape[1])], axis=1)


def _norm_mod_route_kernel(x_ref, g_ref, sh_ref, sc_ref, wr_ref, h_ref, r_ref):
    x = x_ref[...]
    ms = jnp.mean(x * x, axis=-1, keepdims=True)
    xn = x * lax.rsqrt(ms + EPS) * g_ref[...]
    h = xn * (1.0 + sc_ref[...]) + sh_ref[...]
    h_ref[...] = h
    logits = jnp.dot(h, wr_ref[...], preferred_element_type=F32,
                     precision=lax.Precision.HIGHEST)
    r_ref[...] = _route(logits)


def norm_mod(x, gain, mods, layer, m_shift, n_rows, n_tiles, block_fn, row_fn,
             router_w=None, out_dtype=BF16):
    split = None
    if isinstance(x, tuple):
        xc, xl, split = x
        d = xc.shape[1]
        x_specs = [pl.BlockSpec((ROW_TILE, d), lambda i: (split.ctx_src_block(i), 0)),
                   pl.BlockSpec((ROW_TILE, d), lambda i: (split.lat_src_block(i), 0))]
        x_args = [xc, xl]
        assert router_w is None
    else:
        d = x.shape[1]
        x_specs = [pl.BlockSpec((ROW_TILE, d), lambda i: (block_fn(i), 0))]
        x_args = [x]
    x_spec = x_specs[0]
    g_spec = pl.BlockSpec((1, d), lambda i: (0, 0))
    sh_spec = pl.BlockSpec((None, 1, d), _mod_spec(layer, m_shift, row_fn, n_rows, 1, 0))
    sc_spec = pl.BlockSpec((None, 1, d), _mod_spec(layer, m_shift + 1, row_fn, n_rows, 1, 0))
    o_spec = pl.BlockSpec((ROW_TILE, d), lambda i: (i, 0))
    rows = n_tiles * ROW_TILE
    if router_w is None:
        return pl.pallas_call(
            functools.partial(_norm_mod_kernel, split),
            grid=(n_tiles,),
            in_specs=x_specs + [g_spec, sh_spec, sc_spec],
            out_specs=o_spec,
            out_shape=jax.ShapeDtypeStruct((rows, d), out_dtype),
            compiler_params=_cparams(("parallel",)),
            name="norm_mod",
        )(*x_args, gain.reshape(1, d), mods, mods)
    return pl.pallas_call(
        _norm_mod_route_kernel,
        grid=(n_tiles,),
        in_specs=[x_spec, g_spec, sh_spec, sc_spec,
                  pl.BlockSpec((d, LANES), lambda i: (0, 0))],
        out_specs=[o_spec, pl.BlockSpec((ROW_TILE, LANES), lambda i: (i, 0))],
        out_shape=[jax.ShapeDtypeStruct((rows, d), F32),
                   jax.ShapeDtypeStruct((rows, LANES), F32)],
        compiler_params=_cparams(("parallel",)),
        name="norm_mod_route",
    )(x, gain.reshape(1, d), mods, mods, router_w)


def _stage_weights(w_refs, wb_refs):
    @pl.when(pl.program_id(1) == 0)
    def _():
        for w_ref, wb_ref in zip(w_refs, wb_refs):
            wb_ref[...] = w_ref[...].astype(BF16)


def _mm_kernel(x_ref, w_ref, o_ref, wb_ref):
    _stage_weights([w_ref], [wb_ref])
    o_ref[...] = jnp.dot(x_ref[...], wb_ref[...], preferred_element_type=F32).astype(o_ref.dtype)


def _row_block(m, want=1024):
    tm = want
    while m % tm:
        tm //= 2
    return tm


def matmul(x, w, col0, n, out_dtype, tn=512):
    m, k = x.shape
    tm = _row_block(m)
    tn = min(tn, n)
    cb = col0 // tn
    return pl.pallas_call(
        _mm_kernel,
        grid=(n // tn, m // tm),
        in_specs=[pl.BlockSpec((tm, k), lambda j, i: (i, 0)),
                  pl.BlockSpec((k, tn), lambda j, i: (0, cb + j))],
        out_specs=pl.BlockSpec((tm, tn), lambda j, i: (i, j)),
        out_shape=jax.ShapeDtypeStruct((m, n), out_dtype),
        scratch_shapes=[pltpu.VMEM((k, tn), BF16)],
        compiler_params=_cparams(("parallel", "arbitrary")),
        name="matmul",
    )(x, w)


def _mm_glu_kernel(x_ref, wa_ref, wg_ref, o_ref, wab_ref, wgb_ref):
    _stage_weights([wa_ref, wg_ref], [wab_ref, wgb_ref])
    x = x_ref[...]
    a = jnp.dot(x, wab_ref[...], preferred_element_type=F32)
    b = jnp.dot(x, wgb_ref[...], preferred_element_type=F32)
    o_ref[...] = (a * _sigmoid(b)).astype(o_ref.dtype)


def matmul_glu(x, w, col_a, col_g, n, out_dtype, tn=256):
    m, k = x.shape
    tm = _row_block(m)
    ca, cg = col_a // tn, col_g // tn
    return pl.pallas_call(
        _mm_glu_kernel,
        grid=(n // tn, m // tm),
        in_specs=[pl.BlockSpec((tm, k), lambda j, i: (i, 0)),
                  pl.BlockSpec((k, tn), lambda j, i: (0, ca + j)),
                  pl.BlockSpec((k, tn), lambda j, i: (0, cg + j))],
        out_specs=pl.BlockSpec((tm, tn), lambda j, i: (i, j)),
        out_shape=jax.ShapeDtypeStruct((m, n), out_dtype),
        scratch_shapes=[pltpu.VMEM((k, tn), BF16), pltpu.VMEM((k, tn), BF16)],
        compiler_params=_cparams(("parallel", "arbitrary")),
        name="matmul_glu",
    )(x, w, w)


def _mm_res_kernel(n_lhs, split, *refs):
    xs = refs[:n_lhs]
    ws = refs[n_lhs:2 * n_lhs]
    n_res = 1 if split is None else 2
    res_refs = refs[2 * n_lhs:2 * n_lhs + n_res]
    gate_ref, o_ref = refs[2 * n_lhs + n_res:2 * n_lhs + n_res + 2]
    wbs = refs[2 * n_lhs + n_res + 2:]
    _stage_weights(ws, wbs)
    acc = jnp.dot(xs[0][...], wbs[0][...], preferred_element_type=F32)
    for x_ref, wb_ref in zip(xs[1:], wbs[1:]):
        acc = acc + jnp.dot(x_ref[...], wb_ref[...], preferred_element_type=F32)
    if split is None:
        res = res_refs[0][...]
    else:
        res = jnp.where(split.tile_is_ctx(pl.program_id(1)), res_refs[0][...], res_refs[1][...])
    o_ref[...] = res + gate_ref[...] * acc


def matmul_residual(lhs, w, res, mods, layer, m_gate, n_rows, n_tiles, res_block_fn, row_fn,
                    tn=1024):
    n_lhs = len(lhs)
    n = w.shape[1]
    widths = [a.shape[1] for a in lhs]
    assert all(wd == widths[0] for wd in widths) and sum(widths) == w.shape[0]
    in_specs = [pl.BlockSpec((ROW_TILE, wd), lambda j, i: (i, 0)) for wd in widths]
    in_specs += [pl.BlockSpec((wd, tn), functools.partial(lambda q, j, i: (q, j), q),
                              pipeline_mode=pl.Buffered(1))
                 for q, wd in enumerate(widths)]
    split = None
    if isinstance(res, tuple):
        rc, rl, split = res
        in_specs += [pl.BlockSpec((ROW_TILE, tn), lambda j, i: (split.ctx_src_block(i), j)),
                     pl.BlockSpec((ROW_TILE, tn), lambda j, i: (split.lat_src_block(i), j))]
        res_args = [rc, rl]
    else:
        in_specs.append(pl.BlockSpec((ROW_TILE, tn), lambda j, i: (res_block_fn(i), j)))
        res_args = [res]
    in_specs.append(
        pl.BlockSpec((None, 1, tn), _mod_spec(layer, m_gate, row_fn, n_rows, 2, 1, col_axis=0)))
    return pl.pallas_call(
        functools.partial(_mm_res_kernel, n_lhs, split),
        grid=(n // tn, n_tiles),
        in_specs=in_specs,
        out_specs=pl.BlockSpec((ROW_TILE, tn), lambda j, i: (i, j)),
        out_shape=jax.ShapeDtypeStruct((n_tiles * ROW_TILE, n), F32),
        scratch_shapes=[pltpu.VMEM((wd, tn), BF16) for wd in widths],
        compiler_params=_cparams(("parallel", "arbitrary")),
        name="matmul_residual",
    )(*lhs, *([w] * n_lhs), *res_args, mods)


def _conv_kernel(width, mode, per_b, ctx_t, x_ref, prev_ref, next_ref, w_ref, p0_ref, p1_ref,
                 o_ref, xp_ref, acc_ref, sh_ref):
    t = pl.program_id(0) % per_b
    first = (t == 0) | (t == ctx_t)
    last = (t == ctx_t - 1) | (t == per_b - 1)
    c = x_ref.shape[1]
    xp_ref[0:HALO, :] = jnp.where(first, 0.0, prev_ref[...])
    xp_ref[HALO:HALO + ROW_TILE, :] = x_ref[...]
    xp_ref[HALO + ROW_TILE:, :] = jnp.where(last, 0.0, next_ref[...])
    off = HALO - width // 2
    sub = SUBLANES
    taps_of = [[k for k in range(width) if (off + k) % sub == ph] for ph in range(sub)]
    span = ROW_TILE + sub * ((off + width - 1) // sub)

    def chunk(j, carry):
        cs = pl.ds(pl.multiple_of(j * LANES, LANES), LANES)
        for ph in range(1, sub):
            if taps_of[ph]:
                sh_ref[ph, 0:span, :] = xp_ref[ph:ph + span, cs]
        acc = jnp.zeros((ROW_TILE, LANES), F32)
        for ph in range(sub):
            for k in taps_of[ph]:
                lo = off + k - ph
                if ph == 0:
                    win = xp_ref[lo:lo + ROW_TILE, cs]
                else:
                    win = sh_ref[ph, lo:lo + ROW_TILE, :]
                acc = acc + win * w_ref[k:k + 1, cs]
        acc_ref[:, cs] = acc
        return carry

    lax.fori_loop(0, c // LANES, chunk, 0)
    y = acc_ref[...]
    if mode == "layernorm_silu":
        mu = jnp.mean(y, axis=-1, keepdims=True)
        yc = y - mu
        var = jnp.mean(yc * yc, axis=-1, keepdims=True)
        y = yc * lax.rsqrt(var + EPS) * p0_ref[...] + p1_ref[...]
    else:
        y = y + p0_ref[...]
    o_ref[...] = _silu(y).astype(o_ref.dtype)


def seq_conv(x, w, p0, p1, mode, geom, out_dtype=BF16):
    rows, c = x.shape
    width = w.shape[0]
    hb = ROW_TILE // HALO
    n_halo = rows // HALO
    kern = functools.partial(_conv_kernel, width, mode, geom.per_b, geom.ctx_t)
    return pl.pallas_call(
        kern,
        grid=(geom.n_all,),
        in_specs=[
            pl.BlockSpec((ROW_TILE, c), lambda i: (i, 0)),
            pl.BlockSpec((HALO, c), lambda i: (jnp.maximum(i * hb - 1, 0), 0)),
            pl.BlockSpec((HALO, c), lambda i: (jnp.minimum((i + 1) * hb, n_halo - 1), 0)),
            pl.BlockSpec((width, c), lambda i: (0, 0)),
            pl.BlockSpec((1, c), lambda i: (0, 0)),
            pl.BlockSpec((1, c), lambda i: (0, 0)),
        ],
        out_specs=pl.BlockSpec((ROW_TILE, c), lambda i: (i, 0)),
        out_shape=jax.ShapeDtypeStruct((rows, c), out_dtype),
        scratch_shapes=[pltpu.VMEM((ROW_TILE + 2 * HALO, c), F32),
                        pltpu.VMEM((ROW_TILE, c), F32),
                        pltpu.VMEM((SUBLANES, ROW_TILE + 2 * HALO, LANES), F32)],
        compiler_params=_cparams(("parallel",)),
        name="seq_conv_" + mode,
    )(x, x, x, w, p0.reshape(1, c), p1.reshape(1, c))


def _ssd_kernel(n_chunks, ctx_chunks, xs_ref, b_ref, c_ref, dtraw_ref, bias_ref, aneg_ref,
                y_ref, state_ref, tr_ref, row_ref, col_ref, cb_ref, bt_ref):
    T = SSM_CHUNK
    rows = lax.broadcasted_iota(jnp.int32, (T, T), 0)
    cols = lax.broadcasted_iota(jnp.int32, (T, T), 1)
    lane = lax.broadcasted_iota(jnp.int32, (T, T), 1)
    left = lane < SSM_HEAD_DIM
    state_ref[...] = jnp.zeros_like(state_ref)
    tr_ref[...] = jnp.zeros_like(tr_ref)
    n_heads = 4

    def row_of(d, s):
        s = jnp.minimum(s, n_chunks - 1)
        if d == 1:
            s = jnp.where(s < ctx_chunks, ctx_chunks - 1 - s, n_chunks - 1 + ctx_chunks - s)
        return pl.multiple_of(s * T, T)

    def prep(d, s, slot):
        diff = rows - cols if d == 0 else cols - rows
        tri = jnp.where(diff <= 0, 1.0, 0.0).astype(F32)
        r0 = row_of(d, s)
        raw = dtraw_ref[d, :, pl.ds(r0, T)]
        z = raw + bias_ref[d]
        dt_row = jnp.maximum(z, 0.0) + jnp.log1p(jnp.exp(-jnp.abs(z)))
        da_row = dt_row * aneg_ref[d]
        cum_row = jnp.dot(da_row, tri, preferred_element_type=F32,
                          precision=lax.Precision.HIGHEST)
        tot_row = jnp.sum(da_row, axis=-1, keepdims=True)
        tr_ref[d, 0:8, :] = cum_row
        row_ref[slot, d, 0:8, :] = dt_row
        row_ref[slot, d, 8:16, :] = cum_row
        row_ref[slot, d, 16:24, :] = jnp.broadcast_to(tot_row, (8, T))
        col_ref[slot, d] = tr_ref[d].T
        bmat = b_ref[pl.ds(r0, T), :]
        cmat = c_ref[pl.ds(r0, T), :]
        cb_ref[slot, d] = lax.dot_general(cmat, bmat, (((1,), (1,)), ((), ())),
                                          preferred_element_type=F32)
        bt_ref[slot, d] = bmat.astype(F32).T

    def main(d, s, slot):
        diff = rows - cols if d == 0 else cols - rows
        keep = diff >= 0
        r0 = row_of(d, s)
        dt_row = row_ref[slot, d, 0:8, :]
        cum_row = row_ref[slot, d, 8:16, :]
        tot_row = row_ref[slot, d, 16:24, 0:1]
        colform = col_ref[slot, d]
        cb = cb_ref[slot, d]
        bt = bt_ref[slot, d]
        cmat = c_ref[pl.ds(r0, T), :]
        for pair in range(n_heads // 2):
            xs_pair = xs_ref[pl.ds(r0, T), pair * LANES:(pair + 1) * LANES]
            ms, bds, ecs, ets = [], [], [], []
            for q in range(2):
                r = pair * 2 + q
                cum_c = jnp.broadcast_to(colform[:, r:r + 1], (T, T))
                cum_r = cum_row[r:r + 1, :]
                dt_r = dt_row[r:r + 1, :]
                seg = jnp.where(keep, cum_c - cum_r, -jnp.inf)
                ms.append((cb * jnp.exp(seg) * dt_r).astype(BF16))
                dec_r = jnp.exp(tot_row[r:r + 1, :] - cum_r) * dt_r
                bds.append((bt * dec_r).astype(BF16))
                ecs.append(jnp.exp(cum_c))
                ets.append(jnp.exp(tot_row[r:r + 1, :]))
            xs32 = xs_pair.astype(F32)
            xd2 = jnp.concatenate([jnp.where(left, xs32, 0.0).astype(BF16),
                                   jnp.where(left, 0.0, xs32).astype(BF16)], axis=0)
            m2 = jnp.concatenate(ms, axis=1)
            y_diag = jnp.dot(m2, xd2, preferred_element_type=F32)
            st = state_ref[d, pair]
            y_off = jnp.dot(cmat, st.astype(BF16), preferred_element_type=F32)
            y_off = y_off * jnp.where(left, ecs[0], ecs[1])
            y_ref[d, pl.ds(r0, T), pair * LANES:(pair + 1) * LANES] = (y_diag + y_off).astype(y_ref.dtype)
            bd2 = jnp.concatenate(bds, axis=1)
            upd = jnp.dot(bd2, xd2, preferred_element_type=F32)
            state_ref[d, pair] = st * jnp.where(left[0:1, :], ets[0], ets[1]) + upd

    def two_steps(i, carry):
        for k in range(2):
            for d in range(2):
                main(d, 2 * i + k, k)
            for d in range(2):
                prep(d, 2 * i + k + 1, 1 - k)
        return carry

    assert n_chunks % 2 == 0
    for d in range(2):
        prep(d, 0, 0)
    lax.fori_loop(0, n_chunks // 2, two_steps, 0)


def ssd_scan(xbc, dtraw_rows, bias_rows, aneg_rows, geom):
    rows = xbc.shape[0]
    per = geom.lc + geom.n_lat
    n_chunks = per // SSM_CHUNK
    ctx_chunks = geom.lc // SSM_CHUNK
    ssm_d = SSM_HEADS * SSM_HEAD_DIM
    gw = 4 * SSM_HEAD_DIM
    xs_blocks = ssm_d // gw
    kern = functools.partial(_ssd_kernel, n_chunks, ctx_chunks)
    return pl.pallas_call(
        kern,
        grid=(geom.bsz, SSM_GROUPS),
        in_specs=[
            pl.BlockSpec((per, gw), lambda b, g: (b, g)),
            pl.BlockSpec((per, SSM_STATE), lambda b, g: (b, 2 * xs_blocks + g)),
            pl.BlockSpec((per, SSM_STATE), lambda b, g: (b, 2 * xs_blocks + SSM_GROUPS + g)),
            pl.BlockSpec((None, 2, None, 8, per), lambda b, g: (b, 0, g, 0, 0)),
            pl.BlockSpec((2, None, 8, LANES), lambda b, g: (0, g, 0, 0)),
            pl.BlockSpec((2, None, 8, LANES), lambda b, g: (0, g, 0, 0)),
        ],
        out_specs=pl.BlockSpec((2, per, gw), lambda b, g: (0, b, g)),
        out_shape=jax.ShapeDtypeStruct((2, rows, ssm_d), BF16),
        scratch_shapes=[pltpu.VMEM((2, 2, SSM_STATE, LANES), F32),
                        pltpu.VMEM((2, LANES, SSM_CHUNK), F32),
                        pltpu.VMEM((2, 2, 24, SSM_CHUNK), F32),
                        pltpu.VMEM((2, 2, SSM_CHUNK, LANES), F32),
                        pltpu.VMEM((2, 2, SSM_CHUNK, SSM_CHUNK), F32),
                        pltpu.VMEM((2, 2, SSM_STATE, SSM_CHUNK), F32)],
        compiler_params=_cparams(("parallel", "parallel")),
        name="ssd_scan",
    )(xbc, xbc, xbc, dtraw_rows, bias_rows, aneg_rows)


def _ssd_merge_kernel(yf_ref, yb_ref, xs_ref, z_ref, d_ref, g_ref, o_ref):
    y = d_ref[...] * xs_ref[...].astype(F32) + yf_ref[...].astype(F32) + yb_ref[...].astype(F32)
    gated = y * _silu(z_ref[...].astype(F32))
    ms = jnp.mean(gated * gated, axis=-1, keepdims=True)
    o_ref[...] = (gated * lax.rsqrt(ms + EPS) * g_ref[...]).astype(o_ref.dtype)


def ssd_merge(y2, xbc, z, d_full, g, geom):
    rows, ssm_d = z.shape
    spec = pl.BlockSpec((ROW_TILE, ssm_d), lambda i: (i, 0))
    vec = pl.BlockSpec((1, ssm_d), lambda i: (0, 0))
    return pl.pallas_call(
        _ssd_merge_kernel,
        grid=(geom.n_all,),
        in_specs=[pl.BlockSpec((None, ROW_TILE, ssm_d), lambda i: (0, i, 0)),
                  pl.BlockSpec((None, ROW_TILE, ssm_d), lambda i: (1, i, 0)),
                  spec, spec, vec, vec],
        out_specs=spec,
        out_shape=jax.ShapeDtypeStruct((rows, ssm_d), BF16),
        compiler_params=_cparams(("parallel",)),
        name="ssd_merge",
    )(y2, y2, xbc, z, d_full.reshape(1, ssm_d), g.reshape(1, ssm_d))


def _qk_prep_kernel(n_q, n_k, scale, qkv_ref, cos_ref, sin_ref, gq_ref, gk_ref, q_ref, k_ref):
    cosf = cos_ref[...]
    sinf = sin_ref[...]
    hd = ATT_HEAD_DIM
    for h in range(n_q + n_k):
        x = qkv_ref[:, h * hd:(h + 1) * hd].astype(F32)
        g = gq_ref[...] if h < n_q else gk_ref[...]
        ms = jnp.mean(x * x, axis=-1, keepdims=True)
        xn = x * lax.rsqrt(ms + EPS) * g
        rot = pltpu.roll(xn, hd // 2, axis=1)
        o = xn * cosf + rot * sinf
        if h < n_q:
            q_ref[:, h * hd:(h + 1) * hd] = (o * scale).astype(q_ref.dtype)
        else:
            k_ref[:, (h - n_q) * hd:(h - n_q + 1) * hd] = o.astype(k_ref.dtype)


def qk_prep(qkv, cosf, sinf, gq, gk, geom):
    rows = qkv.shape[0]
    hd = ATT_HEAD_DIM
    qc, kc = ATT_HEADS * hd, ATT_KV_HEADS * hd
    kern = functools.partial(_qk_prep_kernel, ATT_HEADS, ATT_KV_HEADS, hd ** -0.5)
    per_b = geom.per_b
    return pl.pallas_call(
        kern,
        grid=(geom.n_all,),
        in_specs=[pl.BlockSpec((ROW_TILE, qc + kc), lambda i: (i, 0)),
                  pl.BlockSpec((ROW_TILE, hd), lambda i: (i % per_b, 0)),
                  pl.BlockSpec((ROW_TILE, hd), lambda i: (i % per_b, 0)),
                  pl.BlockSpec((1, hd), lambda i: (0, 0)),
                  pl.BlockSpec((1, hd), lambda i: (0, 0))],
        out_specs=[pl.BlockSpec((ROW_TILE, qc), lambda i: (i, 0)),
                   pl.BlockSpec((ROW_TILE, kc), lambda i: (i, 0))],
        out_shape=[jax.ShapeDtypeStruct((rows, qc), BF16),
                   jax.ShapeDtypeStruct((rows, kc), BF16)],
        compiler_params=_cparams(("parallel",)),
        name="qk_prep",
    )(qkv, cosf, sinf, gq.reshape(1, hd), gk.reshape(1, hd))


def _attn_kernel(rep, q_ref, k_ref, v_ref, o_ref):
    hd = ATT_HEAD_DIM
    k = k_ref[...]
    v = v_ref[...]
    v1 = jnp.concatenate([v, jnp.ones_like(v)], axis=1)
    for r in range(rep):
        q = q_ref[:, r * hd:(r + 1) * hd]
        s = lax.dot_general(q, k, (((1,), (1,)), ((), ())), preferred_element_type=F32)
        m = jnp.max(s, axis=-1, keepdims=True)
        p = jnp.exp((s - m).astype(BF16))
        o = jnp.dot(p, v1, preferred_element_type=F32)
        o_ref[:, r * hd:(r + 1) * hd] = (o[:, :hd] / o[:, hd:hd + 1]).astype(o_ref.dtype)


def attention(q, k, qkv, geom):
    hd = ATT_HEAD_DIM
    rep = ATT_HEADS // ATT_KV_HEADS
    per = geom.lc + geom.n_lat
    v_col0 = (ATT_HEADS + ATT_KV_HEADS)
    lat_t, per_b, ctx_t = geom.lat_t, geom.per_b, geom.ctx_t
    return pl.pallas_call(
        functools.partial(_attn_kernel, rep),
        grid=(geom.bsz, ATT_KV_HEADS, lat_t),
        in_specs=[pl.BlockSpec((ROW_TILE, rep * hd), lambda b, g, t: (b * per_b + ctx_t + t, g)),
                  pl.BlockSpec((per, hd), lambda b, g, t: (b, g)),
                  pl.BlockSpec((per, hd), lambda b, g, t: (b, v_col0 + g))],
        out_specs=pl.BlockSpec((ROW_TILE, rep * hd), lambda b, g, t: (b * lat_t + t, g)),
        out_shape=jax.ShapeDtypeStruct((geom.bsz * geom.n_lat, ATT_HEADS * hd), BF16),
        compiler_params=_cparams(("parallel", "parallel", "parallel")),
        name="attention",
    )(q, k, qkv)


def _start_row_gather(ids_ref, first, stride, src_ref, dst_ref, sem, inline=False):
    def issue(r, carry):
        row = ids_ref[first + stride * r]
        pltpu.make_async_copy(src_ref.at[pl.ds(row, 1)], dst_ref.at[pl.ds(r, 1)], sem).start()
        return carry

    if inline:
        for r in range(ROW_TILE):
            issue(r, 0)
    else:
        lax.fori_loop(0, ROW_TILE, issue, 0, unroll=8)


def _wait_row_gather(src_ref, dst_ref, sem):
    pltpu.make_async_copy(src_ref.at[pl.ds(0, ROW_TILE)], dst_ref, sem).wait()


def _cast_kernel(x_ref, o_ref):
    o_ref[...] = x_ref[...].astype(o_ref.dtype)


def cast_experts(w_all, layer, dtype=BF16):
    depth, g, e, k, n = w_all.shape
    ne = g * e
    return pl.pallas_call(
        _cast_kernel,
        grid=(ne,),
        in_specs=[pl.BlockSpec((None, k, n), lambda i: (layer * ne + i, 0, 0))],
        out_specs=pl.BlockSpec((None, k, n), lambda i: (i, 0, 0)),
        out_shape=jax.ShapeDtypeStruct((ne, k, n), dtype),
        compiler_params=_cparams(("parallel",)),
        name="cast_experts",
    )(w_all.reshape(depth * ne, k, n))


def _expert_mlp_kernel(te_ref, nt_ref, ids_ref, h_ref, cw_ref, w1_ref, w3_ref, w2_ref, o_ref,
                       xbuf_ref, sem):
    t = pl.program_id(0)
    nt = nt_ref[0]
    slot = t % 2

    @pl.when(t == 0)
    def _():
        _start_row_gather(ids_ref, 0, 1, h_ref, xbuf_ref.at[0], sem.at[0])

    @pl.when(t < nt)
    def _():
        _wait_row_gather(h_ref, xbuf_ref.at[slot], sem.at[slot])
        nxt = jnp.minimum(t + 1, nt - 1)
        _start_row_gather(ids_ref, nxt * ROW_TILE, 1, h_ref, xbuf_ref.at[1 - slot],
                          sem.at[1 - slot], inline=True)
        x = xbuf_ref[slot].astype(BF16)
        a = jnp.dot(x, w1_ref[...], preferred_element_type=F32)
        b = jnp.dot(x, w3_ref[...], preferred_element_type=F32)
        hid = _silu(a) * b * cw_ref[...]
        y = jnp.dot(hid.astype(BF16), w2_ref[...], preferred_element_type=F32)
        o_ref[...] = _to_row_major(y)

        @pl.when(t + 1 == nt)
        def _():
            _wait_row_gather(h_ref, xbuf_ref.at[1 - slot], sem.at[1 - slot])

    @pl.when(t >= nt)
    def _():
        o_ref[...] = jnp.zeros_like(o_ref)


def expert_mlp(h, row_ids, cw, w1, w3, w2, tile_expert, n_tiles_used):
    p = row_ids.shape[0]
    d = h.shape[1]
    s, l = d // LANES, LANES
    hdim = w1.shape[2]
    n_tiles = p // ROW_TILE

    def wmap(t, te, nt, ids):
        return (te[t], 0, 0)

    return pl.pallas_call(
        _expert_mlp_kernel,
        grid_spec=pltpu.PrefetchScalarGridSpec(
            num_scalar_prefetch=3,
            grid=(n_tiles,),
            in_specs=[pl.BlockSpec(memory_space=pl.ANY),
                      pl.BlockSpec((ROW_TILE, 1), lambda t, te, nt, ids: (t, 0)),
                      pl.BlockSpec((None, d, hdim), wmap),
                      pl.BlockSpec((None, d, hdim), wmap),
                      pl.BlockSpec((None, hdim, d), wmap)],
            out_specs=pl.BlockSpec((ROW_TILE, s, l), lambda t, te, nt, ids: (t, 0, 0)),
            scratch_shapes=[pltpu.VMEM((2, ROW_TILE, d), F32), pltpu.SemaphoreType.DMA((2,))],
        ),
        out_shape=jax.ShapeDtypeStruct((p, s, l), F32),
        compiler_params=_cparams(("arbitrary",)),
        name="expert_mlp",
    )(tile_expert, n_tiles_used, row_ids, h, cw, w1, w3, w2)


def _moe_combine_kernel(final_norm, pos_ref, ys_ref, res_ref, gate_ref, g_ref, o_ref, buf_ref, sem):
    i = pl.program_id(0)
    n = pl.num_programs(0)
    slot = i % 2

    def start(tile, s):
        for q in range(2):
            _start_row_gather(pos_ref, tile * (2 * ROW_TILE) + q, 2, ys_ref, buf_ref.at[s, q],
                              sem.at[s, q])

    @pl.when(i == 0)
    def _():
        start(0, 0)

    @pl.when(i + 1 < n)
    def _():
        start(i + 1, 1 - slot)

    for q in range(2):
        _wait_row_gather(ys_ref, buf_ref.at[slot, q], sem.at[slot, q])
    y = _from_row_major(buf_ref[slot, 0] + buf_ref[slot, 1], F32)
    x = res_ref[...] + gate_ref[...] * y
    if final_norm:
        ms = jnp.mean(x * x, axis=-1, keepdims=True)
        x = x * lax.rsqrt(ms + EPS) * g_ref[...]
    o_ref[...] = x


def moe_combine(ys, pos, res, mods, layer, n_rows, n_tiles, row_fn, final_gain=None):
    d = res.shape[1]
    _, s, l = ys.shape
    tile = pl.BlockSpec((ROW_TILE, d), lambda i, pos: (i, 0))
    mod_map = _mod_spec(layer, 5, row_fn, n_rows, 1, 0)
    gain = jnp.ones((d,), F32) if final_gain is None else final_gain
    return pl.pallas_call(
        functools.partial(_moe_combine_kernel, final_gain is not None),
        grid_spec=pltpu.PrefetchScalarGridSpec(
            num_scalar_prefetch=1,
            grid=(n_tiles,),
            in_specs=[pl.BlockSpec(memory_space=pl.ANY),
                      tile,
                      pl.BlockSpec((None, 1, d), lambda i, pos: mod_map(i)),
                      pl.BlockSpec((1, d), lambda i, pos: (0, 0))],
            out_specs=tile,
            scratch_shapes=[pltpu.VMEM((2, 2, ROW_TILE, s, l), F32),
                            pltpu.SemaphoreType.DMA((2, 2))],
        ),
        out_shape=jax.ShapeDtypeStruct((n_tiles * ROW_TILE, d), F32),
        compiler_params=_cparams(("arbitrary",)),
        name="moe_combine",
    )(pos.reshape(-1), ys, res, mods, gain.reshape(1, d))


def _dispatch_plan(route):
    t = route.shape[0]
    ids = route[:, 0:2].astype(jnp.int32)
    wts = route[:, 2:4]
    flat = ids.reshape(-1)
    n_assign = flat.shape[0]
    p_max = n_assign + N_EXPERTS * ROW_TILE
    p_max = (p_max // ROW_TILE) * ROW_TILE
    onehot = (flat[:, None] == jnp.arange(N_EXPERTS, dtype=jnp.int32)[None, :]).astype(jnp.int32)
    counts = jnp.sum(onehot, axis=0)
    padded = ((counts + ROW_TILE - 1) // ROW_TILE) * ROW_TILE
    ends = jnp.cumsum(padded)
    offs = ends - padded
    starts = jnp.cumsum(counts) - counts
    order = jnp.argsort(flat, stable=True).astype(jnp.int32)
    n_tiles = p_max // ROW_TILE
    tile_first = jnp.arange(n_tiles, dtype=jnp.int32) * ROW_TILE
    tile_e = jnp.sum((tile_first[:, None] >= ends[None, :]).astype(jnp.int32), axis=1)
    tile_e = jnp.minimum(tile_e, N_EXPERTS - 1)
    tile_rank = tile_first - offs[tile_e]
    within = jnp.arange(ROW_TILE, dtype=jnp.int32)[None, :]
    rank = tile_rank[:, None] + within
    valid = (rank < counts[tile_e][:, None]) & (tile_first[:, None] < ends[-1])
    src_sorted = jnp.clip(starts[tile_e][:, None] + rank, 0, n_assign - 1).reshape(-1)
    valid = valid.reshape(-1)
    assign = order[src_sorted]
    row_ids = jnp.where(valid, assign // 2, 0).astype(jnp.int32)
    cw = jnp.where(valid, wts.reshape(-1)[assign], 0.0).astype(F32)
    sorted_pos = jnp.argsort(order).astype(jnp.int32)
    shift = jnp.sum(onehot * (offs - starts)[None, :], axis=1)
    pos = (sorted_pos + shift).astype(jnp.int32).reshape(t, 2)
    n_tiles_used = (ends[-1] // ROW_TILE).astype(jnp.int32).reshape(1)
    last_e = tile_e[jnp.maximum(n_tiles_used[0] - 1, 0)]
    tile_idx = jnp.arange(n_tiles, dtype=jnp.int32)
    tile_expert = jnp.where(tile_idx < n_tiles_used[0], tile_e, last_e).astype(jnp.int32)
    return row_ids, cw, pos, tile_expert, n_tiles_used


def hier_moe_block(h, route, w1, w3, w2):
    row_ids, cw, pos, tile_expert, n_tiles_used = _dispatch_plan(route)
    ys = expert_mlp(h, row_ids, cw[:, None], w1, w3, w2, tile_expert, n_tiles_used)
    return ys, pos


def _rope_tables(geom):
    n_lat = geom.n_lat
    half = ATT_HEAD_DIM // 2
    nfreq = half // 2
    row = jnp.repeat(jnp.arange(n_lat // GRID_W, dtype=F32), GRID_W)
    col = (jnp.arange(n_lat) % GRID_W).astype(F32)
    inv = ROPE_THETA ** (-jnp.arange(nfreq, dtype=F32) / nfreq)
    ang = jnp.concatenate([row[:, None] * inv, col[:, None] * inv], axis=-1)
    cos, sin = jnp.cos(ang), jnp.sin(ang)
    cosf = jnp.concatenate([cos, cos], axis=-1)
    sinf = jnp.concatenate([-sin, sin], axis=-1)
    cosf = jnp.concatenate([jnp.ones((geom.lc, ATT_HEAD_DIM), F32), cosf], axis=0)
    sinf = jnp.concatenate([jnp.zeros((geom.lc, ATT_HEAD_DIM), F32), sinf], axis=0)
    return cosf, sinf


def _router_weights(router_g, router_e):
    d = router_g.shape[0]
    pad = jnp.zeros((d, LANES - MOE_GROUPS - N_EXPERTS), F32)
    return jnp.concatenate([router_g, router_e, pad], axis=1)


def mixer_even(xs, mods, layer, j, n_rows, geom, norm_g, w_in, conv_dw, conv_ln_g, conv_ln_b,
               ssm_conv_w, ssm_conv_b, ssm_dt_bias, ssm_a_log, ssm_d, ssm_norm_g, w_out,
               debug=False):
    bsz, per = geom.bsz, geom.lc + geom.n_lat
    conv_d = conv_dw.shape[1]
    ssm_dd = SSM_HEADS * SSM_HEAD_DIM
    xbc_dim = ssm_conv_w.shape[1]
    h = norm_mod(xs, norm_g, mods, layer, 0, n_rows, geom.n_all, geom.all_block, geom.all_mod_row)
    c0, c1, c2, c3 = conv_d, 2 * conv_d, 2 * conv_d + ssm_dd, 2 * conv_d + ssm_dd + xbc_dim
    glu = matmul_glu(h, w_in, 0, c0, conv_d, F32)
    z = matmul(h, w_in, c1, ssm_dd, BF16)
    xbc_raw = matmul(h, w_in, c2, xbc_dim, F32)
    w_dt = jnp.pad(w_in[:, c3:], ((0, 0), (0, LANES - 2 * SSM_HEADS)))
    dt_raw = matmul(h, w_dt, 0, LANES, F32)[:, :2 * SSM_HEADS]
    conf = seq_conv(glu, conv_dw, conv_ln_g, conv_ln_b, "layernorm_silu", geom)
    xbc = seq_conv(xbc_raw, ssm_conv_w, ssm_conv_b, ssm_conv_b, "bias_silu", geom)
    dtr = dt_raw.reshape(bsz, per, 2, SSM_GROUPS, 4).transpose(0, 2, 3, 4, 1)
    dtr = jnp.pad(dtr, ((0, 0), (0, 0), (0, 0), (0, 4), (0, 0)))

    def head_rows(v):
        v = v.astype(F32).reshape(2, SSM_GROUPS, 4)
        v = jnp.pad(v, ((0, 0), (0, 0), (0, 4)))
        return jnp.broadcast_to(v[..., None], (2, SSM_GROUPS, 8, LANES))

    y2 = ssd_scan(xbc, dtr, head_rows(ssm_dt_bias), head_rows(-jnp.exp(ssm_a_log.astype(F32))), geom)
    yn = ssd_merge(y2, xbc, z, jnp.repeat(ssm_d, SSM_HEAD_DIM), ssm_norm_g, geom)
    out = matmul_residual([conf, yn], w_out, xs, mods, layer, 2, n_rows, geom.n_all,
                          geom.all_block, geom.all_mod_row)
    if debug:
        return out, dict(h=h, glu=glu, z=z, xbc_raw=xbc_raw, dt_raw=dt_raw, conf=conf, xbc=xbc,
                         y2=y2, yn=yn)
    return out


def mixer_odd(xs, mods, layer, n_rows, geom, norm_g, w_qkv, q_norm_g, k_norm_g, w_o):
    h = norm_mod(xs, norm_g, mods, layer, 0, n_rows, geom.n_all, geom.all_block, geom.all_mod_row)
    qkv = matmul(h, w_qkv, 0, w_qkv.shape[1], BF16)
    cosf, sinf = _rope_tables(geom)
    q, k = qk_prep(qkv, cosf, sinf, q_norm_g, k_norm_g, geom)
    att = attention(q, k, qkv, geom)
    return matmul_residual([att], w_o, xs, mods, layer, 2, n_rows, geom.n_lat_tiles,
                           geom.lat_block, geom.lat_mod_row)


def moe_layer(xs, mods, layer, n_rows, n_tiles, row_fn, norm_g, router_g, router_e, w1, w3, w2,
              final_gain):
    hm, route = norm_mod(xs, norm_g, mods, layer, 3, n_rows, n_tiles, lambda t: t, row_fn,
                         router_w=_router_weights(router_g, router_e))
    w1 = cast_experts(w1, layer)
    w3 = cast_experts(w3, layer)
    w2 = cast_experts(w2, layer)
    ys, pos = hier_moe_block(hm, route, w1, w3, w2)
    return moe_combine(ys, pos, xs, mods, layer, n_rows, n_tiles, row_fn, final_gain=final_gain)


def kernel(x, c, ctx, c_ctx, w_ada, b_ada, norm_mix, norm_ffn, w_in0, conv_dw, conv_ln_g, conv_ln_b, ssm_conv_w, ssm_conv_b, ssm_dt_bias, ssm_a_log, ssm_d, ssm_norm_g, w_out0, w_qkv, q_norm_g, k_norm_g, w_o, moe_router_g, moe_router_e, moe_w1, moe_w3, moe_w2, norm_final):
    bsz, n_lat, d = x.shape
    lc = ctx.shape[1]
    depth = w_ada.shape[0]
    geom = Geom(bsz, lc, n_lat)
    n_rows = 16

    cvec = jnp.concatenate([c, c_ctx[None, :], jnp.zeros((n_rows - bsz - 1, d), F32)], axis=0)
    mods = ada_mods(cvec, w_ada, b_ada).reshape(depth * n_rows * N_MOD, 1, d)
    xs = (ctx.reshape(bsz * lc, d), x.reshape(bsz * n_lat, d), geom)

    for i in range(depth):
        last = i == depth - 1
        j = i // 2
        if i % 2 == 0:
            if last:
                raise NotImplementedError("a conv/SSD mixer in the last layer is not supported")
            xs = mixer_even(xs, mods, i, j, n_rows, geom, norm_mix[i], w_in0[j], conv_dw[j],
                            conv_ln_g[j], conv_ln_b[j], ssm_conv_w[j], ssm_conv_b[j],
                            ssm_dt_bias[j], ssm_a_log[j], ssm_d[j], ssm_norm_g[j], w_out0[j])
            n_tiles, row_fn = geom.n_all, geom.all_mod_row
        else:
            if not last:
                raise NotImplementedError("an attention mixer before the last layer is not supported")
            xs = mixer_odd(xs, mods, i, n_rows, geom, norm_mix[i], w_qkv[j], q_norm_g[j],
                           k_norm_g[j], w_o[j])
            n_tiles, row_fn = geom.n_lat_tiles, geom.lat_mod_row
        xs = moe_layer(xs, mods, i, n_rows, n_tiles, row_fn, norm_ffn[i], moe_router_g[i],
                       moe_router_e[i], moe_w1, moe_w3, moe_w2, norm_final if last else None)
    return xs.reshape(bsz, n_lat, d)
```

```python
import functools
import math

import jax
import jax.numpy as jnp
from jax import lax
from jax.experimental import pallas as pl
from jax.experimental.pallas import tpu as pltpu

F32 = jnp.float32
BF16 = jnp.bfloat16

EPS = 1e-6
N_MOD = 6
GRID_W = 64
ROPE_THETA = 10000.0
CONV_WIDTH = 31
SSM_CONV_WIDTH = 5
SSM_HEADS = 32
SSM_HEAD_DIM = 64
SSM_GROUPS = 8
SSM_STATE = 128
SSM_CHUNK = 128
ATT_HEADS = 32
ATT_KV_HEADS = 8
ATT_HEAD_DIM = 128
MOE_GROUPS = 4
MOE_EXPERTS = 8
N_EXPERTS = MOE_GROUPS * MOE_EXPERTS

LANES = 128
SUBLANES = 8
ROW_TILE = 256
HALO = 16
VMEM_LIMIT = 56 * 1024 * 1024


def _cparams(sem):
    return pltpu.CompilerParams(dimension_semantics=sem, vmem_limit_bytes=VMEM_LIMIT)


def _sigmoid(v):
    return 0.5 * (jnp.tanh(0.5 * v) + 1.0)


def _silu(v):
    return v * _sigmoid(v)


def _ada_kernel(c_ref, w_ref, b_ref, o_ref):
    s = _silu(c_ref[...]).astype(BF16)
    w = w_ref[...].astype(BF16)
    o_ref[...] = jnp.dot(s, w, preferred_element_type=F32) + b_ref[...]


def ada_mods(cvec, w_ada, b_ada, tn=512):
    depth, d, n = w_ada.shape
    rows = cvec.shape[0]
    return pl.pallas_call(
        _ada_kernel,
        grid=(depth, n // tn),
        in_specs=[
            pl.BlockSpec((rows, d), lambda l, j: (0, 0)),
            pl.BlockSpec((None, d, tn), lambda l, j: (l, 0, j)),
            pl.BlockSpec((None, 1, tn), lambda l, j: (l, 0, j)),
        ],
        out_specs=pl.BlockSpec((None, rows, tn), lambda l, j: (l, 0, j)),
        out_shape=jax.ShapeDtypeStruct((depth, rows, n), F32),
        compiler_params=_cparams(("arbitrary", "arbitrary")),
        name="ada_mods",
    )(cvec, w_ada, b_ada.reshape(depth, 1, n))


class Geom:
    def __init__(self, bsz, lc, n_lat):
        self.bsz, self.lc, self.n_lat = bsz, lc, n_lat
        self.per_b = (lc + n_lat) // ROW_TILE
        self.ctx_t = lc // ROW_TILE
        self.lat_t = n_lat // ROW_TILE
        self.n_all = bsz * self.per_b
        self.n_lat_tiles = bsz * self.lat_t

    def all_block(self, i):
        return i

    def lat_block(self, i):
        return (i // self.lat_t) * self.per_b + self.ctx_t + i % self.lat_t

    def all_mod_row(self, i):
        return jnp.where(i % self.per_b < self.ctx_t, self.bsz, i // self.per_b)

    def lat_mod_row(self, i):
        return i // self.lat_t

    def ctx_src_block(self, i):
        return (i // self.per_b) * self.ctx_t + jnp.minimum(i % self.per_b, self.ctx_t - 1)

    def lat_src_block(self, i):
        return (i // self.per_b) * self.lat_t + jnp.clip(i % self.per_b - self.ctx_t, 0,
                                                         self.lat_t - 1)

    def tile_is_ctx(self, i):
        return i % self.per_b < self.ctx_t


def _mod_spec(layer, m, row_fn, n_rows, grid_rank, tile_axis, width=None, col_axis=None):
    def imap(*g):
        r = row_fn(g[tile_axis])
        c = 0 if col_axis is None else g[col_axis]
        return ((layer * n_rows + r) * N_MOD + m, 0, c)
    return imap


def _norm_mod_kernel(geom, *refs):
    if geom is None:
        x_ref, g_ref, sh_ref, sc_ref, h_ref = refs
        x = x_ref[...]
    else:
        xc_ref, xl_ref, g_ref, sh_ref, sc_ref, h_ref = refs
        x = jnp.where(geom.tile_is_ctx(pl.program_id(0)), xc_ref[...], xl_ref[...])
    ms = jnp.mean(x * x, axis=-1, keepdims=True)
    xn = x * lax.rsqrt(ms + EPS) * g_ref[...]
    h_ref[...] = (xn * (1.0 + sc_ref[...]) + sh_ref[...]).astype(h_ref.dtype)


def _route(logits):
    lane = lax.broadcasted_iota(jnp.int32, logits.shape, 1)
    neg = jnp.float32(-jnp.inf)
    big = jnp.int32(LANES)
    lg = jnp.where(lane < MOE_GROUPS, logits, neg)
    mg = jnp.max(lg, axis=-1, keepdims=True)
    gsel = jnp.min(jnp.where(lg == mg, lane, big), axis=-1, keepdims=True)
    pg = 1.0 / jnp.sum(jnp.exp(lg - mg), axis=-1, keepdims=True)
    lo = MOE_GROUPS + gsel * MOE_EXPERTS
    le = jnp.where((lane >= lo) & (lane < lo + MOE_EXPERTS), logits, neg)
    v1 = jnp.max(le, axis=-1, keepdims=True)
    i1 = jnp.min(jnp.where(le == v1, lane, big), axis=-1, keepdims=True)
    le2 = jnp.where(lane == i1, neg, le)
    v2 = jnp.max(le2, axis=-1, keepdims=True)
    i2 = jnp.min(jnp.where(le2 == v2, lane, big), axis=-1, keepdims=True)
    e2 = jnp.exp(v2 - v1)
    w1 = pg / (1.0 + e2)
    w2 = pg * e2 / (1.0 + e2)
    id1 = (i1 - MOE_GROUPS).astype(F32)
    id2 = (i2 - MOE_GROUPS).astype(F32)
    return jnp.where(lane == 0, id1, jnp.where(lane == 1, id2,
                     jnp.where(lane == 2, w1, jnp.where(lane == 3, w2, 0.0))))


def _to_row_major(x):
    parts = [x[:, s * LANES:(s + 1) * LANES] for s in range(x.shape[1] // LANES)]
    return pltpu.einshape("stl->tsl", jnp.stack(parts, axis=0))


def _from_row_major(x3, dtype):
    xt = pltpu.einshape("tsl->stl", x3)
    return jnp.concatenate([xt[s].astype(dtype) for s in range(x3.shape[1])], axis=1)


def _norm_mod_route_kernel(x_ref, g_ref, sh_ref, sc_ref, wr_ref, h_ref, r_ref):
    x = x_ref[...]
    ms = jnp.mean(x * x, axis=-1, keepdims=True)
    xn = x * lax.rsqrt(ms + EPS) * g_ref[...]
    h = xn * (1.0 + sc_ref[...]) + sh_ref[...]
    h_ref[...] = h
    logits = jnp.dot(h, wr_ref[...], preferred_element_type=F32,
                     precision=lax.Precision.HIGHEST)
    r_ref[...] = _route(logits)


def norm_mod(x, gain, mods, layer, m_shift, n_rows, n_tiles, block_fn, row_fn,
             router_w=None, out_dtype=BF16):
    split = None
    if isinstance(x, tuple):
        xc, xl, split = x
        d = xc.shape[1]
        x_specs = [pl.BlockSpec((ROW_TILE, d), lambda i: (split.ctx_src_block(i), 0)),
                   pl.BlockSpec((ROW_TILE, d), lambda i: (split.lat_src_block(i), 0))]
        x_args = [xc, xl]
        assert router_w is None
    else:
        d = x.shape[1]
        x_specs = [pl.BlockSpec((ROW_TILE, d), lambda i: (block_fn(i), 0))]
        x_args = [x]
    x_spec = x_specs[0]
    g_spec = pl.BlockSpec((1, d), lambda i: (0, 0))
    sh_spec = pl.BlockSpec((None, 1, d), _mod_spec(layer, m_shift, row_fn, n_rows, 1, 0))
    sc_spec = pl.BlockSpec((None, 1, d), _mod_spec(layer, m_shift + 1, row_fn, n_rows, 1, 0))
    o_spec = pl.BlockSpec((ROW_TILE, d), lambda i: (i, 0))
    rows = n_tiles * ROW_TILE
    if router_w is None:
        return pl.pallas_call(
            functools.partial(_norm_mod_kernel, split),
            grid=(n_tiles,),
            in_specs=x_specs + [g_spec, sh_spec, sc_spec],
            out_specs=o_spec,
            out_shape=jax.ShapeDtypeStruct((rows, d), out_dtype),
            compiler_params=_cparams(("parallel",)),
            name="norm_mod",
        )(*x_args, gain.reshape(1, d), mods, mods)
    return pl.pallas_call(
        _norm_mod_route_kernel,
        grid=(n_tiles,),
        in_specs=[x_spec, g_spec, sh_spec, sc_spec,
                  pl.BlockSpec((d, LANES), lambda i: (0, 0))],
        out_specs=[o_spec, pl.BlockSpec((ROW_TILE, LANES), lambda i: (i, 0))],
        out_shape=[jax.ShapeDtypeStruct((rows, d), F32),
                   jax.ShapeDtypeStruct((rows, LANES), F32)],
        compiler_params=_cparams(("parallel",)),
        name="norm_mod_route",
    )(x, gain.reshape(1, d), mods, mods, router_w)


def _stage_weights(w_refs, wb_refs):
    @pl.when(pl.program_id(1) == 0)
    def _():
        for w_ref, wb_ref in zip(w_refs, wb_refs):
            wb_ref[...] = w_ref[...].astype(BF16)


def _mm_kernel(x_ref, w_ref, o_ref, wb_ref):
    _stage_weights([w_ref], [wb_ref])
    o_ref[...] = jnp.dot(x_ref[...], wb_ref[...], preferred_element_type=F32).astype(o_ref.dtype)


def _row_block(m, want=1024):
    tm = want
    while m % tm:
        tm //= 2
    return tm


def matmul(x, w, col0, n, out_dtype, tn=512):
    m, k = x.shape
    tm = _row_block(m)
    tn = min(tn, n)
    cb = col0 // tn
    return pl.pallas_call(
        _mm_kernel,
        grid=(n // tn, m // tm),
        in_specs=[pl.BlockSpec((tm, k), lambda j, i: (i, 0)),
                  pl.BlockSpec((k, tn), lambda j, i: (0, cb + j))],
        out_specs=pl.BlockSpec((tm, tn), lambda j, i: (i, j)),
        out_shape=jax.ShapeDtypeStruct((m, n), out_dtype),
        scratch_shapes=[pltpu.VMEM((k, tn), BF16)],
        compiler_params=_cparams(("parallel", "arbitrary")),
        name="matmul",
    )(x, w)


def _mm_glu_kernel(x_ref, wa_ref, wg_ref, o_ref, wab_ref, wgb_ref):
    _stage_weights([wa_ref, wg_ref], [wab_ref, wgb_ref])
    x = x_ref[...]
    a = jnp.dot(x, wab_ref[...], preferred_element_type=F32)
    b = jnp.dot(x, wgb_ref[...], preferred_element_type=F32)
    o_ref[...] = (a * _sigmoid(b)).astype(o_ref.dtype)


def matmul_glu(x, w, col_a, col_g, n, out_dtype, tn=256):
    m, k = x.shape
    tm = _row_block(m)
    ca, cg = col_a // tn, col_g // tn
    return pl.pallas_call(
        _mm_glu_kernel,
        grid=(n // tn, m // tm),
        in_specs=[pl.BlockSpec((tm, k), lambda j, i: (i, 0)),
                  pl.BlockSpec((k, tn), lambda j, i: (0, ca + j)),
                  pl.BlockSpec((k, tn), lambda j, i: (0, cg + j))],
        out_specs=pl.BlockSpec((tm, tn), lambda j, i: (i, j)),
        out_shape=jax.ShapeDtypeStruct((m, n), out_dtype),
        scratch_shapes=[pltpu.VMEM((k, tn), BF16), pltpu.VMEM((k, tn), BF16)],
        compiler_params=_cparams(("parallel", "arbitrary")),
        name="matmul_glu",
    )(x, w, w)


def _mm_res_kernel(n_lhs, split, *refs):
    xs = refs[:n_lhs]
    ws = refs[n_lhs:2 * n_lhs]
    n_res = 1 if split is None else 2
    res_refs = refs[2 * n_lhs:2 * n_lhs + n_res]
    gate_ref, o_ref = refs[2 * n_lhs + n_res:2 * n_lhs + n_res + 2]
    wbs = refs[2 * n_lhs + n_res + 2:]
    _stage_weights(ws, wbs)
    acc = jnp.dot(xs[0][...], wbs[0][...], preferred_element_type=F32)
    for x_ref, wb_ref in zip(xs[1:], wbs[1:]):
        acc = acc + jnp.dot(x_ref[...], wb_ref[...], preferred_element_type=F32)
    if split is None:
        res = res_refs[0][...]
    else:
        res = jnp.where(split.tile_is_ctx(pl.program_id(1)), res_refs[0][...], res_refs[1][...])
    o_ref[...] = res + gate_ref[...] * acc


def matmul_residual(lhs, w, res, mods, layer, m_gate, n_rows, n_tiles, res_block_fn, row_fn,
                    tn=1024):
    n_lhs = len(lhs)
    n = w.shape[1]
    widths = [a.shape[1] for a in lhs]
    assert all(wd == widths[0] for wd in widths) and sum(widths) == w.shape[0]
    in_specs = [pl.BlockSpec((ROW_TILE, wd), lambda j, i: (i, 0)) for wd in widths]
    in_specs += [pl.BlockSpec((wd, tn), functools.partial(lambda q, j, i: (q, j), q),
                              pipeline_mode=pl.Buffered(1))
                 for q, wd in enumerate(widths)]
    split = None
    if isinstance(res, tuple):
        rc, rl, split = res
        in_specs += [pl.BlockSpec((ROW_TILE, tn), lambda j, i: (split.ctx_src_block(i), j)),
                     pl.BlockSpec((ROW_TILE, tn), lambda j, i: (split.lat_src_block(i), j))]
        res_args = [rc, rl]
    else:
        in_specs.append(pl.BlockSpec((ROW_TILE, tn), lambda j, i: (res_block_fn(i), j)))
        res_args = [res]
    in_specs.append(
        pl.BlockSpec((None, 1, tn), _mod_spec(layer, m_gate, row_fn, n_rows, 2, 1, col_axis=0)))
    return pl.pallas_call(
        functools.partial(_mm_res_kernel, n_lhs, split),
        grid=(n // tn, n_tiles),
        in_specs=in_specs,
        out_specs=pl.BlockSpec((ROW_TILE, tn), lambda j, i: (i, j)),
        out_shape=jax.ShapeDtypeStruct((n_tiles * ROW_TILE, n), F32),
        scratch_shapes=[pltpu.VMEM((wd, tn), BF16) for wd in widths],
        compiler_params=_cparams(("parallel", "arbitrary")),
        name="matmul_residual",
    )(*lhs, *([w] * n_lhs), *res_args, mods)


def _conv_kernel(width, mode, per_b, ctx_t, x_ref, prev_ref, next_ref, w_ref, p0_ref, p1_ref,
                 o_ref, xp_ref, acc_ref, sh_ref):
    t = pl.program_id(0) % per_b
    first = (t == 0) | (t == ctx_t)
    last = (t == ctx_t - 1) | (t == per_b - 1)
    c = x_ref.shape[1]
    xp_ref[0:HALO, :] = jnp.where(first, 0.0, prev_ref[...])
    xp_ref[HALO:HALO + ROW_TILE, :] = x_ref[...]
    xp_ref[HALO + ROW_TILE:, :] = jnp.where(last, 0.0, next_ref[...])
    off = HALO - width // 2
    sub = SUBLANES
    taps_of = [[k for k in range(width) if (off + k) % sub == ph] for ph in range(sub)]
    span = ROW_TILE + sub * ((off + width - 1) // sub)

    def chunk(j, carry):
        cs = pl.ds(pl.multiple_of(j * LANES, LANES), LANES)
        for ph in range(1, sub):
            if taps_of[ph]:
                sh_ref[ph, 0:span, :] = xp_ref[ph:ph + span, cs]
        acc = jnp.zeros((ROW_TILE, LANES), F32)
        for ph in range(sub):
            for k in taps_of[ph]:
                lo = off + k - ph
                if ph == 0:
                    win = xp_ref[lo:lo + ROW_TILE, cs]
                else:
                    win = sh_ref[ph, lo:lo + ROW_TILE, :]
                acc = acc + win * w_ref[k:k + 1, cs]
        acc_ref[:, cs] = acc
        return carry

    lax.fori_loop(0, c // LANES, chunk, 0)
    y = acc_ref[...]
    if mode == "layernorm_silu":
        mu = jnp.mean(y, axis=-1, keepdims=True)
        yc = y - mu
        var = jnp.mean(yc * yc, axis=-1, keepdims=True)
        y = yc * lax.rsqrt(var + EPS) * p0_ref[...] + p1_ref[...]
    else:
        y = y + p0_ref[...]
    o_ref[...] = _silu(y).astype(o_ref.dtype)


def seq_conv(x, w, p0, p1, mode, geom, out_dtype=BF16):
    rows, c = x.shape
    width = w.shape[0]
    hb = ROW_TILE // HALO
    n_halo = rows // HALO
    kern = functools.partial(_conv_kernel, width, mode, geom.per_b, geom.ctx_t)
    return pl.pallas_call(
        kern,
        grid=(geom.n_all,),
        in_specs=[
            pl.BlockSpec((ROW_TILE, c), lambda i: (i, 0)),
            pl.BlockSpec((HALO, c), lambda i: (jnp.maximum(i * hb - 1, 0), 0)),
            pl.BlockSpec((HALO, c), lambda i: (jnp.minimum((i + 1) * hb, n_halo - 1), 0)),
            pl.BlockSpec((width, c), lambda i: (0, 0)),
            pl.BlockSpec((1, c), lambda i: (0, 0)),
            pl.BlockSpec((1, c), lambda i: (0, 0)),
        ],
        out_specs=pl.BlockSpec((ROW_TILE, c), lambda i: (i, 0)),
        out_shape=jax.ShapeDtypeStruct((rows, c), out_dtype),
        scratch_shapes=[pltpu.VMEM((ROW_TILE + 2 * HALO, c), F32),
                        pltpu.VMEM((ROW_TILE, c), F32),
                        pltpu.VMEM((SUBLANES, ROW_TILE + 2 * HALO, LANES), F32)],
        compiler_params=_cparams(("parallel",)),
        name="seq_conv_" + mode,
    )(x, x, x, w, p0.reshape(1, c), p1.reshape(1, c))


def _ssd_kernel(n_chunks, ctx_chunks, xs_ref, b_ref, c_ref, dtraw_ref, bias_ref, aneg_ref,
                y_ref, state_ref, tr_ref, row_ref, col_ref, cb_ref, bt_ref):
    T = SSM_CHUNK
    rows = lax.broadcasted_iota(jnp.int32, (T, T), 0)
    cols = lax.broadcasted_iota(jnp.int32, (T, T), 1)
    lane = lax.broadcasted_iota(jnp.int32, (T, T), 1)
    left = lane < SSM_HEAD_DIM
    state_ref[...] = jnp.zeros_like(state_ref)
    tr_ref[...] = jnp.zeros_like(tr_ref)
    n_heads = 4

    def row_of(d, s):
        s = jnp.minimum(s, n_chunks - 1)
        if d == 1:
            s = jnp.where(s < ctx_chunks, ctx_chunks - 1 - s, n_chunks - 1 + ctx_chunks - s)
        return pl.multiple_of(s * T, T)

    def prep(d, s, slot):
        diff = rows - cols if d == 0 else cols - rows
        tri = jnp.where(diff <= 0, 1.0, 0.0).astype(F32)
        r0 = row_of(d, s)
        raw = dtraw_ref[d, :, pl.ds(r0, T)]
        z = raw + bias_ref[d]
        dt_row = jnp.maximum(z, 0.0) + jnp.log1p(jnp.exp(-jnp.abs(z)))
        da_row = dt_row * aneg_ref[d]
        cum_row = jnp.dot(da_row, tri, preferred_element_type=F32,
                          precision=lax.Precision.HIGHEST)
        tot_row = jnp.sum(da_row, axis=-1, keepdims=True)
        tr_ref[d, 0:8, :] = cum_row
        row_ref[slot, d, 0:8, :] = dt_row
        row_ref[slot, d, 8:16, :] = cum_row
        row_ref[slot, d, 16:24, :] = jnp.broadcast_to(tot_row, (8, T))
        col_ref[slot, d] = tr_ref[d].T
        bmat = b_ref[pl.ds(r0, T), :]
        cmat = c_ref[pl.ds(r0, T), :]
        cb_ref[slot, d] = lax.dot_general(cmat, bmat, (((1,), (1,)), ((), ())),
                                          preferred_element_type=F32)
        bt_ref[slot, d] = bmat.astype(F32).T

    def main(d, s, slot):
        diff = rows - cols if d == 0 else cols - rows
        keep = diff >= 0
        r0 = row_of(d, s)
        dt_row = row_ref[slot, d, 0:8, :]
        cum_row = row_ref[slot, d, 8:16, :]
        tot_row = row_ref[slot, d, 16:24, 0:1]
        colform = col_ref[slot, d]
        cb = cb_ref[slot, d]
        bt = bt_ref[slot, d]
        cmat = c_ref[pl.ds(r0, T), :]
        for pair in range(n_heads // 2):
            xs_pair = xs_ref[pl.ds(r0, T), pair * LANES:(pair + 1) * LANES]
            ms, bds, ecs, ets = [], [], [], []
            for q in range(2):
                r = pair * 2 + q
                cum_c = jnp.broadcast_to(colform[:, r:r + 1], (T, T))
                cum_r = cum_row[r:r + 1, :]
                dt_r = dt_row[r:r + 1, :]
                seg = jnp.where(keep, cum_c - cum_r, -jnp.inf)
                ms.append((cb * jnp.exp(seg) * dt_r).astype(BF16))
                dec_r = jnp.exp(tot_row[r:r + 1, :] - cum_r) * dt_r
                bds.append((bt * dec_r).astype(BF16))
                ecs.append(jnp.exp(cum_c))
                ets.append(jnp.exp(tot_row[r:r + 1, :]))
            xs32 = xs_pair.astype(F32)
            xd2 = jnp.concatenate([jnp.where(left, xs32, 0.0).astype(BF16),
                                   jnp.where(left, 0.0, xs32).astype(BF16)], axis=0)
            m2 = jnp.concatenate(ms, axis=1)
            y_diag = jnp.dot(m2, xd2, preferred_element_type=F32)
            st = state_ref[d, pair]
            y_off = jnp.dot(cmat, st.astype(BF16), preferred_element_type=F32)
            y_off = y_off * jnp.where(left, ecs[0], ecs[1])
            y_ref[d, pl.ds(r0, T), pair * LANES:(pair + 1) * LANES] = (y_diag + y_off).astype(y_ref.dtype)
            bd2 = jnp.concatenate(bds, axis=1)
            upd = jnp.dot(bd2, xd2, preferred_element_type=F32)
            state_ref[d, pair] = st * jnp.where(left[0:1, :], ets[0], ets[1]) + upd

    def two_steps(i, carry):
        for k in range(2):
            for d in range(2):
                main(d, 2 * i + k, k)
            for d in range(2):
                prep(d, 2 * i + k + 1, 1 - k)
        return carry

    assert n_chunks % 2 == 0
    for d in range(2):
        prep(d, 0, 0)
    lax.fori_loop(0, n_chunks // 2, two_steps, 0)


def ssd_scan(xbc, dtraw_rows, bias_rows, aneg_rows, geom):
    rows = xbc.shape[0]
    per = geom.lc + geom.n_lat
    n_chunks = per // SSM_CHUNK
    ctx_chunks = geom.lc // SSM_CHUNK
    ssm_d = SSM_HEADS * SSM_HEAD_DIM
    gw = 4 * SSM_HEAD_DIM
    xs_blocks = ssm_d // gw
    kern = functools.partial(_ssd_kernel, n_chunks, ctx_chunks)
    return pl.pallas_call(
        kern,
        grid=(geom.bsz, SSM_GROUPS),
        in_specs=[
            pl.BlockSpec((per, gw), lambda b, g: (b, g)),
            pl.BlockSpec((per, SSM_STATE), lambda b, g: (b, 2 * xs_blocks + g)),
            pl.BlockSpec((per, SSM_STATE), lambda b, g: (b, 2 * xs_blocks + SSM_GROUPS + g)),
            pl.BlockSpec((None, 2, None, 8, per), lambda b, g: (b, 0, g, 0, 0)),
            pl.BlockSpec((2, None, 8, LANES), lambda b, g: (0, g, 0, 0)),
            pl.BlockSpec((2, None, 8, LANES), lambda b, g: (0, g, 0, 0)),
        ],
        out_specs=pl.BlockSpec((2, per, gw), lambda b, g: (0, b, g)),
        out_shape=jax.ShapeDtypeStruct((2, rows, ssm_d), BF16),
        scratch_shapes=[pltpu.VMEM((2, 2, SSM_STATE, LANES), F32),
                        pltpu.VMEM((2, LANES, SSM_CHUNK), F32),
                        pltpu.VMEM((2, 2, 24, SSM_CHUNK), F32),
                        pltpu.VMEM((2, 2, SSM_CHUNK, LANES), F32),
                        pltpu.VMEM((2, 2, SSM_CHUNK, SSM_CHUNK), F32),
                        pltpu.VMEM((2, 2, SSM_STATE, SSM_CHUNK), F32)],
        compiler_params=_cparams(("parallel", "parallel")),
        name="ssd_scan",
    )(xbc, xbc, xbc, dtraw_rows, bias_rows, aneg_rows)


def _ssd_merge_kernel(yf_ref, yb_ref, xs_ref, z_ref, d_ref, g_ref, o_ref):
    y = d_ref[...] * xs_ref[...].astype(F32) + yf_ref[...].astype(F32) + yb_ref[...].astype(F32)
    gated = y * _silu(z_ref[...].astype(F32))
    ms = jnp.mean(gated * gated, axis=-1, keepdims=True)
    o_ref[...] = (gated * lax.rsqrt(ms + EPS) * g_ref[...]).astype(o_ref.dtype)


def ssd_merge(y2, xbc, z, d_full, g, geom):
    rows, ssm_d = z.shape
    spec = pl.BlockSpec((ROW_TILE, ssm_d), lambda i: (i, 0))
    vec = pl.BlockSpec((1, ssm_d), lambda i: (0, 0))
    return pl.pallas_call(
        _ssd_merge_kernel,
        grid=(geom.n_all,),
        in_specs=[pl.BlockSpec((None, ROW_TILE, ssm_d), lambda i: (0, i, 0)),
                  pl.BlockSpec((None, ROW_TILE, ssm_d), lambda i: (1, i, 0)),
                  spec, spec, vec, vec],
        out_specs=spec,
        out_shape=jax.ShapeDtypeStruct((rows, ssm_d), BF16),
        compiler_params=_cparams(("parallel",)),
        name="ssd_merge",
    )(y2, y2, xbc, z, d_full.reshape(1, ssm_d), g.reshape(1, ssm_d))


def _qk_prep_kernel(n_q, n_k, scale, qkv_ref, cos_ref, sin_ref, gq_ref, gk_ref, q_ref, k_ref):
    cosf = cos_ref[...]
    sinf = sin_ref[...]
    hd = ATT_HEAD_DIM
    for h in range(n_q + n_k):
        x = qkv_ref[:, h * hd:(h + 1) * hd].astype(F32)
        g = gq_ref[...] if h < n_q else gk_ref[...]
        ms = jnp.mean(x * x, axis=-1, keepdims=True)
        xn = x * lax.rsqrt(ms + EPS) * g
        rot = pltpu.roll(xn, hd // 2, axis=1)
        o = xn * cosf + rot * sinf
        if h < n_q:
            q_ref[:, h * hd:(h + 1) * hd] = (o * scale).astype(q_ref.dtype)
        else:
            k_ref[:, (h - n_q) * hd:(h - n_q + 1) * hd] = o.astype(k_ref.dtype)


def qk_prep(qkv, cosf, sinf, gq, gk, geom):
    rows = qkv.shape[0]
    hd = ATT_HEAD_DIM
    qc, kc = ATT_HEADS * hd, ATT_KV_HEADS * hd
    kern = functools.partial(_qk_prep_kernel, ATT_HEADS, ATT_KV_HEADS, hd ** -0.5)
    per_b = geom.per_b
    return pl.pallas_call(
        kern,
        grid=(geom.n_all,),
        in_specs=[pl.BlockSpec((ROW_TILE, qc + kc), lambda i: (i, 0)),
                  pl.BlockSpec((ROW_TILE, hd), lambda i: (i % per_b, 0)),
                  pl.BlockSpec((ROW_TILE, hd), lambda i: (i % per_b, 0)),
                  pl.BlockSpec((1, hd), lambda i: (0, 0)),
                  pl.BlockSpec((1, hd), lambda i: (0, 0))],
        out_specs=[pl.BlockSpec((ROW_TILE, qc), lambda i: (i, 0)),
                   pl.BlockSpec((ROW_TILE, kc), lambda i: (i, 0))],
        out_shape=[jax.ShapeDtypeStruct((rows, qc), BF16),
                   jax.ShapeDtypeStruct((rows, kc), BF16)],
        compiler_params=_cparams(("parallel",)),
        name="qk_prep",
    )(qkv, cosf, sinf, gq.reshape(1, hd), gk.reshape(1, hd))


def _attn_kernel(rep, q_ref, k_ref, v_ref, o_ref):
    hd = ATT_HEAD_DIM
    k = k_ref[...]
    v = v_ref[...]
    v1 = jnp.concatenate([v, jnp.ones_like(v)], axis=1)
    for r in range(rep):
        q = q_ref[:, r * hd:(r + 1) * hd]
        s = lax.dot_general(q, k, (((1,), (1,)), ((), ())), preferred_element_type=F32)
        m = jnp.max(s, axis=-1, keepdims=True)
        p = jnp.exp((s - m).astype(BF16))
        o = jnp.dot(p, v1, preferred_element_type=F32)
        o_ref[:, r * hd:(r + 1) * hd] = (o[:, :hd] / o[:, hd:hd + 1]).astype(o_ref.dtype)


def attention(q, k, qkv, geom):
    hd = ATT_HEAD_DIM
    rep = ATT_HEADS // ATT_KV_HEADS
    per = geom.lc + geom.n_lat
    v_col0 = (ATT_HEADS + ATT_KV_HEADS)
    lat_t, per_b, ctx_t = geom.lat_t, geom.per_b, geom.ctx_t
    return pl.pallas_call(
        functools.partial(_attn_kernel, rep),
        grid=(geom.bsz, ATT_KV_HEADS, lat_t),
        in_specs=[pl.BlockSpec((ROW_TILE, rep * hd), lambda b, g, t: (b * per_b + ctx_t + t, g)),
                  pl.BlockSpec((per, hd), lambda b, g, t: (b, g)),
                  pl.BlockSpec((per, hd), lambda b, g, t: (b, v_col0 + g))],
        out_specs=pl.BlockSpec((ROW_TILE, rep * hd), lambda b, g, t: (b * lat_t + t, g)),
        out_shape=jax.ShapeDtypeStruct((geom.bsz * geom.n_lat, ATT_HEADS * hd), BF16),
        compiler_params=_cparams(("parallel", "parallel", "parallel")),
        name="attention",
    )(q, k, qkv)


def _start_row_gather(ids_ref, first, stride, src_ref, dst_ref, sem, inline=False):
    def issue(r, carry):
        row = ids_ref[first + stride * r]
        pltpu.make_async_copy(src_ref.at[pl.ds(row, 1)], dst_ref.at[pl.ds(r, 1)], sem).start()
        return carry

    if inline:
        for r in range(ROW_TILE):
            issue(r, 0)
    else:
        lax.fori_loop(0, ROW_TILE, issue, 0, unroll=8)


def _wait_row_gather(src_ref, dst_ref, sem):
    pltpu.make_async_copy(src_ref.at[pl.ds(0, ROW_TILE)], dst_ref, sem).wait()


def _cast_kernel(x_ref, o_ref):
    o_ref[...] = x_ref[...].astype(o_ref.dtype)


def cast_experts(w_all, layer, dtype=BF16):
    depth, g, e, k, n = w_all.shape
    ne = g * e
    return pl.pallas_call(
        _cast_kernel,
        grid=(ne,),
        in_specs=[pl.BlockSpec((None, k, n), lambda i: (layer * ne + i, 0, 0))],
        out_specs=pl.BlockSpec((None, k, n), lambda i: (i, 0, 0)),
        out_shape=jax.ShapeDtypeStruct((ne, k, n), dtype),
        compiler_params=_cparams(("parallel",)),
        name="cast_experts",
    )(w_all.reshape(depth * ne, k, n))


def _expert_mlp_kernel(te_ref, nt_ref, ids_ref, h_ref, cw_ref, w1_ref, w3_ref, w2_ref, o_ref,
                       xbuf_ref, sem):
    t = pl.program_id(0)
    nt = nt_ref[0]
    slot = t % 2

    @pl.when(t == 0)
    def _():
        _start_row_gather(ids_ref, 0, 1, h_ref, xbuf_ref.at[0], sem.at[0])

    @pl.when(t < nt)
    def _():
        _wait_row_gather(h_ref, xbuf_ref.at[slot], sem.at[slot])
        nxt = jnp.minimum(t + 1, nt - 1)
        _start_row_gather(ids_ref, nxt * ROW_TILE, 1, h_ref, xbuf_ref.at[1 - slot],
                          sem.at[1 - slot], inline=True)
        x = xbuf_ref[slot].astype(BF16)
        a = jnp.dot(x, w1_ref[...], preferred_element_type=F32)
        b = jnp.dot(x, w3_ref[...], preferred_element_type=F32)
        hid = _silu(a) * b * cw_ref[...]
        y = jnp.dot(hid.astype(BF16), w2_ref[...], preferred_element_type=F32)
        o_ref[...] = _to_row_major(y)

        @pl.when(t + 1 == nt)
        def _():
            _wait_row_gather(h_ref, xbuf_ref.at[1 - slot], sem.at[1 - slot])

    @pl.when(t >= nt)
    def _():
        o_ref[...] = jnp.zeros_like(o_ref)


def expert_mlp(h, row_ids, cw, w1, w3, w2, tile_expert, n_tiles_used):
    p = row_ids.shape[0]
    d = h.shape[1]
    s, l = d // LANES, LANES
    hdim = w1.shape[2]
    n_tiles = p // ROW_TILE

    def wmap(t, te, nt, ids):
        return (te[t], 0, 0)

    return pl.pallas_call(
        _expert_mlp_kernel,
        grid_spec=pltpu.PrefetchScalarGridSpec(
            num_scalar_prefetch=3,
            grid=(n_tiles,),
            in_specs=[pl.BlockSpec(memory_space=pl.ANY),
                      pl.BlockSpec((ROW_TILE, 1), lambda t, te, nt, ids: (t, 0)),
                      pl.BlockSpec((None, d, hdim), wmap),
                      pl.BlockSpec((None, d, hdim), wmap),
                      pl.BlockSpec((None, hdim, d), wmap)],
            out_specs=pl.BlockSpec((ROW_TILE, s, l), lambda t, te, nt, ids: (t, 0, 0)),
            scratch_shapes=[pltpu.VMEM((2, ROW_TILE, d), F32), pltpu.SemaphoreType.DMA((2,))],
        ),
        out_shape=jax.ShapeDtypeStruct((p, s, l), F32),
        compiler_params=_cparams(("arbitrary",)),
        name="expert_mlp",
    )(tile_expert, n_tiles_used, row_ids, h, cw, w1, w3, w2)


def _moe_combine_kernel(mode, pos_ref, ys_ref, res_ref, gate_ref, g_ref, *refs):
    if mode == "next_norm":
        sh_ref, sc_ref, o_ref, h_ref, buf_ref, sem = refs
    else:
        o_ref, buf_ref, sem = refs
    i = pl.program_id(0)
    n = pl.num_programs(0)
    slot = i % 2

    def start(tile, s):
        for q in range(2):
            _start_row_gather(pos_ref, tile * (2 * ROW_TILE) + q, 2, ys_ref, buf_ref.at[s, q],
                              sem.at[s, q])

    @pl.when(i == 0)
    def _():
        start(0, 0)

    @pl.when(i + 1 < n)
    def _():
        start(i + 1, 1 - slot)

    for q in range(2):
        _wait_row_gather(ys_ref, buf_ref.at[slot, q], sem.at[slot, q])
    y = _from_row_major(buf_ref[slot, 0] + buf_ref[slot, 1], F32)
    x = res_ref[...] + gate_ref[...] * y
    if mode == "final_norm":
        ms = jnp.mean(x * x, axis=-1, keepdims=True)
        x = x * lax.rsqrt(ms + EPS) * g_ref[...]
    o_ref[...] = x
    if mode == "next_norm":
        ms = jnp.mean(x * x, axis=-1, keepdims=True)
        xn = x * lax.rsqrt(ms + EPS) * g_ref[...]
        h_ref[...] = (xn * (1.0 + sc_ref[...]) + sh_ref[...]).astype(h_ref.dtype)


def moe_combine(ys, pos, res, mods, layer, n_rows, n_tiles, row_fn, final_gain=None,
                next_gain=None):
    d = res.shape[1]
    _, s, l = ys.shape
    tile = pl.BlockSpec((ROW_TILE, d), lambda i, pos: (i, 0))
    vec = lambda m_layer, m: pl.BlockSpec(
        (None, 1, d), functools.partial(lambda f, i, pos: f(i), _mod_spec(m_layer, m, row_fn, n_rows, 1, 0)))
    mode, gain = "plain", jnp.ones((d,), F32)
    in_specs = [pl.BlockSpec(memory_space=pl.ANY), tile, vec(layer, 5),
                pl.BlockSpec((1, d), lambda i, pos: (0, 0))]
    out_specs, out_shape = tile, jax.ShapeDtypeStruct((n_tiles * ROW_TILE, d), F32)
    args = [mods]
    if final_gain is not None:
        mode, gain = "final_norm", final_gain
    elif next_gain is not None:
        mode, gain = "next_norm", next_gain
        in_specs += [vec(layer + 1, 0), vec(layer + 1, 1)]
        args += [mods, mods]
        out_specs = [tile, tile]
        out_shape = [out_shape, jax.ShapeDtypeStruct((n_tiles * ROW_TILE, d), BF16)]
    return pl.pallas_call(
        functools.partial(_moe_combine_kernel, mode),
        grid_spec=pltpu.PrefetchScalarGridSpec(
            num_scalar_prefetch=1,
            grid=(n_tiles,),
            in_specs=in_specs,
            out_specs=out_specs,
            scratch_shapes=[pltpu.VMEM((2, 2, ROW_TILE, s, l), F32),
                            pltpu.SemaphoreType.DMA((2, 2))],
        ),
        out_shape=out_shape,
        compiler_params=_cparams(("arbitrary",)),
        name="moe_combine",
    )(pos.reshape(-1), ys, res, args[0], gain.reshape(1, d), *args[1:])


def _dispatch_plan(route):
    t = route.shape[0]
    ids = route[:, 0:2].astype(jnp.int32)
    wts = route[:, 2:4]
    flat = ids.reshape(-1)
    n_assign = flat.shape[0]
    p_max = n_assign + N_EXPERTS * ROW_TILE
    p_max = (p_max // ROW_TILE) * ROW_TILE
    onehot = (flat[:, None] == jnp.arange(N_EXPERTS, dtype=jnp.int32)[None, :]).astype(jnp.int32)
    counts = jnp.sum(onehot, axis=0)
    padded = ((counts + ROW_TILE - 1) // ROW_TILE) * ROW_TILE
    ends = jnp.cumsum(padded)
    offs = ends - padded
    starts = jnp.cumsum(counts) - counts
    order = jnp.argsort(flat, stable=True).astype(jnp.int32)
    n_tiles = p_max // ROW_TILE
    tile_first = jnp.arange(n_tiles, dtype=jnp.int32) * ROW_TILE
    tile_e = jnp.sum((tile_first[:, None] >= ends[None, :]).astype(jnp.int32), axis=1)
    tile_e = jnp.minimum(tile_e, N_EXPERTS - 1)
    tile_rank = tile_first - offs[tile_e]
    within = jnp.arange(ROW_TILE, dtype=jnp.int32)[None, :]
    rank = tile_rank[:, None] + within
    valid = (rank < counts[tile_e][:, None]) & (tile_first[:, None] < ends[-1])
    src_sorted = jnp.clip(starts[tile_e][:, None] + rank, 0, n_assign - 1).reshape(-1)
    valid = valid.reshape(-1)
    assign = order[src_sorted]
    row_ids = jnp.where(valid, assign // 2, 0).astype(jnp.int32)
    cw = jnp.where(valid, wts.reshape(-1)[assign], 0.0).astype(F32)
    sorted_pos = jnp.argsort(order).astype(jnp.int32)
    shift = jnp.sum(onehot * (offs - starts)[None, :], axis=1)
    pos = (sorted_pos + shift).astype(jnp.int32).reshape(t, 2)
    n_tiles_used = (ends[-1] // ROW_TILE).astype(jnp.int32).reshape(1)
    last_e = tile_e[jnp.maximum(n_tiles_used[0] - 1, 0)]
    tile_idx = jnp.arange(n_tiles, dtype=jnp.int32)
    tile_expert = jnp.where(tile_idx < n_tiles_used[0], tile_e, last_e).astype(jnp.int32)
    return row_ids, cw, pos, tile_expert, n_tiles_used


def hier_moe_block(h, route, w1, w3, w2):
    row_ids, cw, pos, tile_expert, n_tiles_used = _dispatch_plan(route)
    ys = expert_mlp(h, row_ids, cw[:, None], w1, w3, w2, tile_expert, n_tiles_used)
    return ys, pos


def _rope_tables(geom):
    n_lat = geom.n_lat
    half = ATT_HEAD_DIM // 2
    nfreq = half // 2
    row = jnp.repeat(jnp.arange(n_lat // GRID_W, dtype=F32), GRID_W)
    col = (jnp.arange(n_lat) % GRID_W).astype(F32)
    inv = ROPE_THETA ** (-jnp.arange(nfreq, dtype=F32) / nfreq)
    ang = jnp.concatenate([row[:, None] * inv, col[:, None] * inv], axis=-1)
    cos, sin = jnp.cos(ang), jnp.sin(ang)
    cosf = jnp.concatenate([cos, cos], axis=-1)
    sinf = jnp.concatenate([-sin, sin], axis=-1)
    cosf = jnp.concatenate([jnp.ones((geom.lc, ATT_HEAD_DIM), F32), cosf], axis=0)
    sinf = jnp.concatenate([jnp.zeros((geom.lc, ATT_HEAD_DIM), F32), sinf], axis=0)
    return cosf, sinf


def _router_weights(router_g, router_e):
    d = router_g.shape[0]
    pad = jnp.zeros((d, LANES - MOE_GROUPS - N_EXPERTS), F32)
    return jnp.concatenate([router_g, router_e, pad], axis=1)


def mixer_even(xs, mods, layer, j, n_rows, geom, norm_g, w_in, conv_dw, conv_ln_g, conv_ln_b,
               ssm_conv_w, ssm_conv_b, ssm_dt_bias, ssm_a_log, ssm_d, ssm_norm_g, w_out,
               debug=False):
    bsz, per = geom.bsz, geom.lc + geom.n_lat
    conv_d = conv_dw.shape[1]
    ssm_dd = SSM_HEADS * SSM_HEAD_DIM
    xbc_dim = ssm_conv_w.shape[1]
    h = norm_mod(xs, norm_g, mods, layer, 0, n_rows, geom.n_all, geom.all_block, geom.all_mod_row)
    c0, c1, c2, c3 = conv_d, 2 * conv_d, 2 * conv_d + ssm_dd, 2 * conv_d + ssm_dd + xbc_dim
    glu = matmul_glu(h, w_in, 0, c0, conv_d, F32)
    z = matmul(h, w_in, c1, ssm_dd, BF16)
    xbc_raw = matmul(h, w_in, c2, xbc_dim, F32)
    w_dt = jnp.pad(w_in[:, c3:], ((0, 0), (0, LANES - 2 * SSM_HEADS)))
    dt_raw = matmul(h, w_dt, 0, LANES, F32)[:, :2 * SSM_HEADS]
    conf = seq_conv(glu, conv_dw, conv_ln_g, conv_ln_b, "layernorm_silu", geom)
    xbc = seq_conv(xbc_raw, ssm_conv_w, ssm_conv_b, ssm_conv_b, "bias_silu", geom)
    dtr = dt_raw.reshape(bsz, per, 2, SSM_GROUPS, 4).transpose(0, 2, 3, 4, 1)
    dtr = jnp.pad(dtr, ((0, 0), (0, 0), (0, 0), (0, 4), (0, 0)))

    def head_rows(v):
        v = v.astype(F32).reshape(2, SSM_GROUPS, 4)
        v = jnp.pad(v, ((0, 0), (0, 0), (0, 4)))
        return jnp.broadcast_to(v[..., None], (2, SSM_GROUPS, 8, LANES))

    y2 = ssd_scan(xbc, dtr, head_rows(ssm_dt_bias), head_rows(-jnp.exp(ssm_a_log.astype(F32))), geom)
    yn = ssd_merge(y2, xbc, z, jnp.repeat(ssm_d, SSM_HEAD_DIM), ssm_norm_g, geom)
    out = matmul_residual([conf, yn], w_out, xs, mods, layer, 2, n_rows, geom.n_all,
                          geom.all_block, geom.all_mod_row)
    if debug:
        return out, dict(h=h, glu=glu, z=z, xbc_raw=xbc_raw, dt_raw=dt_raw, conf=conf, xbc=xbc,
                         y2=y2, yn=yn)
    return out


def mixer_odd(xs, mods, layer, n_rows, geom, norm_g, w_qkv, q_norm_g, k_norm_g, w_o, h=None):
    if h is None:
        h = norm_mod(xs, norm_g, mods, layer, 0, n_rows, geom.n_all, geom.all_block,
                     geom.all_mod_row)
    qkv = matmul(h, w_qkv, 0, w_qkv.shape[1], BF16)
    cosf, sinf = _rope_tables(geom)
    q, k = qk_prep(qkv, cosf, sinf, q_norm_g, k_norm_g, geom)
    att = attention(q, k, qkv, geom)
    return matmul_residual([att], w_o, xs, mods, layer, 2, n_rows, geom.n_lat_tiles,
                           geom.lat_block, geom.lat_mod_row)


def moe_layer(xs, mods, layer, n_rows, n_tiles, row_fn, norm_g, router_g, router_e, w1, w3, w2,
              final_gain, next_gain=None):
    hm, route = norm_mod(xs, norm_g, mods, layer, 3, n_rows, n_tiles, lambda t: t, row_fn,
                         router_w=_router_weights(router_g, router_e))
    w1 = cast_experts(w1, layer)
    w3 = cast_experts(w3, layer)
    w2 = cast_experts(w2, layer)
    ys, pos = hier_moe_block(hm, route, w1, w3, w2)
    return moe_combine(ys, pos, xs, mods, layer, n_rows, n_tiles, row_fn, final_gain=final_gain,
                       next_gain=next_gain)


def kernel(x, c, ctx, c_ctx, w_ada, b_ada, norm_mix, norm_ffn, w_in0, conv_dw, conv_ln_g, conv_ln_b, ssm_conv_w, ssm_conv_b, ssm_dt_bias, ssm_a_log, ssm_d, ssm_norm_g, w_out0, w_qkv, q_norm_g, k_norm_g, w_o, moe_router_g, moe_router_e, moe_w1, moe_w3, moe_w2, norm_final):
    bsz, n_lat, d = x.shape
    lc = ctx.shape[1]
    depth = w_ada.shape[0]
    geom = Geom(bsz, lc, n_lat)
    n_rows = 16

    cvec = jnp.concatenate([c, c_ctx[None, :], jnp.zeros((n_rows - bsz - 1, d), F32)], axis=0)
    mods = ada_mods(cvec, w_ada, b_ada).reshape(depth * n_rows * N_MOD, 1, d)
    xs = (ctx.reshape(bsz * lc, d), x.reshape(bsz * n_lat, d), geom)

    h_next = None
    for i in range(depth):
        last = i == depth - 1
        j = i // 2
        if i % 2 == 0:
            if last:
                raise NotImplementedError("a conv/SSD mixer in the last layer is not supported")
            xs = mixer_even(xs, mods, i, j, n_rows, geom, norm_mix[i], w_in0[j], conv_dw[j],
                            conv_ln_g[j], conv_ln_b[j], ssm_conv_w[j], ssm_conv_b[j],
                            ssm_dt_bias[j], ssm_a_log[j], ssm_d[j], ssm_norm_g[j], w_out0[j])
            n_tiles, row_fn = geom.n_all, geom.all_mod_row
        else:
            if not last:
                raise NotImplementedError("an attention mixer before the last layer is not supported")
            xs = mixer_odd(xs, mods, i, n_rows, geom, norm_mix[i], w_qkv[j], q_norm_g[j],
                           k_norm_g[j], w_o[j], h=h_next)
            n_tiles, row_fn = geom.n_lat_tiles, geom.lat_mod_row
        feeds_next = not last and (i + 1) % 2 == 1
        out = moe_layer(xs, mods, i, n_rows, n_tiles, row_fn, norm_ffn[i], moe_router_g[i],
                        moe_router_e[i], moe_w1, moe_w3, moe_w2, norm_final if last else None,
                        next_gain=norm_mix[i + 1] if feeds_next else None)
        xs, h_next = out if feeds_next else (out, None)
    return xs.reshape(bsz, n_lat, d)
```

```python
import functools
import math

import jax
import jax.numpy as jnp
from jax import lax
from jax.experimental import pallas as pl
from jax.experimental.pallas import tpu as pltpu

F32 = jnp.float32
BF16 = jnp.bfloat16

EPS = 1e-6
N_MOD = 6
GRID_W = 64
ROPE_THETA = 10000.0
CONV_WIDTH = 31
SSM_CONV_WIDTH = 5
SSM_HEADS = 32
SSM_HEAD_DIM = 64
SSM_GROUPS = 8
SSM_STATE = 128
SSM_CHUNK = 128
ATT_HEADS = 32
ATT_KV_HEADS = 8
ATT_HEAD_DIM = 128
MOE_GROUPS = 4
MOE_EXPERTS = 8
N_EXPERTS = MOE_GROUPS * MOE_EXPERTS

LANES = 128
SUBLANES = 8
ROW_TILE = 256
HALO = 16
VMEM_LIMIT = 56 * 1024 * 1024


def _cparams(sem):
    return pltpu.CompilerParams(dimension_semantics=sem, vmem_limit_bytes=VMEM_LIMIT)


def _sigmoid(v):
    return 0.5 * (jnp.tanh(0.5 * v) + 1.0)


def _silu(v):
    return v * _sigmoid(v)


def _ada_kernel(c_ref, w_ref, b_ref, o_ref):
    s = _silu(c_ref[...]).astype(BF16)
    w = w_ref[...].astype(BF16)
    o_ref[...] = jnp.dot(s, w, preferred_element_type=F32) + b_ref[...]


def ada_mods(cvec, w_ada, b_ada, tn=512):
    depth, d, n = w_ada.shape
    rows = cvec.shape[0]
    return pl.pallas_call(
        _ada_kernel,
        grid=(depth, n // tn),
        in_specs=[
            pl.BlockSpec((rows, d), lambda l, j: (0, 0)),
            pl.BlockSpec((None, d, tn), lambda l, j: (l, 0, j)),
            pl.BlockSpec((None, 1, tn), lambda l, j: (l, 0, j)),
        ],
        out_specs=pl.BlockSpec((None, rows, tn), lambda l, j: (l, 0, j)),
        out_shape=jax.ShapeDtypeStruct((depth, rows, n), F32),
        compiler_params=_cparams(("arbitrary", "arbitrary")),
        name="ada_mods",
    )(cvec, w_ada, b_ada.reshape(depth, 1, n))


class Geom:
    def __init__(self, bsz, lc, n_lat):
        self.bsz, self.lc, self.n_lat = bsz, lc, n_lat
        self.per_b = (lc + n_lat) // ROW_TILE
        self.ctx_t = lc // ROW_TILE
        self.lat_t = n_lat // ROW_TILE
        self.n_all = bsz * self.per_b
        self.n_lat_tiles = bsz * self.lat_t

    def all_block(self, i):
        return i

    def lat_block(self, i):
        return (i // self.lat_t) * self.per_b + self.ctx_t + i % self.lat_t

    def all_mod_row(self, i):
        return jnp.where(i % self.per_b < self.ctx_t, self.bsz, i // self.per_b)

    def lat_mod_row(self, i):
        return i // self.lat_t

    def ctx_src_block(self, i):
        return (i // self.per_b) * self.ctx_t + jnp.minimum(i % self.per_b, self.ctx_t - 1)

    def lat_src_block(self, i):
        return (i // self.per_b) * self.lat_t + jnp.clip(i % self.per_b - self.ctx_t, 0,
                                                         self.lat_t - 1)

    def tile_is_ctx(self, i):
        return i % self.per_b < self.ctx_t


def _mod_spec(layer, m, row_fn, n_rows, grid_rank, tile_axis, width=None, col_axis=None):
    def imap(*g):
        r = row_fn(g[tile_axis])
        c = 0 if col_axis is None else g[col_axis]
        return ((layer * n_rows + r) * N_MOD + m, 0, c)
    return imap


def _norm_mod_kernel(geom, *refs):
    if geom is None:
        x_ref, g_ref, sh_ref, sc_ref, h_ref = refs
        x = x_ref[...]
    else:
        xc_ref, xl_ref, g_ref, sh_ref, sc_ref, h_ref = refs
        x = jnp.where(geom.tile_is_ctx(pl.program_id(0)), xc_ref[...], xl_ref[...])
    ms = jnp.mean(x * x, axis=-1, keepdims=True)
    xn = x * lax.rsqrt(ms + EPS) * g_ref[...]
    h_ref[...] = (xn * (1.0 + sc_ref[...]) + sh_ref[...]).astype(h_ref.dtype)


def _route(logits):
    lane = lax.broadcasted_iota(jnp.int32, logits.shape, 1)
    neg = jnp.float32(-jnp.inf)
    big = jnp.int32(LANES)
    lg = jnp.where(lane < MOE_GROUPS, logits, neg)
    mg = jnp.max(lg, axis=-1, keepdims=True)
    gsel = jnp.min(jnp.where(lg == mg, lane, big), axis=-1, keepdims=True)
    pg = 1.0 / jnp.sum(jnp.exp(lg - mg), axis=-1, keepdims=True)
    lo = MOE_GROUPS + gsel * MOE_EXPERTS
    le = jnp.where((lane >= lo) & (lane < lo + MOE_EXPERTS), logits, neg)
    v1 = jnp.max(le, axis=-1, keepdims=True)
    i1 = jnp.min(jnp.where(le == v1, lane, big), axis=-1, keepdims=True)
    le2 = jnp.where(lane == i1, neg, le)
    v2 = jnp.max(le2, axis=-1, keepdims=True)
    i2 = jnp.min(jnp.where(le2 == v2, lane, big), axis=-1, keepdims=True)
    e2 = jnp.exp(v2 - v1)
    w1 = pg / (1.0 + e2)
    w2 = pg * e2 / (1.0 + e2)
    id1 = (i1 - MOE_GROUPS).astype(F32)
    id2 = (i2 - MOE_GROUPS).astype(F32)
    return jnp.where(lane == 0, id1, jnp.where(lane == 1, id2,
                     jnp.where(lane == 2, w1, jnp.where(lane == 3, w2, 0.0))))


def _to_row_major(x):
    parts = [x[:, s * LANES:(s + 1) * LANES] for s in range(x.shape[1] // LANES)]
    return pltpu.einshape("stl->tsl", jnp.stack(parts, axis=0))


def _from_row_major(x3, dtype):
    xt = pltpu.einshape("tsl->stl", x3)
    return jnp.concatenate([xt[s].astype(dtype) for s in range(x3.shape[1])], axis=1)


def _norm_mod_route_kernel(x_ref, g_ref, sh_ref, sc_ref, wr_ref, h_ref, r_ref):
    x = x_ref[...]
    ms = jnp.mean(x * x, axis=-1, keepdims=True)
    xn = x * lax.rsqrt(ms + EPS) * g_ref[...]
    h = xn * (1.0 + sc_ref[...]) + sh_ref[...]
    h_ref[...] = h
    logits = jnp.dot(h, wr_ref[...], preferred_element_type=F32,
                     precision=lax.Precision.HIGHEST)
    r_ref[...] = _route(logits)


def norm_mod(x, gain, mods, layer, m_shift, n_rows, n_tiles, block_fn, row_fn,
             router_w=None, out_dtype=BF16):
    split = None
    if isinstance(x, tuple):
        xc, xl, split = x
        d = xc.shape[1]
        x_specs = [pl.BlockSpec((ROW_TILE, d), lambda i: (split.ctx_src_block(i), 0)),
                   pl.BlockSpec((ROW_TILE, d), lambda i: (split.lat_src_block(i), 0))]
        x_args = [xc, xl]
        assert router_w is None
    else:
        d = x.shape[1]
        x_specs = [pl.BlockSpec((ROW_TILE, d), lambda i: (block_fn(i), 0))]
        x_args = [x]
    x_spec = x_specs[0]
    g_spec = pl.BlockSpec((1, d), lambda i: (0, 0))
    sh_spec = pl.BlockSpec((None, 1, d), _mod_spec(layer, m_shift, row_fn, n_rows, 1, 0))
    sc_spec = pl.BlockSpec((None, 1, d), _mod_spec(layer, m_shift + 1, row_fn, n_rows, 1, 0))
    o_spec = pl.BlockSpec((ROW_TILE, d), lambda i: (i, 0))
    rows = n_tiles * ROW_TILE
    if router_w is None:
        return pl.pallas_call(
            functools.partial(_norm_mod_kernel, split),
            grid=(n_tiles,),
            in_specs=x_specs + [g_spec, sh_spec, sc_spec],
            out_specs=o_spec,
            out_shape=jax.ShapeDtypeStruct((rows, d), out_dtype),
            compiler_params=_cparams(("parallel",)),
            name="norm_mod",
        )(*x_args, gain.reshape(1, d), mods, mods)
    return pl.pallas_call(
        _norm_mod_route_kernel,
        grid=(n_tiles,),
        in_specs=[x_spec, g_spec, sh_spec, sc_spec,
                  pl.BlockSpec((d, LANES), lambda i: (0, 0))],
        out_specs=[o_spec, pl.BlockSpec((ROW_TILE, LANES), lambda i: (i, 0))],
        out_shape=[jax.ShapeDtypeStruct((rows, d), F32),
                   jax.ShapeDtypeStruct((rows, LANES), F32)],
        compiler_params=_cparams(("parallel",)),
        name="norm_mod_route",
    )(x, gain.reshape(1, d), mods, mods, router_w)


def _stage_weights(w_refs, wb_refs):
    @pl.when(pl.program_id(1) == 0)
    def _():
        for w_ref, wb_ref in zip(w_refs, wb_refs):
            wb_ref[...] = w_ref[...].astype(BF16)


def _mm_kernel(x_ref, w_ref, o_ref, wb_ref):
    _stage_weights([w_ref], [wb_ref])
    o_ref[...] = jnp.dot(x_ref[...], wb_ref[...], preferred_element_type=F32).astype(o_ref.dtype)


def _row_block(m, want=1024):
    tm = want
    while m % tm:
        tm //= 2
    return tm


def matmul(x, w, col0, n, out_dtype, tn=512):
    m, k = x.shape
    tm = _row_block(m)
    tn = min(tn, n)
    cb = col0 // tn
    return pl.pallas_call(
        _mm_kernel,
        grid=(n // tn, m // tm),
        in_specs=[pl.BlockSpec((tm, k), lambda j, i: (i, 0)),
                  pl.BlockSpec((k, tn), lambda j, i: (0, cb + j))],
        out_specs=pl.BlockSpec((tm, tn), lambda j, i: (i, j)),
        out_shape=jax.ShapeDtypeStruct((m, n), out_dtype),
        scratch_shapes=[pltpu.VMEM((k, tn), BF16)],
        compiler_params=_cparams(("parallel", "arbitrary")),
        name="matmul",
    )(x, w)


def _mm_glu_kernel(x_ref, wa_ref, wg_ref, o_ref, wab_ref, wgb_ref):
    _stage_weights([wa_ref, wg_ref], [wab_ref, wgb_ref])
    x = x_ref[...]
    a = jnp.dot(x, wab_ref[...], preferred_element_type=F32)
    b = jnp.dot(x, wgb_ref[...], preferred_element_type=F32)
    o_ref[...] = (a * _sigmoid(b)).astype(o_ref.dtype)


def matmul_glu(x, w, col_a, col_g, n, out_dtype, tn=256):
    m, k = x.shape
    tm = _row_block(m)
    ca, cg = col_a // tn, col_g // tn
    return pl.pallas_call(
        _mm_glu_kernel,
        grid=(n // tn, m // tm),
        in_specs=[pl.BlockSpec((tm, k), lambda j, i: (i, 0)),
                  pl.BlockSpec((k, tn), lambda j, i: (0, ca + j)),
                  pl.BlockSpec((k, tn), lambda j, i: (0, cg + j))],
        out_specs=pl.BlockSpec((tm, tn), lambda j, i: (i, j)),
        out_shape=jax.ShapeDtypeStruct((m, n), out_dtype),
        scratch_shapes=[pltpu.VMEM((k, tn), BF16), pltpu.VMEM((k, tn), BF16)],
        compiler_params=_cparams(("parallel", "arbitrary")),
        name="matmul_glu",
    )(x, w, w)


def _mm_res_kernel(n_lhs, split, *refs):
    xs = refs[:n_lhs]
    ws = refs[n_lhs:2 * n_lhs]
    n_res = 1 if split is None else 2
    res_refs = refs[2 * n_lhs:2 * n_lhs + n_res]
    gate_ref, o_ref = refs[2 * n_lhs + n_res:2 * n_lhs + n_res + 2]
    wbs = refs[2 * n_lhs + n_res + 2:]
    _stage_weights(ws, wbs)
    acc = jnp.dot(xs[0][...], wbs[0][...], preferred_element_type=F32)
    for x_ref, wb_ref in zip(xs[1:], wbs[1:]):
        acc = acc + jnp.dot(x_ref[...], wb_ref[...], preferred_element_type=F32)
    if split is None:
        res = res_refs[0][...]
    else:
        res = jnp.where(split.tile_is_ctx(pl.program_id(1)), res_refs[0][...], res_refs[1][...])
    o_ref[...] = res + gate_ref[...] * acc


def matmul_residual(lhs, w, res, mods, layer, m_gate, n_rows, n_tiles, res_block_fn, row_fn,
                    tn=1024):
    n_lhs = len(lhs)
    n = w.shape[1]
    widths = [a.shape[1] for a in lhs]
    assert all(wd == widths[0] for wd in widths) and sum(widths) == w.shape[0]
    in_specs = [pl.BlockSpec((ROW_TILE, wd), lambda j, i: (i, 0)) for wd in widths]
    in_specs += [pl.BlockSpec((wd, tn), functools.partial(lambda q, j, i: (q, j), q),
                              pipeline_mode=pl.Buffered(1))
                 for q, wd in enumerate(widths)]
    split = None
    if isinstance(res, tuple):
        rc, rl, split = res
        in_specs += [pl.BlockSpec((ROW_TILE, tn), lambda j, i: (split.ctx_src_block(i), j)),
                     pl.BlockSpec((ROW_TILE, tn), lambda j, i: (split.lat_src_block(i), j))]
        res_args = [rc, rl]
    else:
        in_specs.append(pl.BlockSpec((ROW_TILE, tn), lambda j, i: (res_block_fn(i), j)))
        res_args = [res]
    in_specs.append(
        pl.BlockSpec((None, 1, tn), _mod_spec(layer, m_gate, row_fn, n_rows, 2, 1, col_axis=0)))
    return pl.pallas_call(
        functools.partial(_mm_res_kernel, n_lhs, split),
        grid=(n // tn, n_tiles),
        in_specs=in_specs,
        out_specs=pl.BlockSpec((ROW_TILE, tn), lambda j, i: (i, j)),
        out_shape=jax.ShapeDtypeStruct((n_tiles * ROW_TILE, n), F32),
        scratch_shapes=[pltpu.VMEM((wd, tn), BF16) for wd in widths],
        compiler_params=_cparams(("parallel", "arbitrary")),
        name="matmul_residual",
    )(*lhs, *([w] * n_lhs), *res_args, mods)


def _conv_kernel(width, mode, per_b, ctx_t, x_ref, prev_ref, next_ref, w_ref, p0_ref, p1_ref,
                 o_ref, xp_ref, acc_ref, sh_ref):
    t = pl.program_id(0) % per_b
    first = (t == 0) | (t == ctx_t)
    last = (t == ctx_t - 1) | (t == per_b - 1)
    c = x_ref.shape[1]
    xp_ref[0:HALO, :] = jnp.where(first, 0.0, prev_ref[...])
    xp_ref[HALO:HALO + ROW_TILE, :] = x_ref[...]
    xp_ref[HALO + ROW_TILE:, :] = jnp.where(last, 0.0, next_ref[...])
    off = HALO - width // 2
    sub = SUBLANES
    taps_of = [[k for k in range(width) if (off + k) % sub == ph] for ph in range(sub)]
    span = ROW_TILE + sub * ((off + width - 1) // sub)

    def chunk(j, carry):
        cs = pl.ds(pl.multiple_of(j * LANES, LANES), LANES)
        for ph in range(1, sub):
            if taps_of[ph]:
                sh_ref[ph, 0:span, :] = xp_ref[ph:ph + span, cs]
        acc = jnp.zeros((ROW_TILE, LANES), F32)
        for ph in range(sub):
            for k in taps_of[ph]:
                lo = off + k - ph
                if ph == 0:
                    win = xp_ref[lo:lo + ROW_TILE, cs]
                else:
                    win = sh_ref[ph, lo:lo + ROW_TILE, :]
                acc = acc + win * w_ref[k:k + 1, cs]
        acc_ref[:, cs] = acc
        return carry

    lax.fori_loop(0, c // LANES, chunk, 0)
    y = acc_ref[...]
    if mode == "layernorm_silu":
        mu = jnp.mean(y, axis=-1, keepdims=True)
        yc = y - mu
        var = jnp.mean(yc * yc, axis=-1, keepdims=True)
        y = yc * lax.rsqrt(var + EPS) * p0_ref[...] + p1_ref[...]
    else:
        y = y + p0_ref[...]
    o_ref[...] = _silu(y).astype(o_ref.dtype)


def seq_conv(x, w, p0, p1, mode, geom, out_dtype=BF16):
    rows, c = x.shape
    width = w.shape[0]
    hb = ROW_TILE // HALO
    n_halo = rows // HALO
    kern = functools.partial(_conv_kernel, width, mode, geom.per_b, geom.ctx_t)
    return pl.pallas_call(
        kern,
        grid=(geom.n_all,),
        in_specs=[
            pl.BlockSpec((ROW_TILE, c), lambda i: (i, 0)),
            pl.BlockSpec((HALO, c), lambda i: (jnp.maximum(i * hb - 1, 0), 0)),
            pl.BlockSpec((HALO, c), lambda i: (jnp.minimum((i + 1) * hb, n_halo - 1), 0)),
            pl.BlockSpec((width, c), lambda i: (0, 0)),
            pl.BlockSpec((1, c), lambda i: (0, 0)),
            pl.BlockSpec((1, c), lambda i: (0, 0)),
        ],
        out_specs=pl.BlockSpec((ROW_TILE, c), lambda i: (i, 0)),
        out_shape=jax.ShapeDtypeStruct((rows, c), out_dtype),
        scratch_shapes=[pltpu.VMEM((ROW_TILE + 2 * HALO, c), F32),
                        pltpu.VMEM((ROW_TILE, c), F32),
                        pltpu.VMEM((SUBLANES, ROW_TILE + 2 * HALO, LANES), F32)],
        compiler_params=_cparams(("parallel",)),
        name="seq_conv_" + mode,
    )(x, x, x, w, p0.reshape(1, c), p1.reshape(1, c))


def _ssd_kernel(n_chunks, ctx_chunks, xs_ref, b_ref, c_ref, dtraw_ref, bias_ref, aneg_ref,
                y_ref, state_ref, tr_ref, row_ref, col_ref, cb_ref, bt_ref):
    T = SSM_CHUNK
    rows = lax.broadcasted_iota(jnp.int32, (T, T), 0)
    cols = lax.broadcasted_iota(jnp.int32, (T, T), 1)
    lane = lax.broadcasted_iota(jnp.int32, (T, T), 1)
    left = lane < SSM_HEAD_DIM
    state_ref[...] = jnp.zeros_like(state_ref)
    tr_ref[...] = jnp.zeros_like(tr_ref)
    n_heads = 4

    def row_of(d, s):
        s = jnp.minimum(s, n_chunks - 1)
        if d == 1:
            s = jnp.where(s < ctx_chunks, ctx_chunks - 1 - s, n_chunks - 1 + ctx_chunks - s)
        return pl.multiple_of(s * T, T)

    def prep(d, s, slot):
        diff = rows - cols if d == 0 else cols - rows
        tri = jnp.where(diff <= 0, 1.0, 0.0).astype(F32)
        r0 = row_of(d, s)
        raw = dtraw_ref[d, :, pl.ds(r0, T)]
        z = raw + bias_ref[d]
        dt_row = jnp.maximum(z, 0.0) + jnp.log1p(jnp.exp(-jnp.abs(z)))
        da_row = dt_row * aneg_ref[d]
        cum_row = jnp.dot(da_row, tri, preferred_element_type=F32,
                          precision=lax.Precision.HIGHEST)
        tot_row = jnp.sum(da_row, axis=-1, keepdims=True)
        tr_ref[d, 0:8, :] = cum_row
        row_ref[slot, d, 0:8, :] = dt_row
        row_ref[slot, d, 8:16, :] = cum_row
        row_ref[slot, d, 16:24, :] = jnp.broadcast_to(tot_row, (8, T))
        col_ref[slot, d] = tr_ref[d].T
        bmat = b_ref[pl.ds(r0, T), :]
        cmat = c_ref[pl.ds(r0, T), :]
        cb_ref[slot, d] = lax.dot_general(cmat, bmat, (((1,), (1,)), ((), ())),
                                          preferred_element_type=F32)
        bt_ref[slot, d] = bmat.astype(F32).T

    def main(d, s, slot):
        diff = rows - cols if d == 0 else cols - rows
        keep = diff >= 0
        r0 = row_of(d, s)
        dt_row = row_ref[slot, d, 0:8, :]
        cum_row = row_ref[slot, d, 8:16, :]
        tot_row = row_ref[slot, d, 16:24, 0:1]
        colform = col_ref[slot, d]
        cb = cb_ref[slot, d]
        bt = bt_ref[slot, d]
        cmat = c_ref[pl.ds(r0, T), :]
        for pair in range(n_heads // 2):
            xs_pair = xs_ref[pl.ds(r0, T), pair * LANES:(pair + 1) * LANES]
            ms, bds, ecs, ets = [], [], [], []
            for q in range(2):
                r = pair * 2 + q
                cum_c = jnp.broadcast_to(colform[:, r:r + 1], (T, T))
                cum_r = cum_row[r:r + 1, :]
                dt_r = dt_row[r:r + 1, :]
                seg = jnp.where(keep, cum_c - cum_r, -jnp.inf)
                ms.append((cb * jnp.exp(seg) * dt_r).astype(BF16))
                dec_r = jnp.exp(tot_row[r:r + 1, :] - cum_r) * dt_r
                bds.append((bt * dec_r).astype(BF16))
                ecs.append(jnp.exp(cum_c))
                ets.append(jnp.exp(tot_row[r:r + 1, :]))
            xs32 = xs_pair.astype(F32)
            xd2 = jnp.concatenate([jnp.where(left, xs32, 0.0).astype(BF16),
                                   jnp.where(left, 0.0, xs32).astype(BF16)], axis=0)
            m2 = jnp.concatenate(ms, axis=1)
            y_diag = jnp.dot(m2, xd2, preferred_element_type=F32)
            st = state_ref[d, pair]
            y_off = jnp.dot(cmat, st.astype(BF16), preferred_element_type=F32)
            y_off = y_off * jnp.where(left, ecs[0], ecs[1])
            y_ref[d, pl.ds(r0, T), pair * LANES:(pair + 1) * LANES] = (y_diag + y_off).astype(y_ref.dtype)
            bd2 = jnp.concatenate(bds, axis=1)
            upd = jnp.dot(bd2, xd2, preferred_element_type=F32)
            state_ref[d, pair] = st * jnp.where(left[0:1, :], ets[0], ets[1]) + upd

    def two_steps(i, carry):
        for k in range(2):
            for d in range(2):
                main(d, 2 * i + k, k)
            for d in range(2):
                prep(d, 2 * i + k + 1, 1 - k)
        return carry

    assert n_chunks % 2 == 0
    for d in range(2):
        prep(d, 0, 0)
    lax.fori_loop(0, n_chunks // 2, two_steps, 0)


def ssd_scan(xbc, dtraw_rows, bias_rows, aneg_rows, geom):
    rows = xbc.shape[0]
    per = geom.lc + geom.n_lat
    n_chunks = per // SSM_CHUNK
    ctx_chunks = geom.lc // SSM_CHUNK
    ssm_d = SSM_HEADS * SSM_HEAD_DIM
    gw = 4 * SSM_HEAD_DIM
    xs_blocks = ssm_d // gw
    kern = functools.partial(_ssd_kernel, n_chunks, ctx_chunks)
    return pl.pallas_call(
        kern,
        grid=(geom.bsz, SSM_GROUPS),
        in_specs=[
            pl.BlockSpec((per, gw), lambda b, g: (b, g)),
            pl.BlockSpec((per, SSM_STATE), lambda b, g: (b, 2 * xs_blocks + g)),
            pl.BlockSpec((per, SSM_STATE), lambda b, g: (b, 2 * xs_blocks + SSM_GROUPS + g)),
            pl.BlockSpec((None, 2, None, 8, per), lambda b, g: (b, 0, g, 0, 0)),
            pl.BlockSpec((2, None, 8, LANES), lambda b, g: (0, g, 0, 0)),
            pl.BlockSpec((2, None, 8, LANES), lambda b, g: (0, g, 0, 0)),
        ],
        out_specs=pl.BlockSpec((2, per, gw), lambda b, g: (0, b, g)),
        out_shape=jax.ShapeDtypeStruct((2, rows, ssm_d), BF16),
        scratch_shapes=[pltpu.VMEM((2, 2, SSM_STATE, LANES), F32),
                        pltpu.VMEM((2, LANES, SSM_CHUNK), F32),
                        pltpu.VMEM((2, 2, 24, SSM_CHUNK), F32),
                        pltpu.VMEM((2, 2, SSM_CHUNK, LANES), F32),
                        pltpu.VMEM((2, 2, SSM_CHUNK, SSM_CHUNK), F32),
                        pltpu.VMEM((2, 2, SSM_STATE, SSM_CHUNK), F32)],
        compiler_params=_cparams(("parallel", "parallel")),
        name="ssd_scan",
    )(xbc, xbc, xbc, dtraw_rows, bias_rows, aneg_rows)


def _ssd_merge_kernel(yf_ref, yb_ref, xs_ref, z_ref, d_ref, g_ref, o_ref):
    y = d_ref[...] * xs_ref[...].astype(F32) + yf_ref[...].astype(F32) + yb_ref[...].astype(F32)
    gated = y * _silu(z_ref[...].astype(F32))
    ms = jnp.mean(gated * gated, axis=-1, keepdims=True)
    o_ref[...] = (gated * lax.rsqrt(ms + EPS) * g_ref[...]).astype(o_ref.dtype)


def ssd_merge(y2, xbc, z, d_full, g, geom):
    rows, ssm_d = z.shape
    spec = pl.BlockSpec((ROW_TILE, ssm_d), lambda i: (i, 0))
    vec = pl.BlockSpec((1, ssm_d), lambda i: (0, 0))
    return pl.pallas_call(
        _ssd_merge_kernel,
        grid=(geom.n_all,),
        in_specs=[pl.BlockSpec((None, ROW_TILE, ssm_d), lambda i: (0, i, 0)),
                  pl.BlockSpec((None, ROW_TILE, ssm_d), lambda i: (1, i, 0)),
                  spec, spec, vec, vec],
        out_specs=spec,
        out_shape=jax.ShapeDtypeStruct((rows, ssm_d), BF16),
        compiler_params=_cparams(("parallel",)),
        name="ssd_merge",
    )(y2, y2, xbc, z, d_full.reshape(1, ssm_d), g.reshape(1, ssm_d))


def _qk_prep_kernel(n_q, n_k, qkv_ref, aq_ref, bq_ref, ak_ref, bk_ref, q_ref, k_ref):
    hd = ATT_HEAD_DIM
    for h in range(n_q + n_k):
        x = qkv_ref[:, h * hd:(h + 1) * hd].astype(F32)
        a, b = (aq_ref[...], bq_ref[...]) if h < n_q else (ak_ref[...], bk_ref[...])
        inv = lax.rsqrt(jnp.mean(x * x, axis=-1, keepdims=True) + EPS)
        o = (x * a + pltpu.roll(x, hd // 2, axis=1) * b) * inv
        if h < n_q:
            q_ref[:, h * hd:(h + 1) * hd] = o.astype(q_ref.dtype)
        else:
            k_ref[:, (h - n_q) * hd:(h - n_q + 1) * hd] = o.astype(k_ref.dtype)


def qk_prep(qkv, cosf, sinf, gq, gk, geom):
    rows = qkv.shape[0]
    hd = ATT_HEAD_DIM
    qc, kc = ATT_HEADS * hd, ATT_KV_HEADS * hd
    scale = hd ** -0.5
    tables = [gq * cosf * scale, jnp.roll(gq, hd // 2) * sinf * scale,
              gk * cosf, jnp.roll(gk, hd // 2) * sinf]
    kern = functools.partial(_qk_prep_kernel, ATT_HEADS, ATT_KV_HEADS)
    per_b = geom.per_b
    tab = pl.BlockSpec((ROW_TILE, hd), lambda i: (i % per_b, 0))
    return pl.pallas_call(
        kern,
        grid=(geom.n_all,),
        in_specs=[pl.BlockSpec((ROW_TILE, qc + kc), lambda i: (i, 0)), tab, tab, tab, tab],
        out_specs=[pl.BlockSpec((ROW_TILE, qc), lambda i: (i, 0)),
                   pl.BlockSpec((ROW_TILE, kc), lambda i: (i, 0))],
        out_shape=[jax.ShapeDtypeStruct((rows, qc), BF16),
                   jax.ShapeDtypeStruct((rows, kc), BF16)],
        compiler_params=_cparams(("parallel",)),
        name="qk_prep",
    )(qkv, *tables)


def _attn_kernel(rep, q_ref, k_ref, v_ref, o_ref):
    hd = ATT_HEAD_DIM
    k = k_ref[...]
    v = v_ref[...]
    v1 = jnp.concatenate([v, jnp.ones_like(v)], axis=1)
    for r in range(rep):
        q = q_ref[:, r * hd:(r + 1) * hd]
        s = lax.dot_general(q, k, (((1,), (1,)), ((), ())), preferred_element_type=F32)
        m = jnp.max(s, axis=-1, keepdims=True)
        p = jnp.exp((s - m).astype(BF16))
        o = jnp.dot(p, v1, preferred_element_type=F32)
        o_ref[:, r * hd:(r + 1) * hd] = (o[:, :hd] / o[:, hd:hd + 1]).astype(o_ref.dtype)


def attention(q, k, qkv, geom):
    hd = ATT_HEAD_DIM
    rep = ATT_HEADS // ATT_KV_HEADS
    per = geom.lc + geom.n_lat
    v_col0 = (ATT_HEADS + ATT_KV_HEADS)
    lat_t, per_b, ctx_t = geom.lat_t, geom.per_b, geom.ctx_t
    return pl.pallas_call(
        functools.partial(_attn_kernel, rep),
        grid=(geom.bsz, ATT_KV_HEADS, lat_t),
        in_specs=[pl.BlockSpec((ROW_TILE, rep * hd), lambda b, g, t: (b * per_b + ctx_t + t, g)),
                  pl.BlockSpec((per, hd), lambda b, g, t: (b, g)),
                  pl.BlockSpec((per, hd), lambda b, g, t: (b, v_col0 + g))],
        out_specs=pl.BlockSpec((ROW_TILE, rep * hd), lambda b, g, t: (b * lat_t + t, g)),
        out_shape=jax.ShapeDtypeStruct((geom.bsz * geom.n_lat, ATT_HEADS * hd), BF16),
        compiler_params=_cparams(("parallel", "parallel", "parallel")),
        name="attention",
    )(q, k, qkv)


def _start_row_gather(ids_ref, first, stride, src_ref, dst_ref, sem, inline=False):
    def issue(r, carry):
        row = ids_ref[first + stride * r]
        pltpu.make_async_copy(src_ref.at[pl.ds(row, 1)], dst_ref.at[pl.ds(r, 1)], sem).start()
        return carry

    if inline:
        for r in range(ROW_TILE):
            issue(r, 0)
    else:
        lax.fori_loop(0, ROW_TILE, issue, 0, unroll=8)


def _wait_row_gather(src_ref, dst_ref, sem):
    pltpu.make_async_copy(src_ref.at[pl.ds(0, ROW_TILE)], dst_ref, sem).wait()


def _cast_kernel(x_ref, o_ref):
    o_ref[...] = x_ref[...].astype(o_ref.dtype)


def cast_experts(w_all, layer, dtype=BF16):
    depth, g, e, k, n = w_all.shape
    ne = g * e
    return pl.pallas_call(
        _cast_kernel,
        grid=(ne,),
        in_specs=[pl.BlockSpec((None, k, n), lambda i: (layer * ne + i, 0, 0))],
        out_specs=pl.BlockSpec((None, k, n), lambda i: (i, 0, 0)),
        out_shape=jax.ShapeDtypeStruct((ne, k, n), dtype),
        compiler_params=_cparams(("parallel",)),
        name="cast_experts",
    )(w_all.reshape(depth * ne, k, n))


def _expert_mlp_kernel(te_ref, nt_ref, ids_ref, h_ref, cw_ref, w1_ref, w3_ref, w2_ref, o_ref,
                       xbuf_ref, sem):
    t = pl.program_id(0)
    nt = nt_ref[0]
    slot = t % 2

    @pl.when(t == 0)
    def _():
        _start_row_gather(ids_ref, 0, 1, h_ref, xbuf_ref.at[0], sem.at[0])

    @pl.when(t < nt)
    def _():
        _wait_row_gather(h_ref, xbuf_ref.at[slot], sem.at[slot])
        nxt = jnp.minimum(t + 1, nt - 1)
        _start_row_gather(ids_ref, nxt * ROW_TILE, 1, h_ref, xbuf_ref.at[1 - slot],
                          sem.at[1 - slot], inline=True)
        x = xbuf_ref[slot].astype(BF16)
        a = jnp.dot(x, w1_ref[...], preferred_element_type=F32)
        b = jnp.dot(x, w3_ref[...], preferred_element_type=F32)
        hid = a * (1.0 / (1.0 + jnp.exp(-a))) * b * cw_ref[...]
        y = jnp.dot(hid.astype(BF16), w2_ref[...], preferred_element_type=F32)
        o_ref[...] = _to_row_major(y)

        @pl.when(t + 1 == nt)
        def _():
            _wait_row_gather(h_ref, xbuf_ref.at[1 - slot], sem.at[1 - slot])

    @pl.when(t >= nt)
    def _():
        o_ref[...] = jnp.zeros_like(o_ref)


def expert_mlp(h, row_ids, cw, w1, w3, w2, tile_expert, n_tiles_used):
    p = row_ids.shape[0]
    d = h.shape[1]
    s, l = d // LANES, LANES
    hdim = w1.shape[2]
    n_tiles = p // ROW_TILE

    def wmap(t, te, nt, ids):
        return (te[t], 0, 0)

    return pl.pallas_call(
        _expert_mlp_kernel,
        grid_spec=pltpu.PrefetchScalarGridSpec(
            num_scalar_prefetch=3,
            grid=(n_tiles,),
            in_specs=[pl.BlockSpec(memory_space=pl.ANY),
                      pl.BlockSpec((ROW_TILE, 1), lambda t, te, nt, ids: (t, 0)),
                      pl.BlockSpec((None, d, hdim), wmap),
                      pl.BlockSpec((None, d, hdim), wmap),
                      pl.BlockSpec((None, hdim, d), wmap)],
            out_specs=pl.BlockSpec((ROW_TILE, s, l), lambda t, te, nt, ids: (t, 0, 0)),
            scratch_shapes=[pltpu.VMEM((2, ROW_TILE, d), F32), pltpu.SemaphoreType.DMA((2,))],
        ),
        out_shape=jax.ShapeDtypeStruct((p, s, l), F32),
        compiler_params=_cparams(("arbitrary",)),
        name="expert_mlp",
    )(tile_expert, n_tiles_used, row_ids, h, cw, w1, w3, w2)


def _moe_combine_kernel(mode, pos_ref, ys_ref, res_ref, gate_ref, g_ref, *refs):
    if mode == "next_norm":
        sh_ref, sc_ref, o_ref, h_ref, buf_ref, sem = refs
    else:
        o_ref, buf_ref, sem = refs
    i = pl.program_id(0)
    n = pl.num_programs(0)
    slot = i % 2

    def start(tile, s):
        for q in range(2):
            _start_row_gather(pos_ref, tile * (2 * ROW_TILE) + q, 2, ys_ref, buf_ref.at[s, q],
                              sem.at[s, q])

    @pl.when(i == 0)
    def _():
        start(0, 0)

    @pl.when(i + 1 < n)
    def _():
        start(i + 1, 1 - slot)

    for q in range(2):
        _wait_row_gather(ys_ref, buf_ref.at[slot, q], sem.at[slot, q])
    y = _from_row_major(buf_ref[slot, 0] + buf_ref[slot, 1], F32)
    x = res_ref[...] + gate_ref[...] * y
    if mode == "final_norm":
        ms = jnp.mean(x * x, axis=-1, keepdims=True)
        x = x * lax.rsqrt(ms + EPS) * g_ref[...]
    o_ref[...] = x
    if mode == "next_norm":
        ms = jnp.mean(x * x, axis=-1, keepdims=True)
        xn = x * lax.rsqrt(ms + EPS) * g_ref[...]
        h_ref[...] = (xn * (1.0 + sc_ref[...]) + sh_ref[...]).astype(h_ref.dtype)


def moe_combine(ys, pos, res, mods, layer, n_rows, n_tiles, row_fn, final_gain=None,
                next_gain=None):
    d = res.shape[1]
    _, s, l = ys.shape
    tile = pl.BlockSpec((ROW_TILE, d), lambda i, pos: (i, 0))
    vec = lambda m_layer, m: pl.BlockSpec(
        (None, 1, d), functools.partial(lambda f, i, pos: f(i), _mod_spec(m_layer, m, row_fn, n_rows, 1, 0)))
    mode, gain = "plain", jnp.ones((d,), F32)
    in_specs = [pl.BlockSpec(memory_space=pl.ANY), tile, vec(layer, 5),
                pl.BlockSpec((1, d), lambda i, pos: (0, 0))]
    out_specs, out_shape = tile, jax.ShapeDtypeStruct((n_tiles * ROW_TILE, d), F32)
    args = [mods]
    if final_gain is not None:
        mode, gain = "final_norm", final_gain
    elif next_gain is not None:
        mode, gain = "next_norm", next_gain
        in_specs += [vec(layer + 1, 0), vec(layer + 1, 1)]
        args += [mods, mods]
        out_specs = [tile, tile]
        out_shape = [out_shape, jax.ShapeDtypeStruct((n_tiles * ROW_TILE, d), BF16)]
    return pl.pallas_call(
        functools.partial(_moe_combine_kernel, mode),
        grid_spec=pltpu.PrefetchScalarGridSpec(
            num_scalar_prefetch=1,
            grid=(n_tiles,),
            in_specs=in_specs,
            out_specs=out_specs,
            scratch_shapes=[pltpu.VMEM((2, 2, ROW_TILE, s, l), F32),
                            pltpu.SemaphoreType.DMA((2, 2))],
        ),
        out_shape=out_shape,
        compiler_params=_cparams(("arbitrary",)),
        name="moe_combine",
    )(pos.reshape(-1), ys, res, args[0], gain.reshape(1, d), *args[1:])


def _dispatch_plan(route):
    t = route.shape[0]
    ids = route[:, 0:2].astype(jnp.int32)
    wts = route[:, 2:4]
    flat = ids.reshape(-1)
    n_assign = flat.shape[0]
    p_max = n_assign + N_EXPERTS * ROW_TILE
    p_max = (p_max // ROW_TILE) * ROW_TILE
    onehot = (flat[:, None] == jnp.arange(N_EXPERTS, dtype=jnp.int32)[None, :]).astype(jnp.int32)
    counts = jnp.sum(onehot, axis=0)
    padded = ((counts + ROW_TILE - 1) // ROW_TILE) * ROW_TILE
    ends = jnp.cumsum(padded)
    offs = ends - padded
    starts = jnp.cumsum(counts) - counts
    order = jnp.argsort(flat, stable=True).astype(jnp.int32)
    n_tiles = p_max // ROW_TILE
    tile_first = jnp.arange(n_tiles, dtype=jnp.int32) * ROW_TILE
    tile_e = jnp.sum((tile_first[:, None] >= ends[None, :]).astype(jnp.int32), axis=1)
    tile_e = jnp.minimum(tile_e, N_EXPERTS - 1)
    tile_rank = tile_first - offs[tile_e]
    within = jnp.arange(ROW_TILE, dtype=jnp.int32)[None, :]
    rank = tile_rank[:, None] + within
    valid = (rank < counts[tile_e][:, None]) & (tile_first[:, None] < ends[-1])
    src_sorted = jnp.clip(starts[tile_e][:, None] + rank, 0, n_assign - 1).reshape(-1)
    valid = valid.reshape(-1)
    assign = order[src_sorted]
    row_ids = jnp.where(valid, assign // 2, 0).astype(jnp.int32)
    cw = jnp.where(valid, wts.reshape(-1)[assign], 0.0).astype(F32)
    sorted_pos = jnp.argsort(order).astype(jnp.int32)
    shift = jnp.sum(onehot * (offs - starts)[None, :], axis=1)
    pos = (sorted_pos + shift).astype(jnp.int32).reshape(t, 2)
    n_tiles_used = (ends[-1] // ROW_TILE).astype(jnp.int32).reshape(1)
    last_e = tile_e[jnp.maximum(n_tiles_used[0] - 1, 0)]
    tile_idx = jnp.arange(n_tiles, dtype=jnp.int32)
    tile_expert = jnp.where(tile_idx < n_tiles_used[0], tile_e, last_e).astype(jnp.int32)
    return row_ids, cw, pos, tile_expert, n_tiles_used


def hier_moe_block(h, route, w1, w3, w2):
    row_ids, cw, pos, tile_expert, n_tiles_used = _dispatch_plan(route)
    ys = expert_mlp(h, row_ids, cw[:, None], w1, w3, w2, tile_expert, n_tiles_used)
    return ys, pos


def _rope_tables(geom):
    n_lat = geom.n_lat
    half = ATT_HEAD_DIM // 2
    nfreq = half // 2
    row = jnp.repeat(jnp.arange(n_lat // GRID_W, dtype=F32), GRID_W)
    col = (jnp.arange(n_lat) % GRID_W).astype(F32)
    inv = ROPE_THETA ** (-jnp.arange(nfreq, dtype=F32) / nfreq)
    ang = jnp.concatenate([row[:, None] * inv, col[:, None] * inv], axis=-1)
    cos, sin = jnp.cos(ang), jnp.sin(ang)
    cosf = jnp.concatenate([cos, cos], axis=-1)
    sinf = jnp.concatenate([-sin, sin], axis=-1)
    cosf = jnp.concatenate([jnp.ones((geom.lc, ATT_HEAD_DIM), F32), cosf], axis=0)
    sinf = jnp.concatenate([jnp.zeros((geom.lc, ATT_HEAD_DIM), F32), sinf], axis=0)
    return cosf, sinf


def _router_weights(router_g, router_e):
    d = router_g.shape[0]
    pad = jnp.zeros((d, LANES - MOE_GROUPS - N_EXPERTS), F32)
    return jnp.concatenate([router_g, router_e, pad], axis=1)


def mixer_even(xs, mods, layer, j, n_rows, geom, norm_g, w_in, conv_dw, conv_ln_g, conv_ln_b,
               ssm_conv_w, ssm_conv_b, ssm_dt_bias, ssm_a_log, ssm_d, ssm_norm_g, w_out,
               debug=False):
    bsz, per = geom.bsz, geom.lc + geom.n_lat
    conv_d = conv_dw.shape[1]
    ssm_dd = SSM_HEADS * SSM_HEAD_DIM
    xbc_dim = ssm_conv_w.shape[1]
    h = norm_mod(xs, norm_g, mods, layer, 0, n_rows, geom.n_all, geom.all_block, geom.all_mod_row)
    c0, c1, c2, c3 = conv_d, 2 * conv_d, 2 * conv_d + ssm_dd, 2 * conv_d + ssm_dd + xbc_dim
    glu = matmul_glu(h, w_in, 0, c0, conv_d, F32)
    z = matmul(h, w_in, c1, ssm_dd, BF16)
    xbc_raw = matmul(h, w_in, c2, xbc_dim, F32)
    w_dt = jnp.pad(w_in[:, c3:], ((0, 0), (0, LANES - 2 * SSM_HEADS)))
    dt_raw = matmul(h, w_dt, 0, LANES, F32)[:, :2 * SSM_HEADS]
    conf = seq_conv(glu, conv_dw, conv_ln_g, conv_ln_b, "layernorm_silu", geom)
    xbc = seq_conv(xbc_raw, ssm_conv_w, ssm_conv_b, ssm_conv_b, "bias_silu", geom)
    dtr = dt_raw.reshape(bsz, per, 2, SSM_GROUPS, 4).transpose(0, 2, 3, 4, 1)
    dtr = jnp.pad(dtr, ((0, 0), (0, 0), (0, 0), (0, 4), (0, 0)))

    def head_rows(v):
        v = v.astype(F32).reshape(2, SSM_GROUPS, 4)
        v = jnp.pad(v, ((0, 0), (0, 0), (0, 4)))
        return jnp.broadcast_to(v[..., None], (2, SSM_GROUPS, 8, LANES))

    y2 = ssd_scan(xbc, dtr, head_rows(ssm_dt_bias), head_rows(-jnp.exp(ssm_a_log.astype(F32))), geom)
    yn = ssd_merge(y2, xbc, z, jnp.repeat(ssm_d, SSM_HEAD_DIM), ssm_norm_g, geom)
    out = matmul_residual([conf, yn], w_out, xs, mods, layer, 2, n_rows, geom.n_all,
                          geom.all_block, geom.all_mod_row)
    if debug:
        return out, dict(h=h, glu=glu, z=z, xbc_raw=xbc_raw, dt_raw=dt_raw, conf=conf, xbc=xbc,
                         y2=y2, yn=yn)
    return out


def mixer_odd(xs, mods, layer, n_rows, geom, norm_g, w_qkv, q_norm_g, k_norm_g, w_o, h=None):
    if h is None:
        h = norm_mod(xs, norm_g, mods, layer, 0, n_rows, geom.n_all, geom.all_block,
                     geom.all_mod_row)
    qkv = matmul(h, w_qkv, 0, w_qkv.shape[1], BF16)
    cosf, sinf = _rope_tables(geom)
    q, k = qk_prep(qkv, cosf, sinf, q_norm_g, k_norm_g, geom)
    att = attention(q, k, qkv, geom)
    return matmul_residual([att], w_o, xs, mods, layer, 2, n_rows, geom.n_lat_tiles,
                           geom.lat_block, geom.lat_mod_row)


def moe_layer(xs, mods, layer, n_rows, n_tiles, row_fn, norm_g, router_g, router_e, w1, w3, w2,
              final_gain, next_gain=None):
    hm, route = norm_mod(xs, norm_g, mods, layer, 3, n_rows, n_tiles, lambda t: t, row_fn,
                         router_w=_router_weights(router_g, router_e))
    w1 = cast_experts(w1, layer)
    w3 = cast_experts(w3, layer)
    w2 = cast_experts(w2, layer)
    ys, pos = hier_moe_block(hm, route, w1, w3, w2)
    return moe_combine(ys, pos, xs, mods, layer, n_rows, n_tiles, row_fn, final_gain=final_gain,
                       next_gain=next_gain)


def kernel(x, c, ctx, c_ctx, w_ada, b_ada, norm_mix, norm_ffn, w_in0, conv_dw, conv_ln_g, conv_ln_b, ssm_conv_w, ssm_conv_b, ssm_dt_bias, ssm_a_log, ssm_d, ssm_norm_g, w_out0, w_qkv, q_norm_g, k_norm_g, w_o, moe_router_g, moe_router_e, moe_w1, moe_w3, moe_w2, norm_final):
    bsz, n_lat, d = x.shape
    lc = ctx.shape[1]
    depth = w_ada.shape[0]
    geom = Geom(bsz, lc, n_lat)
    n_rows = 16

    cvec = jnp.concatenate([c, c_ctx[None, :], jnp.zeros((n_rows - bsz - 1, d), F32)], axis=0)
    mods = ada_mods(cvec, w_ada, b_ada).reshape(depth * n_rows * N_MOD, 1, d)
    xs = (ctx.reshape(bsz * lc, d), x.reshape(bsz * n_lat, d), geom)

    h_next = None
    for i in range(depth):
        last = i == depth - 1
        j = i // 2
        if i % 2 == 0:
            if last:
                raise NotImplementedError("a conv/SSD mixer in the last layer is not supported")
            xs = mixer_even(xs, mods, i, j, n_rows, geom, norm_mix[i], w_in0[j], conv_dw[j],
                            conv_ln_g[j], conv_ln_b[j], ssm_conv_w[j], ssm_conv_b[j],
                            ssm_dt_bias[j], ssm_a_log[j], ssm_d[j], ssm_norm_g[j], w_out0[j])
            n_tiles, row_fn = geom.n_all, geom.all_mod_row
        else:
            if not last:
                raise NotImplementedError("an attention mixer before the last layer is not supported")
            xs = mixer_odd(xs, mods, i, n_rows, geom, norm_mix[i], w_qkv[j], q_norm_g[j],
                           k_norm_g[j], w_o[j], h=h_next)
            n_tiles, row_fn = geom.n_lat_tiles, geom.lat_mod_row
        feeds_next = not last and (i + 1) % 2 == 1
        out = moe_layer(xs, mods, i, n_rows, n_tiles, row_fn, norm_ffn[i], moe_router_g[i],
                        moe_router_e[i], moe_w1, moe_w3, moe_w2, norm_final if last else None,
                        next_gain=norm_mix[i + 1] if feeds_next else None)
        xs, h_next = out if feeds_next else (out, None)
    return xs.reshape(bsz, n_lat, d)
```

```python
import functools

import jax
import jax.numpy as jnp
from jax import lax
from jax.experimental import pallas as pl
from jax.experimental.pallas import tpu as pltpu

F32 = jnp.float32
BF16 = jnp.bfloat16

EPS = 1e-6
N_MOD = 6
GRID_W = 64
ROPE_THETA = 10000.0
CONV_WIDTH = 31
SSM_CONV_WIDTH = 5
SSM_HEADS = 32
SSM_HEAD_DIM = 64
SSM_GROUPS = 8
SSM_STATE = 128
SSM_CHUNK = 128
ATT_HEADS = 32
ATT_KV_HEADS = 8
ATT_HEAD_DIM = 128
MOE_GROUPS = 4
MOE_EXPERTS = 8
N_EXPERTS = MOE_GROUPS * MOE_EXPERTS

LANES = 128
SUBLANES = 8
ROW_TILE = 256
HALO = 16
VMEM_LIMIT = 56 * 1024 * 1024


def _cparams(sem):
    return pltpu.CompilerParams(dimension_semantics=sem, vmem_limit_bytes=VMEM_LIMIT)


def _sigmoid(v):
    return 0.5 * (jnp.tanh(0.5 * v) + 1.0)


def _silu(v):
    return v * _sigmoid(v)


def _ada_kernel(c_ref, w_ref, b_ref, o_ref):
    s = _silu(c_ref[...]).astype(BF16)
    w = w_ref[...].astype(BF16)
    o_ref[...] = jnp.dot(s, w, preferred_element_type=F32) + b_ref[...]


def ada_mods(cvec, w_ada, b_ada, tn=512):
    depth, d, n = w_ada.shape
    rows = cvec.shape[0]
    return pl.pallas_call(
        _ada_kernel,
        grid=(depth, n // tn),
        in_specs=[
            pl.BlockSpec((rows, d), lambda l, j: (0, 0)),
            pl.BlockSpec((None, d, tn), lambda l, j: (l, 0, j)),
            pl.BlockSpec((None, 1, tn), lambda l, j: (l, 0, j)),
        ],
        out_specs=pl.BlockSpec((None, rows, tn), lambda l, j: (l, 0, j)),
        out_shape=jax.ShapeDtypeStruct((depth, rows, n), F32),
        compiler_params=_cparams(("arbitrary", "arbitrary")),
        name="ada_mods",
    )(cvec, w_ada, b_ada.reshape(depth, 1, n))


class Geom:
    def __init__(self, bsz, lc, n_lat):
        self.bsz, self.lc, self.n_lat = bsz, lc, n_lat
        self.per_b = (lc + n_lat) // ROW_TILE
        self.ctx_t = lc // ROW_TILE
        self.lat_t = n_lat // ROW_TILE
        self.n_all = bsz * self.per_b
        self.n_lat_tiles = bsz * self.lat_t

    def all_block(self, i):
        return i

    def lat_block(self, i):
        return (i // self.lat_t) * self.per_b + self.ctx_t + i % self.lat_t

    def all_mod_row(self, i):
        return jnp.where(i % self.per_b < self.ctx_t, self.bsz, i // self.per_b)

    def lat_mod_row(self, i):
        return i // self.lat_t

    def ctx_src_block(self, i):
        return (i // self.per_b) * self.ctx_t + jnp.minimum(i % self.per_b, self.ctx_t - 1)

    def lat_src_block(self, i):
        return (i // self.per_b) * self.lat_t + jnp.clip(i % self.per_b - self.ctx_t, 0,
                                                         self.lat_t - 1)

    def tile_is_ctx(self, i):
        return i % self.per_b < self.ctx_t


def _mod_spec(layer, m, row_fn, n_rows, grid_rank, tile_axis, width=None, col_axis=None):
    def imap(*g):
        r = row_fn(g[tile_axis])
        c = 0 if col_axis is None else g[col_axis]
        return ((layer * n_rows + r) * N_MOD + m, 0, c)
    return imap


def _norm_mod_kernel(geom, *refs):
    if geom is None:
        x_ref, g_ref, sh_ref, sc_ref, h_ref = refs
        x = x_ref[...]
    else:
        xc_ref, xl_ref, g_ref, sh_ref, sc_ref, h_ref = refs
        x = jnp.where(geom.tile_is_ctx(pl.program_id(0)), xc_ref[...], xl_ref[...])
    ms = jnp.mean(x * x, axis=-1, keepdims=True)
    xn = x * lax.rsqrt(ms + EPS) * g_ref[...]
    h_ref[...] = (xn * (1.0 + sc_ref[...]) + sh_ref[...]).astype(h_ref.dtype)


def _route(logits):
    lane = lax.broadcasted_iota(jnp.int32, logits.shape, 1)
    neg = jnp.float32(-jnp.inf)
    big = jnp.int32(LANES)
    lg = jnp.where(lane < MOE_GROUPS, logits, neg)
    mg = jnp.max(lg, axis=-1, keepdims=True)
    gsel = jnp.min(jnp.where(lg == mg, lane, big), axis=-1, keepdims=True)
    pg = 1.0 / jnp.sum(jnp.exp(lg - mg), axis=-1, keepdims=True)
    lo = MOE_GROUPS + gsel * MOE_EXPERTS
    le = jnp.where((lane >= lo) & (lane < lo + MOE_EXPERTS), logits, neg)
    v1 = jnp.max(le, axis=-1, keepdims=True)
    i1 = jnp.min(jnp.where(le == v1, lane, big), axis=-1, keepdims=True)
    le2 = jnp.where(lane == i1, neg, le)
    v2 = jnp.max(le2, axis=-1, keepdims=True)
    i2 = jnp.min(jnp.where(le2 == v2, lane, big), axis=-1, keepdims=True)
    e2 = jnp.exp(v2 - v1)
    w1 = pg / (1.0 + e2)
    w2 = pg * e2 / (1.0 + e2)
    id1 = (i1 - MOE_GROUPS).astype(F32)
    id2 = (i2 - MOE_GROUPS).astype(F32)
    return jnp.where(lane == 0, id1, jnp.where(lane == 1, id2,
                     jnp.where(lane == 2, w1, jnp.where(lane == 3, w2, 0.0))))


def _to_row_major(x):
    parts = [x[:, s * LANES:(s + 1) * LANES] for s in range(x.shape[1] // LANES)]
    return pltpu.einshape("stl->tsl", jnp.stack(parts, axis=0))


def _from_row_major(x3, dtype):
    xt = pltpu.einshape("tsl->stl", x3)
    return jnp.concatenate([xt[s].astype(dtype) for s in range(x3.shape[1])], axis=1)


def _norm_mod_route_kernel(x_ref, g_ref, sh_ref, sc_ref, wr_ref, h_ref, r_ref):
    x = x_ref[...]
    ms = jnp.mean(x * x, axis=-1, keepdims=True)
    xn = x * lax.rsqrt(ms + EPS) * g_ref[...]
    h = xn * (1.0 + sc_ref[...]) + sh_ref[...]
    h_ref[...] = h
    logits = jnp.dot(h, wr_ref[...], preferred_element_type=F32,
                     precision=lax.Precision.HIGHEST)
    r_ref[...] = _route(logits)


def norm_mod(x, gain, mods, layer, m_shift, n_rows, n_tiles, block_fn, row_fn,
             router_w=None, out_dtype=BF16):
    split = None
    if isinstance(x, tuple):
        xc, xl, split = x
        d = xc.shape[1]
        x_specs = [pl.BlockSpec((ROW_TILE, d), lambda i: (split.ctx_src_block(i), 0)),
                   pl.BlockSpec((ROW_TILE, d), lambda i: (split.lat_src_block(i), 0))]
        x_args = [xc, xl]
        assert router_w is None
    else:
        d = x.shape[1]
        x_specs = [pl.BlockSpec((ROW_TILE, d), lambda i: (block_fn(i), 0))]
        x_args = [x]
    x_spec = x_specs[0]
    g_spec = pl.BlockSpec((1, d), lambda i: (0, 0))
    sh_spec = pl.BlockSpec((None, 1, d), _mod_spec(layer, m_shift, row_fn, n_rows, 1, 0))
    sc_spec = pl.BlockSpec((None, 1, d), _mod_spec(layer, m_shift + 1, row_fn, n_rows, 1, 0))
    o_spec = pl.BlockSpec((ROW_TILE, d), lambda i: (i, 0))
    rows = n_tiles * ROW_TILE
    if router_w is None:
        return pl.pallas_call(
            functools.partial(_norm_mod_kernel, split),
            grid=(n_tiles,),
            in_specs=x_specs + [g_spec, sh_spec, sc_spec],
            out_specs=o_spec,
            out_shape=jax.ShapeDtypeStruct((rows, d), out_dtype),
            compiler_params=_cparams(("parallel",)),
            name="norm_mod",
        )(*x_args, gain.reshape(1, d), mods, mods)
    return pl.pallas_call(
        _norm_mod_route_kernel,
        grid=(n_tiles,),
        in_specs=[x_spec, g_spec, sh_spec, sc_spec,
                  pl.BlockSpec((d, LANES), lambda i: (0, 0))],
        out_specs=[o_spec, pl.BlockSpec((ROW_TILE, LANES), lambda i: (i, 0))],
        out_shape=[jax.ShapeDtypeStruct((rows, d), F32),
                   jax.ShapeDtypeStruct((rows, LANES), F32)],
        compiler_params=_cparams(("parallel",)),
        name="norm_mod_route",
    )(x, gain.reshape(1, d), mods, mods, router_w)


def _stage_weights(w_refs, wb_refs):
    @pl.when(pl.program_id(1) == 0)
    def _():
        for w_ref, wb_ref in zip(w_refs, wb_refs):
            wb_ref[...] = w_ref[...].astype(BF16)


def _mm_kernel(x_ref, w_ref, o_ref, wb_ref):
    _stage_weights([w_ref], [wb_ref])
    o_ref[...] = jnp.dot(x_ref[...], wb_ref[...], preferred_element_type=F32).astype(o_ref.dtype)


def _row_block(m, want=1024):
    tm = want
    while m % tm:
        tm //= 2
    return tm


def matmul(x, w, col0, n, out_dtype, tn=512):
    m, k = x.shape
    tm = _row_block(m)
    tn = min(tn, n)
    cb = col0 // tn
    return pl.pallas_call(
        _mm_kernel,
        grid=(n // tn, m // tm),
        in_specs=[pl.BlockSpec((tm, k), lambda j, i: (i, 0)),
                  pl.BlockSpec((k, tn), lambda j, i: (0, cb + j))],
        out_specs=pl.BlockSpec((tm, tn), lambda j, i: (i, j)),
        out_shape=jax.ShapeDtypeStruct((m, n), out_dtype),
        scratch_shapes=[pltpu.VMEM((k, tn), BF16)],
        compiler_params=_cparams(("parallel", "arbitrary")),
        name="matmul",
    )(x, w)


def _mm_glu_kernel(x_ref, wa_ref, wg_ref, o_ref, wab_ref, wgb_ref):
    _stage_weights([wa_ref, wg_ref], [wab_ref, wgb_ref])
    x = x_ref[...]
    a = jnp.dot(x, wab_ref[...], preferred_element_type=F32)
    b = jnp.dot(x, wgb_ref[...], preferred_element_type=F32)
    o_ref[...] = (a * _sigmoid(b)).astype(o_ref.dtype)


def matmul_glu(x, w, col_a, col_g, n, out_dtype, tn=256):
    m, k = x.shape
    tm = _row_block(m)
    ca, cg = col_a // tn, col_g // tn
    return pl.pallas_call(
        _mm_glu_kernel,
        grid=(n // tn, m // tm),
        in_specs=[pl.BlockSpec((tm, k), lambda j, i: (i, 0)),
                  pl.BlockSpec((k, tn), lambda j, i: (0, ca + j)),
                  pl.BlockSpec((k, tn), lambda j, i: (0, cg + j))],
        out_specs=pl.BlockSpec((tm, tn), lambda j, i: (i, j)),
        out_shape=jax.ShapeDtypeStruct((m, n), out_dtype),
        scratch_shapes=[pltpu.VMEM((k, tn), BF16), pltpu.VMEM((k, tn), BF16)],
        compiler_params=_cparams(("parallel", "arbitrary")),
        name="matmul_glu",
    )(x, w, w)


def _mm_res_kernel(n_lhs, split, *refs):
    xs = refs[:n_lhs]
    ws = refs[n_lhs:2 * n_lhs]
    n_res = 1 if split is None else 2
    res_refs = refs[2 * n_lhs:2 * n_lhs + n_res]
    gate_ref, o_ref = refs[2 * n_lhs + n_res:2 * n_lhs + n_res + 2]
    wbs = refs[2 * n_lhs + n_res + 2:]
    _stage_weights(ws, wbs)
    acc = jnp.dot(xs[0][...], wbs[0][...], preferred_element_type=F32)
    for x_ref, wb_ref in zip(xs[1:], wbs[1:]):
        acc = acc + jnp.dot(x_ref[...], wb_ref[...], preferred_element_type=F32)
    if split is None:
        res = res_refs[0][...]
    else:
        res = jnp.where(split.tile_is_ctx(pl.program_id(1)), res_refs[0][...], res_refs[1][...])
    o_ref[...] = res + gate_ref[...] * acc


def matmul_residual(lhs, w, res, mods, layer, m_gate, n_rows, n_tiles, res_block_fn, row_fn,
                    tn=1024):
    n_lhs = len(lhs)
    n = w.shape[1]
    widths = [a.shape[1] for a in lhs]
    assert all(wd == widths[0] for wd in widths) and sum(widths) == w.shape[0]
    in_specs = [pl.BlockSpec((ROW_TILE, wd), lambda j, i: (i, 0)) for wd in widths]
    in_specs += [pl.BlockSpec((wd, tn), functools.partial(lambda q, j, i: (q, j), q),
                              pipeline_mode=pl.Buffered(1))
                 for q, wd in enumerate(widths)]
    split = None
    if isinstance(res, tuple):
        rc, rl, split = res
        in_specs += [pl.BlockSpec((ROW_TILE, tn), lambda j, i: (split.ctx_src_block(i), j)),
                     pl.BlockSpec((ROW_TILE, tn), lambda j, i: (split.lat_src_block(i), j))]
        res_args = [rc, rl]
    else:
        in_specs.append(pl.BlockSpec((ROW_TILE, tn), lambda j, i: (res_block_fn(i), j)))
        res_args = [res]
    in_specs.append(
        pl.BlockSpec((None, 1, tn), _mod_spec(layer, m_gate, row_fn, n_rows, 2, 1, col_axis=0)))
    return pl.pallas_call(
        functools.partial(_mm_res_kernel, n_lhs, split),
        grid=(n // tn, n_tiles),
        in_specs=in_specs,
        out_specs=pl.BlockSpec((ROW_TILE, tn), lambda j, i: (i, j)),
        out_shape=jax.ShapeDtypeStruct((n_tiles * ROW_TILE, n), F32),
        scratch_shapes=[pltpu.VMEM((wd, tn), BF16) for wd in widths],
        compiler_params=_cparams(("parallel", "arbitrary")),
        name="matmul_residual",
    )(*lhs, *([w] * n_lhs), *res_args, mods)


def _conv_kernel(width, mode, per_b, ctx_t, x_ref, prev_ref, next_ref, w_ref, p0_ref, p1_ref,
                 o_ref, xp_ref, acc_ref, sh_ref):
    t = pl.program_id(0) % per_b
    first = (t == 0) | (t == ctx_t)
    last = (t == ctx_t - 1) | (t == per_b - 1)
    c = x_ref.shape[1]
    xp_ref[0:HALO, :] = jnp.where(first, 0.0, prev_ref[...])
    xp_ref[HALO:HALO + ROW_TILE, :] = x_ref[...]
    xp_ref[HALO + ROW_TILE:, :] = jnp.where(last, 0.0, next_ref[...])
    off = HALO - width // 2
    sub = SUBLANES
    taps_of = [[k for k in range(width) if (off + k) % sub == ph] for ph in range(sub)]
    span = ROW_TILE + sub * ((off + width - 1) // sub)

    def chunk(j, carry):
        cs = pl.ds(pl.multiple_of(j * LANES, LANES), LANES)
        for ph in range(1, sub):
            if taps_of[ph]:
                sh_ref[ph, 0:span, :] = xp_ref[ph:ph + span, cs]
        acc = jnp.zeros((ROW_TILE, LANES), F32)
        for ph in range(sub):
            for k in taps_of[ph]:
                lo = off + k - ph
                if ph == 0:
                    win = xp_ref[lo:lo + ROW_TILE, cs]
                else:
                    win = sh_ref[ph, lo:lo + ROW_TILE, :]
                acc = acc + win * w_ref[k:k + 1, cs]
        acc_ref[:, cs] = acc
        return carry

    lax.fori_loop(0, c // LANES, chunk, 0)
    y = acc_ref[...]
    if mode == "layernorm_silu":
        mu = jnp.mean(y, axis=-1, keepdims=True)
        yc = y - mu
        var = jnp.mean(yc * yc, axis=-1, keepdims=True)
        y = yc * lax.rsqrt(var + EPS) * p0_ref[...] + p1_ref[...]
    else:
        y = y + p0_ref[...]
    o_ref[...] = _silu(y).astype(o_ref.dtype)


def seq_conv(x, w, p0, p1, mode, geom, out_dtype=BF16):
    rows, c = x.shape
    width = w.shape[0]
    hb = ROW_TILE // HALO
    n_halo = rows // HALO
    kern = functools.partial(_conv_kernel, width, mode, geom.per_b, geom.ctx_t)
    return pl.pallas_call(
        kern,
        grid=(geom.n_all,),
        in_specs=[
            pl.BlockSpec((ROW_TILE, c), lambda i: (i, 0)),
            pl.BlockSpec((HALO, c), lambda i: (jnp.maximum(i * hb - 1, 0), 0)),
            pl.BlockSpec((HALO, c), lambda i: (jnp.minimum((i + 1) * hb, n_halo - 1), 0)),
            pl.BlockSpec((width, c), lambda i: (0, 0)),
            pl.BlockSpec((1, c), lambda i: (0, 0)),
            pl.BlockSpec((1, c), lambda i: (0, 0)),
        ],
        out_specs=pl.BlockSpec((ROW_TILE, c), lambda i: (i, 0)),
        out_shape=jax.ShapeDtypeStruct((rows, c), out_dtype),
        scratch_shapes=[pltpu.VMEM((ROW_TILE + 2 * HALO, c), F32),
                        pltpu.VMEM((ROW_TILE, c), F32),
                        pltpu.VMEM((SUBLANES, ROW_TILE + 2 * HALO, LANES), F32)],
        compiler_params=_cparams(("parallel",)),
        name="seq_conv_" + mode,
    )(x, x, x, w, p0.reshape(1, c), p1.reshape(1, c))


def _ssd_kernel(n_chunks, ctx_chunks, xs_ref, b_ref, c_ref, dtraw_ref, bias_ref, aneg_ref,
                y_ref, state_ref, tr_ref, row_ref, col_ref, cb_ref, bt_ref):
    T = SSM_CHUNK
    rows = lax.broadcasted_iota(jnp.int32, (T, T), 0)
    cols = lax.broadcasted_iota(jnp.int32, (T, T), 1)
    lane = lax.broadcasted_iota(jnp.int32, (T, T), 1)
    left = lane < SSM_HEAD_DIM
    state_ref[...] = jnp.zeros_like(state_ref)
    tr_ref[...] = jnp.zeros_like(tr_ref)
    n_heads = 4

    def row_of(d, s):
        s = jnp.minimum(s, n_chunks - 1)
        if d == 1:
            s = jnp.where(s < ctx_chunks, ctx_chunks - 1 - s, n_chunks - 1 + ctx_chunks - s)
        return pl.multiple_of(s * T, T)

    def prep(d, s, slot):
        diff = rows - cols if d == 0 else cols - rows
        tri = jnp.where(diff <= 0, 1.0, 0.0).astype(F32)
        r0 = row_of(d, s)
        raw = dtraw_ref[d, :, pl.ds(r0, T)]
        z = raw + bias_ref[d]
        dt_row = jnp.maximum(z, 0.0) + jnp.log1p(jnp.exp(-jnp.abs(z)))
        da_row = dt_row * aneg_ref[d]
        cum_row = jnp.dot(da_row, tri, preferred_element_type=F32,
                          precision=lax.Precision.HIGHEST)
        tot_row = jnp.sum(da_row, axis=-1, keepdims=True)
        tr_ref[d, 0:8, :] = cum_row
        row_ref[slot, d, 0:8, :] = dt_row
        row_ref[slot, d, 8:16, :] = cum_row
        row_ref[slot, d, 16:24, :] = jnp.broadcast_to(tot_row, (8, T))
        col_ref[slot, d] = tr_ref[d].T
        bmat = b_ref[pl.ds(r0, T), :]
        cmat = c_ref[pl.ds(r0, T), :]
        cb_ref[slot, d] = lax.dot_general(cmat, bmat, (((1,), (1,)), ((), ())),
                                          preferred_element_type=F32)
        bt_ref[slot, d] = bmat.astype(F32).T

    def main(d, s, slot):
        diff = rows - cols if d == 0 else cols - rows
        keep = diff >= 0
        r0 = row_of(d, s)
        dt_row = row_ref[slot, d, 0:8, :]
        cum_row = row_ref[slot, d, 8:16, :]
        tot_row = row_ref[slot, d, 16:24, 0:1]
        colform = col_ref[slot, d]
        cb = cb_ref[slot, d]
        bt = bt_ref[slot, d]
        cmat = c_ref[pl.ds(r0, T), :]
        for pair in range(n_heads // 2):
            xs_pair = xs_ref[pl.ds(r0, T), pair * LANES:(pair + 1) * LANES]
            ms, bds, ecs, ets = [], [], [], []
            for q in range(2):
                r = pair * 2 + q
                cum_c = jnp.broadcast_to(colform[:, r:r + 1], (T, T))
                cum_r = cum_row[r:r + 1, :]
                dt_r = dt_row[r:r + 1, :]
                seg = jnp.where(keep, cum_c - cum_r, -jnp.inf)
                ms.append((cb * jnp.exp(seg) * dt_r).astype(BF16))
                dec_r = jnp.exp(tot_row[r:r + 1, :] - cum_r) * dt_r
                bds.append((bt * dec_r).astype(BF16))
                ecs.append(jnp.exp(cum_c))
                ets.append(jnp.exp(tot_row[r:r + 1, :]))
            xs32 = xs_pair.astype(F32)
            xd2 = jnp.concatenate([jnp.where(left, xs32, 0.0).astype(BF16),
                                   jnp.where(left, 0.0, xs32).astype(BF16)], axis=0)
            m2 = jnp.concatenate(ms, axis=1)
            y_diag = jnp.dot(m2, xd2, preferred_element_type=F32)
            st = state_ref[d, pair]
            y_off = jnp.dot(cmat, st.astype(BF16), preferred_element_type=F32)
            y_off = y_off * jnp.where(left, ecs[0], ecs[1])
            y_ref[d, pl.ds(r0, T), pair * LANES:(pair + 1) * LANES] = (y_diag + y_off).astype(y_ref.dtype)
            bd2 = jnp.concatenate(bds, axis=1)
            upd = jnp.dot(bd2, xd2, preferred_element_type=F32)
            state_ref[d, pair] = st * jnp.where(left[0:1, :], ets[0], ets[1]) + upd

    def two_steps(i, carry):
        for k in range(2):
            for d in range(2):
                main(d, 2 * i + k, k)
            for d in range(2):
                prep(d, 2 * i + k + 1, 1 - k)
        return carry

    assert n_chunks % 2 == 0
    for d in range(2):
        prep(d, 0, 0)
    lax.fori_loop(0, n_chunks // 2, two_steps, 0)


def ssd_scan(xbc, dtraw_rows, bias_rows, aneg_rows, geom):
    rows = xbc.shape[0]
    per = geom.lc + geom.n_lat
    n_chunks = per // SSM_CHUNK
    ctx_chunks = geom.lc // SSM_CHUNK
    ssm_d = SSM_HEADS * SSM_HEAD_DIM
    gw = 4 * SSM_HEAD_DIM
    xs_blocks = ssm_d // gw
    kern = functools.partial(_ssd_kernel, n_chunks, ctx_chunks)
    return pl.pallas_call(
        kern,
        grid=(geom.bsz, SSM_GROUPS),
        in_specs=[
            pl.BlockSpec((per, gw), lambda b, g: (b, g)),
            pl.BlockSpec((per, SSM_STATE), lambda b, g: (b, 2 * xs_blocks + g)),
            pl.BlockSpec((per, SSM_STATE), lambda b, g: (b, 2 * xs_blocks + SSM_GROUPS + g)),
            pl.BlockSpec((None, 2, None, 8, per), lambda b, g: (b, 0, g, 0, 0)),
            pl.BlockSpec((2, None, 8, LANES), lambda b, g: (0, g, 0, 0)),
            pl.BlockSpec((2, None, 8, LANES), lambda b, g: (0, g, 0, 0)),
        ],
        out_specs=pl.BlockSpec((2, per, gw), lambda b, g: (0, b, g)),
        out_shape=jax.ShapeDtypeStruct((2, rows, ssm_d), BF16),
        scratch_shapes=[pltpu.VMEM((2, 2, SSM_STATE, LANES), F32),
                        pltpu.VMEM((2, LANES, SSM_CHUNK), F32),
                        pltpu.VMEM((2, 2, 24, SSM_CHUNK), F32),
                        pltpu.VMEM((2, 2, SSM_CHUNK, LANES), F32),
                        pltpu.VMEM((2, 2, SSM_CHUNK, SSM_CHUNK), F32),
                        pltpu.VMEM((2, 2, SSM_STATE, SSM_CHUNK), F32)],
        compiler_params=_cparams(("parallel", "parallel")),
        name="ssd_scan",
    )(xbc, xbc, xbc, dtraw_rows, bias_rows, aneg_rows)


def _ssd_merge_kernel(yf_ref, yb_ref, xs_ref, z_ref, d_ref, g_ref, o_ref):
    y = d_ref[...] * xs_ref[...].astype(F32) + yf_ref[...].astype(F32) + yb_ref[...].astype(F32)
    gated = y * _silu(z_ref[...].astype(F32))
    ms = jnp.mean(gated * gated, axis=-1, keepdims=True)
    o_ref[...] = (gated * lax.rsqrt(ms + EPS) * g_ref[...]).astype(o_ref.dtype)


def ssd_merge(y2, xbc, z, d_full, g, geom):
    rows, ssm_d = z.shape
    spec = pl.BlockSpec((ROW_TILE, ssm_d), lambda i: (i, 0))
    vec = pl.BlockSpec((1, ssm_d), lambda i: (0, 0))
    return pl.pallas_call(
        _ssd_merge_kernel,
        grid=(geom.n_all,),
        in_specs=[pl.BlockSpec((None, ROW_TILE, ssm_d), lambda i: (0, i, 0)),
                  pl.BlockSpec((None, ROW_TILE, ssm_d), lambda i: (1, i, 0)),
                  spec, spec, vec, vec],
        out_specs=spec,
        out_shape=jax.ShapeDtypeStruct((rows, ssm_d), BF16),
        compiler_params=_cparams(("parallel",)),
        name="ssd_merge",
    )(y2, y2, xbc, z, d_full.reshape(1, ssm_d), g.reshape(1, ssm_d))


def _qk_prep_kernel(n_q, n_k, qkv_ref, aq_ref, bq_ref, ak_ref, bk_ref, q_ref, k_ref):
    hd = ATT_HEAD_DIM
    for h in range(n_q + n_k):
        x = qkv_ref[:, h * hd:(h + 1) * hd].astype(F32)
        a, b = (aq_ref[...], bq_ref[...]) if h < n_q else (ak_ref[...], bk_ref[...])
        inv = lax.rsqrt(jnp.mean(x * x, axis=-1, keepdims=True) + EPS)
        o = (x * a + pltpu.roll(x, hd // 2, axis=1) * b) * inv
        if h < n_q:
            q_ref[:, h * hd:(h + 1) * hd] = o.astype(q_ref.dtype)
        else:
            k_ref[:, (h - n_q) * hd:(h - n_q + 1) * hd] = o.astype(k_ref.dtype)


def qk_prep(qkv, cosf, sinf, gq, gk, geom):
    rows = qkv.shape[0]
    hd = ATT_HEAD_DIM
    qc, kc = ATT_HEADS * hd, ATT_KV_HEADS * hd
    scale = hd ** -0.5
    tables = [gq * cosf * scale, jnp.roll(gq, hd // 2) * sinf * scale,
              gk * cosf, jnp.roll(gk, hd // 2) * sinf]
    kern = functools.partial(_qk_prep_kernel, ATT_HEADS, ATT_KV_HEADS)
    per_b = geom.per_b
    tab = pl.BlockSpec((ROW_TILE, hd), lambda i: (i % per_b, 0))
    return pl.pallas_call(
        kern,
        grid=(geom.n_all,),
        in_specs=[pl.BlockSpec((ROW_TILE, qc + kc), lambda i: (i, 0)), tab, tab, tab, tab],
        out_specs=[pl.BlockSpec((ROW_TILE, qc), lambda i: (i, 0)),
                   pl.BlockSpec((ROW_TILE, kc), lambda i: (i, 0))],
        out_shape=[jax.ShapeDtypeStruct((rows, qc), BF16),
                   jax.ShapeDtypeStruct((rows, kc), BF16)],
        compiler_params=_cparams(("parallel",)),
        name="qk_prep",
    )(qkv, *tables)


def _attn_kernel(rep, q_ref, k_ref, v_ref, o_ref):
    hd = ATT_HEAD_DIM
    k = k_ref[...]
    v = v_ref[...]
    v1 = jnp.concatenate([v, jnp.ones_like(v)], axis=1)
    for r in range(rep):
        q = q_ref[:, r * hd:(r + 1) * hd]
        s = lax.dot_general(q, k, (((1,), (1,)), ((), ())), preferred_element_type=F32)
        m = jnp.max(s, axis=-1, keepdims=True)
        p = jnp.exp((s - m).astype(BF16))
        o = jnp.dot(p, v1, preferred_element_type=F32)
        o_ref[:, r * hd:(r + 1) * hd] = (o[:, :hd] / o[:, hd:hd + 1]).astype(o_ref.dtype)


def attention(q, k, qkv, geom):
    hd = ATT_HEAD_DIM
    rep = ATT_HEADS // ATT_KV_HEADS
    per = geom.lc + geom.n_lat
    v_col0 = (ATT_HEADS + ATT_KV_HEADS)
    lat_t, per_b, ctx_t = geom.lat_t, geom.per_b, geom.ctx_t
    return pl.pallas_call(
        functools.partial(_attn_kernel, rep),
        grid=(geom.bsz, ATT_KV_HEADS, lat_t),
        in_specs=[pl.BlockSpec((ROW_TILE, rep * hd), lambda b, g, t: (b * per_b + ctx_t + t, g)),
                  pl.BlockSpec((per, hd), lambda b, g, t: (b, g)),
                  pl.BlockSpec((per, hd), lambda b, g, t: (b, v_col0 + g))],
        out_specs=pl.BlockSpec((ROW_TILE, rep * hd), lambda b, g, t: (b * lat_t + t, g)),
        out_shape=jax.ShapeDtypeStruct((geom.bsz * geom.n_lat, ATT_HEADS * hd), BF16),
        compiler_params=_cparams(("parallel", "parallel", "parallel")),
        name="attention",
    )(q, k, qkv)


def _start_row_gather(ids_ref, first, stride, src_ref, dst_ref, sem, inline=False):
    def issue(r, carry):
        row = ids_ref[first + stride * r]
        pltpu.make_async_copy(src_ref.at[pl.ds(row, 1)], dst_ref.at[pl.ds(r, 1)], sem).start()
        return carry

    if inline:
        for r in range(ROW_TILE):
            issue(r, 0)
    else:
        lax.fori_loop(0, ROW_TILE, issue, 0, unroll=8)


def _wait_row_gather(src_ref, dst_ref, sem):
    pltpu.make_async_copy(src_ref.at[pl.ds(0, ROW_TILE)], dst_ref, sem).wait()


def _cast_kernel(x_ref, o_ref):
    o_ref[...] = x_ref[...].astype(o_ref.dtype)


def cast_experts(w_all, layer, dtype=BF16):
    depth, g, e, k, n = w_all.shape
    ne = g * e
    return pl.pallas_call(
        _cast_kernel,
        grid=(ne,),
        in_specs=[pl.BlockSpec((None, k, n), lambda i: (layer * ne + i, 0, 0))],
        out_specs=pl.BlockSpec((None, k, n), lambda i: (i, 0, 0)),
        out_shape=jax.ShapeDtypeStruct((ne, k, n), dtype),
        compiler_params=_cparams(("parallel",)),
        name="cast_experts",
    )(w_all.reshape(depth * ne, k, n))


def _expert_mlp_kernel(te_ref, nt_ref, ids_ref, h_ref, cw_ref, w1_ref, w3_ref, w2_ref, o_ref,
                       xbuf_ref, sem):
    t = pl.program_id(0)
    nt = nt_ref[0]
    slot = t % 2

    @pl.when(t == 0)
    def _():
        _start_row_gather(ids_ref, 0, 1, h_ref, xbuf_ref.at[0], sem.at[0])

    @pl.when(t < nt)
    def _():
        _wait_row_gather(h_ref, xbuf_ref.at[slot], sem.at[slot])
        nxt = jnp.minimum(t + 1, nt - 1)
        _start_row_gather(ids_ref, nxt * ROW_TILE, 1, h_ref, xbuf_ref.at[1 - slot],
                          sem.at[1 - slot], inline=True)
        x = xbuf_ref[slot].astype(BF16)
        a = jnp.dot(x, w1_ref[...], preferred_element_type=F32)
        b = jnp.dot(x, w3_ref[...], preferred_element_type=F32)
        hid = _silu(a) * b * cw_ref[...]
        y = jnp.dot(hid.astype(BF16), w2_ref[...], preferred_element_type=F32)
        o_ref[...] = _to_row_major(y)

        @pl.when(t + 1 == nt)
        def _():
            _wait_row_gather(h_ref, xbuf_ref.at[1 - slot], sem.at[1 - slot])

    @pl.when(t >= nt)
    def _():
        o_ref[...] = jnp.zeros_like(o_ref)


def expert_mlp(h, row_ids, cw, w1, w3, w2, tile_expert, n_tiles_used):
    p = row_ids.shape[0]
    d = h.shape[1]
    s, l = d // LANES, LANES
    hdim = w1.shape[2]
    n_tiles = p // ROW_TILE

    def wmap(t, te, nt, ids):
        return (te[t], 0, 0)

    return pl.pallas_call(
        _expert_mlp_kernel,
        grid_spec=pltpu.PrefetchScalarGridSpec(
            num_scalar_prefetch=3,
            grid=(n_tiles,),
            in_specs=[pl.BlockSpec(memory_space=pl.ANY),
                      pl.BlockSpec((ROW_TILE, 1), lambda t, te, nt, ids: (t, 0)),
                      pl.BlockSpec((None, d, hdim), wmap),
                      pl.BlockSpec((None, d, hdim), wmap),
                      pl.BlockSpec((None, hdim, d), wmap)],
            out_specs=pl.BlockSpec((ROW_TILE, s, l), lambda t, te, nt, ids: (t, 0, 0)),
            scratch_shapes=[pltpu.VMEM((2, ROW_TILE, d), F32), pltpu.SemaphoreType.DMA((2,))],
        ),
        out_shape=jax.ShapeDtypeStruct((p, s, l), F32),
        compiler_params=_cparams(("arbitrary",)),
        name="expert_mlp",
    )(tile_expert, n_tiles_used, row_ids, h, cw, w1, w3, w2)


def _moe_combine_kernel(mode, pos_ref, ys_ref, res_ref, gate_ref, g_ref, *refs):
    if mode == "next_norm":
        sh_ref, sc_ref, o_ref, h_ref, buf_ref, sem = refs
    else:
        o_ref, buf_ref, sem = refs
    i = pl.program_id(0)
    n = pl.num_programs(0)
    slot = i % 2

    def start(tile, s):
        for q in range(2):
            _start_row_gather(pos_ref, tile * (2 * ROW_TILE) + q, 2, ys_ref, buf_ref.at[s, q],
                              sem.at[s, q])

    @pl.when(i == 0)
    def _():
        start(0, 0)

    @pl.when(i + 1 < n)
    def _():
        start(i + 1, 1 - slot)

    for q in range(2):
        _wait_row_gather(ys_ref, buf_ref.at[slot, q], sem.at[slot, q])
    y = _from_row_major(buf_ref[slot, 0] + buf_ref[slot, 1], F32)
    x = res_ref[...] + gate_ref[...] * y
    if mode == "final_norm":
        ms = jnp.mean(x * x, axis=-1, keepdims=True)
        x = x * lax.rsqrt(ms + EPS) * g_ref[...]
    o_ref[...] = x
    if mode == "next_norm":
        ms = jnp.mean(x * x, axis=-1, keepdims=True)
        xn = x * lax.rsqrt(ms + EPS) * g_ref[...]
        h_ref[...] = (xn * (1.0 + sc_ref[...]) + sh_ref[...]).astype(h_ref.dtype)


def moe_combine(ys, pos, res, mods, layer, n_rows, n_tiles, row_fn, final_gain=None,
                next_gain=None):
    d = res.shape[1]
    _, s, l = ys.shape
    tile = pl.BlockSpec((ROW_TILE, d), lambda i, pos: (i, 0))
    vec = lambda m_layer, m: pl.BlockSpec(
        (None, 1, d), functools.partial(lambda f, i, pos: f(i), _mod_spec(m_layer, m, row_fn, n_rows, 1, 0)))
    mode, gain = "plain", jnp.ones((d,), F32)
    in_specs = [pl.BlockSpec(memory_space=pl.ANY), tile, vec(layer, 5),
                pl.BlockSpec((1, d), lambda i, pos: (0, 0))]
    out_specs, out_shape = tile, jax.ShapeDtypeStruct((n_tiles * ROW_TILE, d), F32)
    args = [mods]
    if final_gain is not None:
        mode, gain = "final_norm", final_gain
    elif next_gain is not None:
        mode, gain = "next_norm", next_gain
        in_specs += [vec(layer + 1, 0), vec(layer + 1, 1)]
        args += [mods, mods]
        out_specs = [tile, tile]
        out_shape = [out_shape, jax.ShapeDtypeStruct((n_tiles * ROW_TILE, d), BF16)]
    return pl.pallas_call(
        functools.partial(_moe_combine_kernel, mode),
        grid_spec=pltpu.PrefetchScalarGridSpec(
            num_scalar_prefetch=1,
            grid=(n_tiles,),
            in_specs=in_specs,
            out_specs=out_specs,
            scratch_shapes=[pltpu.VMEM((2, 2, ROW_TILE, s, l), F32),
                            pltpu.SemaphoreType.DMA((2, 2))],
        ),
        out_shape=out_shape,
        compiler_params=_cparams(("arbitrary",)),
        name="moe_combine",
    )(pos.reshape(-1), ys, res, args[0], gain.reshape(1, d), *args[1:])


def _dispatch_plan(route):
    t = route.shape[0]
    ids = route[:, 0:2].astype(jnp.int32)
    wts = route[:, 2:4]
    flat = ids.reshape(-1)
    n_assign = flat.shape[0]
    p_max = n_assign + N_EXPERTS * ROW_TILE
    p_max = (p_max // ROW_TILE) * ROW_TILE
    onehot = (flat[:, None] == jnp.arange(N_EXPERTS, dtype=jnp.int32)[None, :]).astype(jnp.int32)
    counts = jnp.sum(onehot, axis=0)
    padded = ((counts + ROW_TILE - 1) // ROW_TILE) * ROW_TILE
    ends = jnp.cumsum(padded)
    offs = ends - padded
    starts = jnp.cumsum(counts) - counts
    order = jnp.argsort(flat, stable=True).astype(jnp.int32)
    n_tiles = p_max // ROW_TILE
    tile_first = jnp.arange(n_tiles, dtype=jnp.int32) * ROW_TILE
    tile_e = jnp.sum((tile_first[:, None] >= ends[None, :]).astype(jnp.int32), axis=1)
    tile_e = jnp.minimum(tile_e, N_EXPERTS - 1)
    tile_rank = tile_first - offs[tile_e]
    within = jnp.arange(ROW_TILE, dtype=jnp.int32)[None, :]
    rank = tile_rank[:, None] + within
    valid = (rank < counts[tile_e][:, None]) & (tile_first[:, None] < ends[-1])
    src_sorted = jnp.clip(starts[tile_e][:, None] + rank, 0, n_assign - 1).reshape(-1)
    valid = valid.reshape(-1)
    assign = order[src_sorted]
    row_ids = jnp.where(valid, assign // 2, 0).astype(jnp.int32)
    cw = jnp.where(valid, wts.reshape(-1)[assign], 0.0).astype(F32)
    sorted_pos = jnp.argsort(order).astype(jnp.int32)
    shift = jnp.sum(onehot * (offs - starts)[None, :], axis=1)
    pos = (sorted_pos + shift).astype(jnp.int32).reshape(t, 2)
    n_tiles_used = (ends[-1] // ROW_TILE).astype(jnp.int32).reshape(1)
    last_e = tile_e[jnp.maximum(n_tiles_used[0] - 1, 0)]
    tile_idx = jnp.arange(n_tiles, dtype=jnp.int32)
    tile_expert = jnp.where(tile_idx < n_tiles_used[0], tile_e, last_e).astype(jnp.int32)
    return row_ids, cw, pos, tile_expert, n_tiles_used


def hier_moe_block(h, route, w1, w3, w2):
    row_ids, cw, pos, tile_expert, n_tiles_used = _dispatch_plan(route)
    ys = expert_mlp(h, row_ids, cw[:, None], w1, w3, w2, tile_expert, n_tiles_used)
    return ys, pos


def _rope_tables(geom):
    n_lat = geom.n_lat
    half = ATT_HEAD_DIM // 2
    nfreq = half // 2
    row = jnp.repeat(jnp.arange(n_lat // GRID_W, dtype=F32), GRID_W)
    col = (jnp.arange(n_lat) % GRID_W).astype(F32)
    inv = ROPE_THETA ** (-jnp.arange(nfreq, dtype=F32) / nfreq)
    ang = jnp.concatenate([row[:, None] * inv, col[:, None] * inv], axis=-1)
    cos, sin = jnp.cos(ang), jnp.sin(ang)
    cosf = jnp.concatenate([cos, cos], axis=-1)
    sinf = jnp.concatenate([-sin, sin], axis=-1)
    cosf = jnp.concatenate([jnp.ones((geom.lc, ATT_HEAD_DIM), F32), cosf], axis=0)
    sinf = jnp.concatenate([jnp.zeros((geom.lc, ATT_HEAD_DIM), F32), sinf], axis=0)
    return cosf, sinf


def _router_weights(router_g, router_e):
    d = router_g.shape[0]
    pad = jnp.zeros((d, LANES - MOE_GROUPS - N_EXPERTS), F32)
    return jnp.concatenate([router_g, router_e, pad], axis=1)


def mixer_even(xs, mods, layer, j, n_rows, geom, norm_g, w_in, conv_dw, conv_ln_g, conv_ln_b,
               ssm_conv_w, ssm_conv_b, ssm_dt_bias, ssm_a_log, ssm_d, ssm_norm_g, w_out,
               debug=False):
    bsz, per = geom.bsz, geom.lc + geom.n_lat
    conv_d = conv_dw.shape[1]
    ssm_dd = SSM_HEADS * SSM_HEAD_DIM
    xbc_dim = ssm_conv_w.shape[1]
    h = norm_mod(xs, norm_g, mods, layer, 0, n_rows, geom.n_all, geom.all_block, geom.all_mod_row)
    c0, c1, c2, c3 = conv_d, 2 * conv_d, 2 * conv_d + ssm_dd, 2 * conv_d + ssm_dd + xbc_dim
    glu = matmul_glu(h, w_in, 0, c0, conv_d, F32)
    z = matmul(h, w_in, c1, ssm_dd, BF16)
    xbc_raw = matmul(h, w_in, c2, xbc_dim, F32)
    w_dt = jnp.pad(w_in[:, c3:], ((0, 0), (0, LANES - 2 * SSM_HEADS)))
    dt_raw = matmul(h, w_dt, 0, LANES, F32)[:, :2 * SSM_HEADS]
    conf = seq_conv(glu, conv_dw, conv_ln_g, conv_ln_b, "layernorm_silu", geom)
    xbc = seq_conv(xbc_raw, ssm_conv_w, ssm_conv_b, ssm_conv_b, "bias_silu", geom)
    dtr = dt_raw.reshape(bsz, per, 2, SSM_GROUPS, 4).transpose(0, 2, 3, 4, 1)
    dtr = jnp.pad(dtr, ((0, 0), (0, 0), (0, 0), (0, 4), (0, 0)))

    def head_rows(v):
        v = v.astype(F32).reshape(2, SSM_GROUPS, 4)
        v = jnp.pad(v, ((0, 0), (0, 0), (0, 4)))
        return jnp.broadcast_to(v[..., None], (2, SSM_GROUPS, 8, LANES))

    y2 = ssd_scan(xbc, dtr, head_rows(ssm_dt_bias), head_rows(-jnp.exp(ssm_a_log.astype(F32))), geom)
    yn = ssd_merge(y2, xbc, z, jnp.repeat(ssm_d, SSM_HEAD_DIM), ssm_norm_g, geom)
    out = matmul_residual([conf, yn], w_out, xs, mods, layer, 2, n_rows, geom.n_all,
                          geom.all_block, geom.all_mod_row)
    if debug:
        return out, dict(h=h, glu=glu, z=z, xbc_raw=xbc_raw, dt_raw=dt_raw, conf=conf, xbc=xbc,
                         y2=y2, yn=yn)
    return out


def mixer_odd(xs, mods, layer, n_rows, geom, norm_g, w_qkv, q_norm_g, k_norm_g, w_o, h=None):
    if h is None:
        h = norm_mod(xs, norm_g, mods, layer, 0, n_rows, geom.n_all, geom.all_block,
                     geom.all_mod_row)
    qkv = matmul(h, w_qkv, 0, w_qkv.shape[1], BF16)
    cosf, sinf = _rope_tables(geom)
    q, k = qk_prep(qkv, cosf, sinf, q_norm_g, k_norm_g, geom)
    att = attention(q, k, qkv, geom)
    return matmul_residual([att], w_o, xs, mods, layer, 2, n_rows, geom.n_lat_tiles,
                           geom.lat_block, geom.lat_mod_row)


def moe_layer(xs, mods, layer, n_rows, n_tiles, row_fn, norm_g, router_g, router_e, w1, w3, w2,
              final_gain, next_gain=None):
    hm, route = norm_mod(xs, norm_g, mods, layer, 3, n_rows, n_tiles, lambda t: t, row_fn,
                         router_w=_router_weights(router_g, router_e))
    w1 = cast_experts(w1, layer)
    w3 = cast_experts(w3, layer)
    w2 = cast_experts(w2, layer)
    ys, pos = hier_moe_block(hm, route, w1, w3, w2)
    return moe_combine(ys, pos, xs, mods, layer, n_rows, n_tiles, row_fn, final_gain=final_gain,
                       next_gain=next_gain)


def kernel(x, c, ctx, c_ctx, w_ada, b_ada, norm_mix, norm_ffn, w_in0, conv_dw, conv_ln_g, conv_ln_b, ssm_conv_w, ssm_conv_b, ssm_dt_bias, ssm_a_log, ssm_d, ssm_norm_g, w_out0, w_qkv, q_norm_g, k_norm_g, w_o, moe_router_g, moe_router_e, moe_w1, moe_w3, moe_w2, norm_final):
    bsz, n_lat, d = x.shape
    lc = ctx.shape[1]
    depth = w_ada.shape[0]
    geom = Geom(bsz, lc, n_lat)
    n_rows = 16

    cvec = jnp.concatenate([c, c_ctx[None, :], jnp.zeros((n_rows - bsz - 1, d), F32)], axis=0)
    mods = ada_mods(cvec, w_ada, b_ada).reshape(depth * n_rows * N_MOD, 1, d)
    xs = (ctx.reshape(bsz * lc, d), x.reshape(bsz * n_lat, d), geom)

    h_next = None
    for i in range(depth):
        last = i == depth - 1
        j = i // 2
        if i % 2 == 0:
            if last:
                raise NotImplementedError("a conv/SSD mixer in the last layer is not supported")
            xs = mixer_even(xs, mods, i, j, n_rows, geom, norm_mix[i], w_in0[j], conv_dw[j],
                            conv_ln_g[j], conv_ln_b[j], ssm_conv_w[j], ssm_conv_b[j],
                            ssm_dt_bias[j], ssm_a_log[j], ssm_d[j], ssm_norm_g[j], w_out0[j])
            n_tiles, row_fn = geom.n_all, geom.all_mod_row
        else:
            if not last:
                raise NotImplementedError("an attention mixer before the last layer is not supported")
            xs = mixer_odd(xs, mods, i, n_rows, geom, norm_mix[i], w_qkv[j], q_norm_g[j],
                           k_norm_g[j], w_o[j], h=h_next)
            n_tiles, row_fn = geom.n_lat_tiles, geom.lat_mod_row
        feeds_next = not last and (i + 1) % 2 == 1
        out = moe_layer(xs, mods, i, n_rows, n_tiles, row_fn, norm_ffn[i], moe_router_g[i],
                        moe_router_e[i], moe_w1, moe_w3, moe_w2, norm_final if last else None,
                        next_gain=norm_mix[i + 1] if feeds_next else None)
        xs, h_next = out if feeds_next else (out, None)
    return xs.reshape(bsz, n_lat, d)
```
